```python
import math
import jax, jax.numpy as jnp
from jax import lax
import numpy as np

D_MODEL = 1024
BATCH = 8
SEQ = 8192
DEPTH = 1
DEC_BATCH = 128
DEC_SEQ = 8
PAST_LEN = 8192
PAGE_SIZE = 128

N_HEADS = 8
N_KV_HEADS = 4
KV_REP = N_HEADS // N_KV_HEADS
HEAD_DIM = 64
ATTN_WIDTH = N_HEADS * 2 * HEAD_DIM
KV_WIDTH = N_KV_HEADS * 2 * HEAD_DIM
ATTN_SCALE = HEAD_DIM ** -0.5
ROPE_THETA = 10000.0
Q_BLOCK = 128
SUBLN_EPS = 1e-5
D_INNER = 2 * D_MODEL
SSM_HEAD_DIM = 64
N_SSM_HEADS = D_INNER // SSM_HEAD_DIM
N_SSM_GROUPS = 4
SSM_HEADS_PER_GROUP = N_SSM_HEADS // N_SSM_GROUPS
D_STATE = 128
CONV_W = 4
CONV_DIM = D_INNER + 2 * N_SSM_GROUPS * D_STATE
SSD_CHUNK = 128
SSM_NORM_EPS = 1e-5
DT_MIN = 0.001
DT_MAX = 0.1
N_EXPERT_GROUPS = 4
EXPERTS_PER_GROUP = 8
N_EXPERTS = N_EXPERT_GROUPS * EXPERTS_PER_GROUP
TOP_K = 2
D_FF_EXPERT = 512
MOE_BLOCK = 256
EPS = 1e-6
_OFF_K = ATTN_WIDTH
_OFF_V = _OFF_K + KV_WIDTH
_OFF_Z = _OFF_V + KV_WIDTH
_OFF_XBC = _OFF_Z + D_INNER
_OFF_DT = _OFF_XBC + CONV_DIM
_OFF_GA = _OFF_DT + N_SSM_HEADS
_OFF_GS = _OFF_GA + D_MODEL
D_IN_PROJ = _OFF_GS + D_MODEL

kernel_name = 'hybrid_diffattn_ssd_hmoe_step'


def rmsnorm(x, g, eps=EPS):
    xf = x.astype(jnp.float32)
    y = xf * lax.rsqrt(jnp.mean(xf * xf, axis=-1, keepdims=True) + eps)
    return (y * g.astype(jnp.float32)).astype(x.dtype)


def group_rmsnorm(y, g, eps=SSM_NORM_EPS):
    shp = y.shape
    yf = y.astype(jnp.float32).reshape(shp[:-1] + (N_SSM_GROUPS, D_INNER // N_SSM_GROUPS))
    yf = yf * lax.rsqrt(jnp.mean(yf * yf, axis=-1, keepdims=True) + eps)
    return (yf.reshape(shp) * g.astype(jnp.float32)).astype(y.dtype)


def rope(x, pos):
    half = HEAD_DIM // 2
    inv = ROPE_THETA ** (-2.0 * jnp.arange(half, dtype=jnp.float32) / HEAD_DIM)
    ang = pos.astype(jnp.float32)[:, None] * inv[None, :]
    cos = jnp.cos(ang)[None, :, None, None, :]
    sin = jnp.sin(ang)[None, :, None, None, :]
    xf = x.astype(jnp.float32)
    x1, x2 = xf[..., :half], xf[..., half:]
    return jnp.concatenate([x1 * cos - x2 * sin, x2 * cos + x1 * sin], axis=-1).astype(x.dtype)


def diff_attn_block(q, qpos, k, v, kpos, lam):
    lq = q.shape[0]
    qg = q.reshape(lq, N_KV_HEADS, KV_REP, 2, HEAD_DIM)
    s = jnp.einsum('qgrcd,kgcd->cgrqk', qg, k).astype(jnp.float32) * ATTN_SCALE
    s = jnp.where(kpos[None, :] <= qpos[:, None], s, -jnp.inf)
    p = jax.nn.softmax(s, axis=-1)
    a = (p[0] - lam * p[1]).astype(v.dtype)
    o = jnp.einsum('grqk,kge->qgre', a, v)
    return o.reshape(lq, N_HEADS, 2 * HEAD_DIM)


def prompt_attend(q, k, v, lam, pos):
    s_len = q.shape[1]
    qblk = min(Q_BLOCK, s_len)
    nb = s_len // qblk

    def per_seq(args):
        qs, ks, vs = args
        qb = qs.reshape((nb, qblk) + qs.shape[1:])
        pb = pos.reshape(nb, qblk)
        ob = lax.map(lambda a: diff_attn_block(a[0], a[1], ks, vs, pos, lam), (qb, pb))
        return ob.reshape((s_len,) + ob.shape[2:])

    return lax.map(per_seq, (q, k, v))


def sample_attend(q, k, v, lam, pos, cache_k, cache_v, page_table, layer_idx):
    past = page_table.shape[1] * PAGE_SIZE
    kpos = jnp.arange(past + q.shape[1], dtype=jnp.int32)

    def per_seq(args):
        qs, ks, vs, pages = args
        kp = cache_k[layer_idx, pages].reshape(past, N_KV_HEADS, 2, HEAD_DIM).astype(ks.dtype)
        vp = cache_v[layer_idx, pages].reshape(past, N_KV_HEADS, 2 * HEAD_DIM).astype(vs.dtype)
        return diff_attn_block(qs, pos, jnp.concatenate([kp, ks], axis=0),
                               jnp.concatenate([vp, vs], axis=0), kpos, lam)

    return lax.map(per_seq, (q, k, v, page_table))


def causal_conv(xbc, conv_state, conv_w, conv_b):
    seq_len = xbc.shape[1]
    xp = jnp.concatenate([conv_state.astype(xbc.dtype), xbc], axis=1)
    y = conv_b + sum(xp[:, j:j + seq_len] * conv_w[j] for j in range(CONV_W))
    return jax.nn.silu(y), xp[:, seq_len:]


def ssd_scan(xh, dt, a_neg, bm, cm, h0):
    f32 = jnp.float32
    bsz, seq_len = xh.shape[0], xh.shape[1]
    q_len = min(SSD_CHUNK, seq_len)
    nc = -(-seq_len // q_len)
    pad = nc * q_len - seq_len
    G, R, P, N = N_SSM_GROUPS, SSM_HEADS_PER_GROUP, SSM_HEAD_DIM, D_STATE
    xdt = xh.astype(f32) * dt[..., None]
    a = dt * a_neg

    def chunks(t):
        t = jnp.pad(t, [(0, 0), (0, pad)] + [(0, 0)] * (t.ndim - 2))
        return jnp.moveaxis(t.reshape((bsz, nc, q_len) + t.shape[2:]), 1, 0)

    xc = chunks(xdt).reshape(nc, bsz, q_len, G, R, P)
    ac = chunks(a).reshape(nc, bsz, q_len, G, R)
    bc = chunks(bm.astype(f32))
    cc = chunks(cm.astype(f32))
    causal = jnp.tril(jnp.ones((q_len, q_len), bool))[None, :, :, None, None]

    def step(h, inp):
        x_c, a_c, b_c, c_c = inp
        acs = jnp.cumsum(a_c, axis=1)
        seg = acs[:, :, None] - acs[:, None, :]
        lmat = jnp.exp(jnp.where(causal, seg, -jnp.inf))
        cb = jnp.einsum('blgn,bsgn->blsg', c_c, b_c)
        y = (jnp.einsum('blsg,blsgr,bsgrp->blgrp', cb, lmat, x_c)
             + jnp.einsum('blgn,bgrpn->blgrp', c_c, h) * jnp.exp(acs)[..., None])
        decay_end = jnp.exp(acs[:, -1:] - acs)
        h = (jnp.exp(acs[:, -1])[..., None, None] * h
             + jnp.einsum('bsgn,bsgr,bsgrp->bgrpn', b_c, decay_end, x_c))
        return h, y

    h, ys = lax.scan(step, h0.astype(f32).reshape(bsz, G, R, P, N), (xc, ac, bc, cc))
    y = jnp.moveaxis(ys, 0, 1).reshape(bsz, nc * q_len, N_SSM_HEADS, P)[:, :seq_len]
    return y, h.reshape(bsz, N_SSM_HEADS, P, N)


def hmoe(u, w_rg, b_rg, w_re, b_re, w_gate, w_up, w_down):
    shp = u.shape
    xt = u.reshape(-1, shp[-1])
    T = xt.shape[0]
    g_logit = (xt @ w_rg + b_rg).astype(jnp.float32)
    g_sel = jnp.argmax(g_logit, axis=-1).astype(jnp.int32)
    p_g = jnp.take_along_axis(jax.nn.softmax(g_logit, axis=-1), g_sel[:, None], axis=1)
    e_logit = (xt @ w_re + b_re).astype(jnp.float32).reshape(T, N_EXPERT_GROUPS, EXPERTS_PER_GROUP)
    e_in = jnp.take_along_axis(e_logit, g_sel[:, None, None], axis=1)[:, 0]
    top_v, top_i = lax.top_k(e_in, TOP_K)
    wts = jax.nn.softmax(top_v, axis=-1) * p_g
    eid = g_sel[:, None] * EXPERTS_PER_GROUP + top_i.astype(jnp.int32)
    n_assign = T * TOP_K
    e_flat = eid.reshape(n_assign)
    w_flat = wts.reshape(n_assign)
    tok_flat = jnp.arange(n_assign, dtype=jnp.int32) // TOP_K
    order = jnp.argsort(e_flat)
    e_sorted = e_flat[order]
    counts = jnp.bincount(e_flat, length=N_EXPERTS).astype(jnp.int32)
    start = jnp.cumsum(counts) - counts
    rank = jnp.arange(n_assign, dtype=jnp.int32) - start[e_sorted]
    padded = (counts + MOE_BLOCK - 1) // MOE_BLOCK * MOE_BLOCK
    pend = jnp.cumsum(padded)
    dest = pend[e_sorted] - padded[e_sorted] + rank
    n_blocks = -(-n_assign // MOE_BLOCK) + N_EXPERTS
    n_slots = n_blocks * MOE_BLOCK
    slot_tok = jnp.full((n_slots,), T, jnp.int32).at[dest].set(tok_flat[order])
    slot_w = jnp.zeros((n_slots,), xt.dtype).at[dest].set(w_flat[order].astype(xt.dtype))
    blk_start = jnp.arange(n_blocks, dtype=pend.dtype) * MOE_BLOCK
    blk_e = jnp.minimum(jnp.searchsorted(pend, blk_start, side='right'), N_EXPERTS - 1)
    x_pad = jnp.concatenate([xt, jnp.zeros((1, xt.shape[1]), xt.dtype)], axis=0)

    def run_block(args):
        e, toks, w = args
        xb = x_pad[toks]
        h = jax.nn.silu(xb @ w_gate[e]) * (xb @ w_up[e])
        return (h @ w_down[e]) * w[:, None]

    out = lax.map(run_block, (blk_e, slot_tok.reshape(n_blocks, MOE_BLOCK),
                              slot_w.reshape(n_blocks, MOE_BLOCK)))
    y = jnp.zeros_like(x_pad).at[slot_tok].add(out.reshape(n_slots, -1).astype(xt.dtype))[:T]
    return y.reshape(shp)


def layer(x, c, pos, attend, conv_state, ssm_state, lam_init, lp):
    f32 = jnp.float32
    bsz, seq_len = x.shape[0], x.shape[1]
    mod = (jax.nn.silu(c) @ lp['w_ada'] + lp['b_ada'])[:, None, :]
    sh1, sc1, gt1, sh2, sc2, gt2 = jnp.split(mod, 6, axis=-1)
    u = rmsnorm(x, lp['g_pre1']) * (1 + sc1) + sh1
    proj = u @ lp['w_in']
    q, k, v, z, xbc, dt_raw, g_a, g_s = jnp.split(
        proj, [_OFF_K, _OFF_V, _OFF_Z, _OFF_XBC, _OFF_DT, _OFF_GA, _OFF_GS], axis=-1)
    q = rope(q.reshape(bsz, seq_len, N_HEADS, 2, HEAD_DIM), pos)
    k = rope(k.reshape(bsz, seq_len, N_KV_HEADS, 2, HEAD_DIM), pos)
    v = v.reshape(bsz, seq_len, N_KV_HEADS, 2 * HEAD_DIM)
    lam = (jnp.exp(jnp.sum(lp['lambda_q1'].astype(f32) * lp['lambda_k1'].astype(f32)))
           - jnp.exp(jnp.sum(lp['lambda_q2'].astype(f32) * lp['lambda_k2'].astype(f32))) + lam_init)
    o = attend(q, k, v, lam)
    o = rmsnorm(o, lp['g_subln'], SUBLN_EPS) * (1.0 - lam_init)
    y_a = o.reshape(bsz, seq_len, ATTN_WIDTH) @ lp['w_oa']
    xbc, new_conv = causal_conv(xbc, conv_state, lp['conv_w'], lp['conv_b'])
    xs, bs, cs = jnp.split(xbc, [D_INNER, D_INNER + N_SSM_GROUPS * D_STATE], axis=-1)
    xs = xs.reshape(bsz, seq_len, N_SSM_HEADS, SSM_HEAD_DIM)
    dt = jax.nn.softplus(dt_raw.astype(f32) + lp['dt_bias'].astype(f32))
    a_neg = -jnp.exp(lp['a_log'].astype(f32))
    y_h, h_new = ssd_scan(xs, dt, a_neg,
                          bs.reshape(bsz, seq_len, N_SSM_GROUPS, D_STATE),
                          cs.reshape(bsz, seq_len, N_SSM_GROUPS, D_STATE), ssm_state)
    y_h = y_h + lp['d_skip'].astype(f32)[:, None] * xs.astype(f32)
    y_s = y_h.reshape(bsz, seq_len, D_INNER).astype(x.dtype) * jax.nn.silu(z)
    y_s = group_rmsnorm(y_s, lp['g_ssm']) @ lp['w_os']
    m = jax.nn.sigmoid(g_a) * y_a + jax.nn.sigmoid(g_s) * y_s
    x = x + gt1 * rmsnorm(m @ lp['w_o'], lp['g_post1'])
    u2 = rmsnorm(x, lp['g_pre2']) * (1 + sc2) + sh2
    f = hmoe(u2, lp['w_rg'], lp['b_rg'], lp['w_re'], lp['b_re'], lp['w_gate'], lp['w_up'], lp['w_down'])
    x = x + gt2 * rmsnorm(f, lp['g_post2'])
    return (x, k.reshape(bsz, seq_len, N_KV_HEADS, 2 * HEAD_DIM), v, new_conv,
            h_new.astype(ssm_state.dtype))


def setup_inputs(seed: int = 0) -> dict:
    key = jax.random.key(seed)
    ks = iter(jax.random.split(key, 48))
    f32 = jnp.float32

    def nrm(shape, scale=1.0):
        return jax.random.normal(next(ks), shape, f32) * scale

    def gain(shape):
        return 1.0 + nrm(shape, 0.02)

    n_pages = PAST_LEN // PAGE_SIZE
    n_used = DEC_BATCH * n_pages
    n_pool = n_used + max(1, n_used // 4)
    page_table = jax.random.permutation(next(ks), n_pool)[:n_used].reshape(DEC_BATCH, n_pages).astype(jnp.int32)
    u_dt = jax.random.uniform(next(ks), (DEPTH, N_SSM_HEADS), f32)
    dt0 = jnp.exp(u_dt * (math.log(DT_MAX) - math.log(DT_MIN)) + math.log(DT_MIN))
    dt_bias = dt0 + jnp.log(-jnp.expm1(-dt0))
    a_log = jnp.log(jax.random.uniform(next(ks), (DEPTH, N_SSM_HEADS), f32, 1.0, 16.0))
    return {
        'x_prompt': nrm((BATCH, SEQ, D_MODEL)),
        'x_sample': nrm((DEC_BATCH, DEC_SEQ, D_MODEL)),
        'cache_k': nrm((DEPTH, n_pool, PAGE_SIZE, N_KV_HEADS, 2 * HEAD_DIM)),
        'cache_v': nrm((DEPTH, n_pool, PAGE_SIZE, N_KV_HEADS, 2 * HEAD_DIM)),
        'state_conv': nrm((DEPTH, DEC_BATCH, CONV_W - 1, CONV_DIM)),
        'state_ssm': nrm((DEPTH, DEC_BATCH, N_SSM_HEADS, SSM_HEAD_DIM, D_STATE), 0.1),
        'page_table': page_table,
        'c_prompt': nrm((BATCH, D_MODEL)),
        'c_sample': nrm((DEC_BATCH, D_MODEL)),
        'w_ada': nrm((DEPTH, D_MODEL, 6 * D_MODEL), 0.5 * D_MODEL ** -0.5),
        'b_ada': nrm((DEPTH, 6 * D_MODEL), 0.02),
        'g_pre1': gain((DEPTH, D_MODEL)),
        'g_post1': gain((DEPTH, D_MODEL)),
        'g_pre2': gain((DEPTH, D_MODEL)),
        'g_post2': gain((DEPTH, D_MODEL)),
        'w_in': nrm((DEPTH, D_MODEL, D_IN_PROJ), D_MODEL ** -0.5),
        'lambda_q1': nrm((DEPTH, HEAD_DIM), 0.1),
        'lambda_k1': nrm((DEPTH, HEAD_DIM), 0.1),
        'lambda_q2': nrm((DEPTH, HEAD_DIM), 0.1),
        'lambda_k2': nrm((DEPTH, HEAD_DIM), 0.1),
        'g_subln': gain((DEPTH, 2 * HEAD_DIM)),
        'w_oa': nrm((DEPTH, ATTN_WIDTH, D_MODEL), ATTN_WIDTH ** -0.5),
        'conv_w': nrm((DEPTH, CONV_W, CONV_DIM), CONV_W ** -0.5),
        'conv_b': nrm((DEPTH, CONV_DIM), 0.02),
        'dt_bias': dt_bias,
        'a_log': a_log,
        'd_skip': gain((DEPTH, N_SSM_HEADS)),
        'g_ssm': gain((DEPTH, D_INNER)),
        'w_os': nrm((DEPTH, D_INNER, D_MODEL), D_INNER ** -0.5),
        'w_o': nrm((DEPTH, D_MODEL, D_MODEL), D_MODEL ** -0.5),
        'w_rg': nrm((DEPTH, D_MODEL, N_EXPERT_GROUPS), D_MODEL ** -0.5),
        'b_rg': nrm((DEPTH, N_EXPERT_GROUPS), 0.01),
        'w_re': nrm((DEPTH, D_MODEL, N_EXPERTS), D_MODEL ** -0.5),
        'b_re': nrm((DEPTH, N_EXPERTS), 0.01),
        'w_gate': nrm((DEPTH, N_EXPERTS, D_MODEL, D_FF_EXPERT), D_MODEL ** -0.5),
        'w_up': nrm((DEPTH, N_EXPERTS, D_MODEL, D_FF_EXPERT), D_MODEL ** -0.5),
        'w_down': nrm((DEPTH, N_EXPERTS, D_FF_EXPERT, D_MODEL), D_FF_EXPERT ** -0.5),
    }


def reference(x_prompt, x_sample, cache_k, cache_v, state_conv, state_ssm, page_table,
              c_prompt, c_sample, w_ada, b_ada, g_pre1, g_post1, g_pre2, g_post2, w_in,
              lambda_q1, lambda_k1, lambda_q2, lambda_k2, g_subln, w_oa, conv_w, conv_b,
              dt_bias, a_log, d_skip, g_ssm, w_os, w_o, w_rg, b_rg, w_re, b_re,
              w_gate, w_up, w_down):
    weights = dict(w_ada=w_ada, b_ada=b_ada, g_pre1=g_pre1, g_post1=g_post1, g_pre2=g_pre2,
                   g_post2=g_post2, w_in=w_in, lambda_q1=lambda_q1, lambda_k1=lambda_k1,
                   lambda_q2=lambda_q2, lambda_k2=lambda_k2, g_subln=g_subln, w_oa=w_oa,
                   conv_w=conv_w, conv_b=conv_b, dt_bias=dt_bias, a_log=a_log, d_skip=d_skip,
                   g_ssm=g_ssm, w_os=w_os, w_o=w_o, w_rg=w_rg, b_rg=b_rg, w_re=w_re, b_re=b_re,
                   w_gate=w_gate, w_up=w_up, w_down=w_down)
    bp, bs = x_prompt.shape[0], x_sample.shape[0]
    pos_p = jnp.arange(x_prompt.shape[1], dtype=jnp.int32)
    past = page_table.shape[1] * PAGE_SIZE
    pos_s = past + jnp.arange(x_sample.shape[1], dtype=jnp.int32)
    hp, hs = x_prompt, x_sample
    kp_l, vp_l, cp_l, sp_l, ks_l, vs_l, cs_l, ss_l = [], [], [], [], [], [], [], []
    for l in range(DEPTH):
        lp = {name: w[l] for name, w in weights.items()}
        lam_init = 0.8 - 0.6 * math.exp(-0.3 * l)
        conv0 = jnp.zeros((bp, CONV_W - 1, CONV_DIM), x_prompt.dtype)
        ssm0 = jnp.zeros((bp, N_SSM_HEADS, SSM_HEAD_DIM, D_STATE), x_prompt.dtype)
        hp, kp, vp, cp, sp = layer(
            hp, c_prompt, pos_p, lambda q, k, v, lam: prompt_attend(q, k, v, lam, pos_p),
            conv0, ssm0, lam_init, lp)
        hs, ksm, vsm, csm, ssm = layer(
            hs, c_sample, pos_s,
            lambda q, k, v, lam: sample_attend(q, k, v, lam, pos_s, cache_k, cache_v, page_table, l),
            state_conv[l], state_ssm[l], lam_init, lp)
        kp_l.append(kp); vp_l.append(vp); cp_l.append(cp); sp_l.append(sp)
        ks_l.append(ksm); vs_l.append(vsm); cs_l.append(csm); ss_l.append(ssm)
    k_prompt = jnp.stack(kp_l)
    v_prompt = jnp.stack(vp_l)
    conv_prompt = jnp.stack(cp_l)
    ssm_prompt = jnp.stack(sp_l)
    k_sample = jnp.stack(ks_l)
    v_sample = jnp.stack(vs_l)
    conv_sample = jnp.stack(cs_l)
    ssm_sample = jnp.stack(ss_l)
    return (hp, hs, k_prompt, v_prompt, conv_prompt, ssm_prompt, k_sample, v_sample, conv_sample, ssm_sample)
```

```python
import functools
import math

import jax
import jax.numpy as jnp
from jax import lax
from jax.experimental import pallas as pl
from jax.experimental.pallas import tpu as pltpu

f32 = jnp.float32
bf16 = jnp.bfloat16
i32 = jnp.int32

D_MODEL = 1024
N_HEADS = 8
N_KV = 4
HEAD_DIM = 64
HEAD_W = 2 * HEAD_DIM
ATTN_W = N_HEADS * HEAD_W
KV_W = N_KV * HEAD_W
ATTN_SCALE = HEAD_DIM ** -0.5
ROPE_THETA = 10000.0
SUBLN_EPS = 1e-5
D_INNER = 2048
SSM_P = 64
SSM_H = D_INNER // SSM_P
SSM_G = 4
SSM_GW = D_INNER // SSM_G
D_STATE = 128
CONV_W = 4
CONV_DIM = D_INNER + 2 * SSM_G * D_STATE
SSD_CHUNK = 128
SSM_EPS = 1e-5
N_EGROUPS = 4
E_PER_GROUP = 8
N_EXPERTS = 32
D_FF = 512
EPS = 1e-6
OFF_K = ATTN_W
OFF_V = OFF_K + KV_W
OFF_Z = OFF_V + KV_W
OFF_XBC = OFF_Z + D_INNER
OFF_DT = OFF_XBC + CONV_DIM
OFF_GA = OFF_DT + SSM_H
OFF_GS = OFF_GA + D_MODEL

LANES = 128
SUBLANES = 8
VMEM_LIMIT = 56 * 1024 * 1024
NEG = -1e30

TOKEN_TILE = 512
ATTN_TILE = 512
PAGES_PER_STEP = 8
MOE_BLOCK = 512
DISPATCH_TILE = 512


def _params(sem, vmem=VMEM_LIMIT):
    return pltpu.CompilerParams(dimension_semantics=sem, vmem_limit_bytes=vmem)


def _sigmoid(x):
    return 1.0 / (1.0 + jnp.exp(-x))


def _silu(x):
    return x * _sigmoid(x)


def _rms(x, g, eps):
    return x * lax.rsqrt(jnp.mean(x * x, axis=-1, keepdims=True) + eps) * g


def _bdot(a, b):
    return jnp.dot(a.astype(bf16), b.astype(bf16), preferred_element_type=f32)


class _Group:
    def __init__(self, n_seq, seq_len, mod_rows, rope_rows, per_token):
        self.n_seq, self.seq_len = n_seq, seq_len
        self.T = n_seq * seq_len
        self.tm = min(TOKEN_TILE, self.T if per_token else seq_len)
        assert self.T % self.tm == 0
        self.per_token = per_token
        if per_token:
            self.mod = mod_rows
            self.rope = rope_rows
        else:
            assert seq_len % self.tm == 0
            self.tiles_per_seq = seq_len // self.tm
            self.mod = mod_rows.reshape(n_seq, 1, 6 * D_MODEL)
            self.rope = rope_rows
        self.n_tiles = self.T // self.tm

    def mod_spec(self, col):
        if self.per_token:
            return pl.BlockSpec((self.tm, D_MODEL), lambda i: (i, col))
        tps = self.tiles_per_seq
        return pl.BlockSpec((None, 1, D_MODEL), lambda i: (i // tps, 0, col))

    def rope_spec(self):
        if self.per_token:
            return pl.BlockSpec((self.tm, LANES), lambda i: (i, 0))
        tps = self.tiles_per_seq
        return pl.BlockSpec((self.tm, LANES), lambda i: (i % tps, 0))

    def row_spec(self, width):
        return pl.BlockSpec((self.tm, width), lambda i: (i, 0))


def _const_spec(shape):
    nd = len(shape)
    return pl.BlockSpec(shape, lambda i: (0,) * nd)


def _ada_kernel(c_ref, w_ref, b_ref, o_ref):
    o_ref[...] = _bdot(_silu(c_ref[...]), w_ref[...]) + b_ref[...]


def _ada(c, w_ada, b_ada):
    m = c.shape[0]
    n = w_ada.shape[1]
    tn = D_MODEL
    return pl.pallas_call(
        _ada_kernel,
        out_shape=jax.ShapeDtypeStruct((m, n), f32),
        grid=(n // tn,),
        in_specs=[pl.BlockSpec((m, D_MODEL), lambda j: (0, 0)),
                  pl.BlockSpec((D_MODEL, tn), lambda j: (0, j)),
                  pl.BlockSpec((1, tn), lambda j: (0, j))],
        out_specs=pl.BlockSpec((m, tn), lambda j: (0, j)),
        compiler_params=_params(("parallel",)),
        name="ada_mod",
    )(c, w_ada, b_ada.reshape(1, n))


def _modulated(x_ref, g_ref, sc_ref, sh_ref):
    u = _rms(x_ref[...], g_ref[...], EPS) * (1.0 + sc_ref[...]) + sh_ref[...]
    return u.astype(bf16)


def _rope_cols(x, n_blocks, cos, sin_signed, lo):
    out = []
    for j in range(n_blocks):
        xb = x[:, j * LANES:(j + 1) * LANES]
        partner = jnp.where(lo, pltpu.roll(xb, LANES - HEAD_DIM // 2, 1), pltpu.roll(xb, HEAD_DIM // 2, 1))
        out.append(xb * cos + partner * sin_signed)
    return jnp.concatenate(out, axis=1)


def _qkv_kernel(x_ref, g_ref, sc_ref, sh_ref, w_ref, cos_ref, sin_ref,
                q_ref, k_ref, kb_ref, v_ref, vb_ref):
    u = _modulated(x_ref, g_ref, sc_ref, sh_ref)
    proj = jnp.dot(u, w_ref[...], preferred_element_type=f32)
    cos, sin_signed = cos_ref[...], sin_ref[...]
    lane = lax.broadcasted_iota(i32, cos.shape, 1)
    lo = jnp.bitwise_and(lane, HEAD_DIM - 1) < HEAD_DIM // 2
    q = _rope_cols(proj[:, :OFF_K], N_HEADS, cos, sin_signed, lo)
    k = _rope_cols(proj[:, OFF_K:OFF_V], N_KV, cos, sin_signed, lo)
    v = proj[:, OFF_V:OFF_Z]
    q_ref[...] = (q * ATTN_SCALE).astype(bf16)
    k_ref[...] = k
    kb_ref[...] = k.astype(bf16)
    v_ref[...] = v
    vb_ref[...] = v.astype(bf16)


def _proj_qkv(grp, x2, g, w_qkv, col_scale, col_shift):
    T, tm = grp.T, grp.tm
    return pl.pallas_call(
        _qkv_kernel,
        out_shape=(jax.ShapeDtypeStruct((T, ATTN_W), bf16),
                   jax.ShapeDtypeStruct((T, KV_W), f32), jax.ShapeDtypeStruct((T, KV_W), bf16),
                   jax.ShapeDtypeStruct((T, KV_W), f32), jax.ShapeDtypeStruct((T, KV_W), bf16)),
        grid=(grp.n_tiles,),
        in_specs=[grp.row_spec(D_MODEL), _const_spec((1, D_MODEL)),
                  grp.mod_spec(col_scale), grp.mod_spec(col_shift),
                  _const_spec(w_qkv.shape), grp.rope_spec(), grp.rope_spec()],
        out_specs=(grp.row_spec(ATTN_W), grp.row_spec(KV_W), grp.row_spec(KV_W),
                   grp.row_spec(KV_W), grp.row_spec(KV_W)),
        compiler_params=_params(("parallel",)),
        name="proj_qkv",
    )(x2, g, grp.mod, grp.mod, w_qkv, grp.rope[0], grp.rope[1])


def _zx_kernel(x_ref, g_ref, sc_ref, sh_ref, wz_ref, wx_ref, zs_ref, xbc_ref):
    u = _modulated(x_ref, g_ref, sc_ref, sh_ref)
    zs_ref[...] = _silu(jnp.dot(u, wz_ref[...], preferred_element_type=f32)).astype(bf16)
    xbc_ref[...] = jnp.dot(u, wx_ref[...], preferred_element_type=f32).astype(bf16)


def _proj_zx(grp, x2, g, w_z, w_xbc, col_scale, col_shift):
    T = grp.T
    return pl.pallas_call(
        _zx_kernel,
        out_shape=(jax.ShapeDtypeStruct((T, D_INNER), bf16), jax.ShapeDtypeStruct((T, CONV_DIM), bf16)),
        grid=(grp.n_tiles,),
        in_specs=[grp.row_spec(D_MODEL), _const_spec((1, D_MODEL)),
                  grp.mod_spec(col_scale), grp.mod_spec(col_shift),
                  _const_spec(w_z.shape), _const_spec(w_xbc.shape)],
        out_specs=(grp.row_spec(D_INNER), grp.row_spec(CONV_DIM)),
        compiler_params=_params(("parallel",)),
        name="proj_zx",
    )(x2, g, grp.mod, grp.mod, w_z, w_xbc)


def _gates_kernel(x_ref, g_ref, sc_ref, sh_ref, wg_ref, wdt_ref, dtb_ref, ga_ref, gs_ref, dt_ref):
    u = _modulated(x_ref, g_ref, sc_ref, sh_ref)
    gates = _sigmoid(jnp.dot(u, wg_ref[...], preferred_element_type=f32))
    ga_ref[...] = gates[:, :D_MODEL].astype(bf16)
    gs_ref[...] = gates[:, D_MODEL:].astype(bf16)
    raw = jnp.dot(u, wdt_ref[...], preferred_element_type=f32) + dtb_ref[...]
    dt_ref[...] = jnp.maximum(raw, 0.0) + jnp.log1p(jnp.exp(-jnp.abs(raw)))


def _proj_gates(grp, x2, g, w_gates, w_dt, dt_bias, col_scale, col_shift):
    T = grp.T
    return pl.pallas_call(
        _gates_kernel,
        out_shape=(jax.ShapeDtypeStruct((T, D_MODEL), bf16), jax.ShapeDtypeStruct((T, D_MODEL), bf16),
                   jax.ShapeDtypeStruct((T, LANES), f32)),
        grid=(grp.n_tiles,),
        in_specs=[grp.row_spec(D_MODEL), _const_spec((1, D_MODEL)),
                  grp.mod_spec(col_scale), grp.mod_spec(col_shift),
                  _const_spec(w_gates.shape), _const_spec(w_dt.shape), _const_spec((1, LANES))],
        out_specs=(grp.row_spec(D_MODEL), grp.row_spec(D_MODEL), grp.row_spec(LANES)),
        compiler_params=_params(("parallel",)),
        name="proj_gates",
    )(x2, g, grp.mod, grp.mod, w_gates, w_dt, dt_bias)


def _lambda_full(lam_ref, lam_init):
    lp = lam_ref[...]
    l1 = jnp.sum(lp[0:1] * lp[1:2], axis=1, keepdims=True)
    l2 = jnp.sum(lp[2:3] * lp[3:4], axis=1, keepdims=True)
    return jnp.exp(l1) - jnp.exp(l2) + lam_init


def _softmax_step(s, vc, m_ref, l_ref, acc_ref):
    reps = s.shape[1] // LANES if s.shape[1] % LANES == 0 else 0
    m_prev = m_ref[...]
    m_new = jnp.maximum(m_prev, jnp.max(s, axis=1, keepdims=True))
    alpha = jnp.exp(m_prev - m_new)
    if reps:
        p = jnp.exp(s - jnp.concatenate([m_new] * reps, axis=1))
    else:
        p = jnp.exp(s - m_new[:, 0:1])
    l_ref[...] = alpha * l_ref[...] + jnp.sum(p, axis=1, keepdims=True)
    acc_ref[...] = alpha * acc_ref[...] + jnp.dot(p.astype(bf16), vc, preferred_element_type=f32)
    m_ref[...] = m_new


def _diff_heads(acc, l, lam, gsub, lam_init, rows):
    outs = []
    for r in range(2):
        o0 = acc[(2 * r) * rows:(2 * r + 1) * rows] / l[(2 * r) * rows:(2 * r + 1) * rows]
        o1 = acc[(2 * r + 1) * rows:(2 * r + 2) * rows] / l[(2 * r + 1) * rows:(2 * r + 2) * rows]
        o = o0 - lam * o1
        outs.append(_rms(o, gsub, SUBLN_EPS) * (1.0 - lam_init))
    return jnp.concatenate(outs, axis=1)


def _pad_components(q, rows):
    lane = lax.broadcasted_iota(i32, (rows, LANES), 1)
    lo = lane < HEAD_DIM
    blocks = []
    for r in range(2):
        qh = q[:, r * LANES:(r + 1) * LANES].astype(f32)
        blocks.append(jnp.where(lo, qh, 0.0))
        blocks.append(jnp.where(lo, 0.0, qh))
    return jnp.concatenate(blocks, axis=0)


def _stacked_causal(n):
    assert n & (n - 1) == 0
    row = jnp.bitwise_and(lax.broadcasted_iota(i32, (4 * n, n), 0), n - 1)
    col = lax.broadcasted_iota(i32, (4 * n, n), 1)
    return col <= row


def _attn_prompt_kernel(lam_ref, gsub_ref, q_ref, k_ref, v_ref, o_ref, qpad, m_s, l_s, acc_s, *, tq, lam_init):
    i = pl.program_id(2)
    qpad[...] = _pad_components(q_ref[0], tq).astype(bf16)
    m_s[...] = jnp.full(m_s.shape, NEG, f32)
    l_s[...] = jnp.zeros(l_s.shape, f32)
    acc_s[...] = jnp.zeros(acc_s.shape, f32)

    def scores(off):
        kc = k_ref[0, pl.ds(off, tq), :]
        return lax.dot_general(qpad[...], kc, (((1,), (1,)), ((), ())), preferred_element_type=f32)

    def body(j, carry):
        off = pl.multiple_of(j * tq, tq)
        _softmax_step(scores(off), v_ref[0, pl.ds(off, tq), :], m_s, l_s, acc_s)
        return carry

    lax.fori_loop(0, i, body, 0)
    off = pl.multiple_of(i * tq, tq)
    s = jnp.where(_stacked_causal(tq), scores(off), NEG)
    _softmax_step(s, v_ref[0, pl.ds(off, tq), :], m_s, l_s, acc_s)
    lam = _lambda_full(lam_ref, lam_init)
    o_ref[0] = _diff_heads(acc_s[...], l_s[...], lam, gsub_ref[...], lam_init, tq).astype(bf16)


def _attn_prompt(q, kb, vb, lam_p, gsub, lam_init):
    B, L, _ = q.shape
    tq = min(ATTN_TILE, L)
    assert L % tq == 0
    kern = functools.partial(_attn_prompt_kernel, tq=tq, lam_init=lam_init)
    return pl.pallas_call(
        kern,
        out_shape=jax.ShapeDtypeStruct((B, L, ATTN_W), bf16),
        grid=(B, N_KV, L // tq),
        in_specs=[pl.BlockSpec((4, HEAD_DIM), lambda b, g, i: (0, 0)),
                  pl.BlockSpec((1, HEAD_W), lambda b, g, i: (0, 0)),
                  pl.BlockSpec((1, tq, 2 * HEAD_W), lambda b, g, i: (b, i, g)),
                  pl.BlockSpec((1, L, HEAD_W), lambda b, g, i: (b, 0, g)),
                  pl.BlockSpec((1, L, HEAD_W), lambda b, g, i: (b, 0, g))],
        out_specs=pl.BlockSpec((1, tq, 2 * HEAD_W), lambda b, g, i: (b, i, g)),
        scratch_shapes=[pltpu.VMEM((4 * tq, LANES), bf16), pltpu.VMEM((4 * tq, LANES), f32),
                        pltpu.VMEM((4 * tq, LANES), f32), pltpu.VMEM((4 * tq, LANES), f32)],
        compiler_params=_params(("parallel", "parallel", "arbitrary")),
        name="attn_prompt",
    )(lam_p, gsub, q, kb, vb)


def _attn_sample_kernel(pt_ref, lam_ref, gsub_ref, q_ref, kn_ref, vn_ref, *rest, n_pages_step, seq_new, lam_init):
    del pt_ref
    kp = rest[:n_pages_step]
    vp = rest[n_pages_step:2 * n_pages_step]
    o_ref, qpad, m_s, l_s, acc_s = rest[2 * n_pages_step:]
    j = pl.program_id(1)

    @pl.when(j == 0)
    def _():
        q = q_ref[0]
        for g in range(N_KV):
            qpad[g] = _pad_components(q[:, g * 2 * HEAD_W:(g + 1) * 2 * HEAD_W], seq_new)
        m_s[...] = jnp.full(m_s.shape, NEG, f32)
        l_s[...] = jnp.zeros(l_s.shape, f32)
        acc_s[...] = jnp.zeros(acc_s.shape, f32)

    for g in range(N_KV):
        qg = qpad[g].astype(bf16)
        kg = jnp.concatenate([r[:, g * HEAD_W:(g + 1) * HEAD_W].astype(bf16) for r in kp], axis=0)
        vg = jnp.concatenate([r[:, g * HEAD_W:(g + 1) * HEAD_W].astype(bf16) for r in vp], axis=0)
        s = lax.dot_general(qg, kg, (((1,), (1,)), ((), ())), preferred_element_type=f32)
        _softmax_step(s, vg, m_s.at[g], l_s.at[g], acc_s.at[g])

    @pl.when(j == pl.num_programs(1) - 1)
    def _():
        lam = _lambda_full(lam_ref, lam_init)
        causal = _stacked_causal(seq_new)
        outs = []
        for g in range(N_KV):
            qg = qpad[g].astype(bf16)
            kg = kn_ref[0][:, g * HEAD_W:(g + 1) * HEAD_W]
            vg = vn_ref[0][:, g * HEAD_W:(g + 1) * HEAD_W]
            s = lax.dot_general(qg, kg, (((1,), (1,)), ((), ())), preferred_element_type=f32)
            _softmax_step(jnp.where(causal, s, NEG), vg, m_s.at[g], l_s.at[g], acc_s.at[g])
            outs.append(_diff_heads(acc_s[g], l_s[g], lam, gsub_ref[...], lam_init, seq_new))
        o_ref[0] = jnp.concatenate(outs, axis=1).astype(bf16)


def _attn_sample(q, kb, vb, cache_k, cache_v, page_table, layer, lam_p, gsub, lam_init):
    S, Ls, _ = q.shape
    n_pages = page_table.shape[1]
    page = cache_k.shape[2]
    pps = PAGES_PER_STEP
    while n_pages % pps:
        pps -= 1
    steps = n_pages // pps

    def page_spec(p):
        return pl.BlockSpec((None, None, page, KV_W), lambda s, j, pt: (layer, pt[s, j * pps + p], 0, 0))

    kern = functools.partial(_attn_sample_kernel, n_pages_step=pps, seq_new=Ls, lam_init=lam_init)
    grid_spec = pltpu.PrefetchScalarGridSpec(
        num_scalar_prefetch=1,
        grid=(S, steps),
        in_specs=[pl.BlockSpec((4, HEAD_DIM), lambda s, j, pt: (0, 0)),
                  pl.BlockSpec((1, HEAD_W), lambda s, j, pt: (0, 0)),
                  pl.BlockSpec((1, Ls, ATTN_W), lambda s, j, pt: (s, 0, 0)),
                  pl.BlockSpec((1, Ls, KV_W), lambda s, j, pt: (s, 0, 0)),
                  pl.BlockSpec((1, Ls, KV_W), lambda s, j, pt: (s, 0, 0))]
        + [page_spec(p) for p in range(pps)] + [page_spec(p) for p in range(pps)],
        out_specs=pl.BlockSpec((1, Ls, ATTN_W), lambda s, j, pt: (s, 0, 0)),
        scratch_shapes=[pltpu.VMEM((N_KV, 4 * Ls, LANES), f32), pltpu.VMEM((N_KV, 4 * Ls, LANES), f32),
                        pltpu.VMEM((N_KV, 4 * Ls, LANES), f32), pltpu.VMEM((N_KV, 4 * Ls, LANES), f32)],
    )
    return pl.pallas_call(
        kern,
        out_shape=jax.ShapeDtypeStruct((S, Ls, ATTN_W), bf16),
        grid_spec=grid_spec,
        compiler_params=_params(("parallel", "arbitrary")),
        name="attn_sample",
    )(page_table, lam_p, gsub, q, kb, vb, *([cache_k] * pps), *([cache_v] * pps))


def _split3(x):
    hi = x.astype(bf16)
    r1 = x - hi.astype(f32)
    mid = r1.astype(bf16)
    lo = (r1 - mid.astype(f32)).astype(bf16)
    return hi, mid, lo


def _ssd_kernel(xbc_ref, zs_ref, dt_ref, dtT_ref, conv0_ref, ssm0_ref, cw_ref, cb_ref,
                alog_ref, alogT_ref, dskip_ref, gssm_ref, e_ref,
                y_ref, convo_ref, ssmo_ref, hT, cbuf, *, Q):
    c = pl.program_id(1)
    tail = CONV_W - 1
    base = SUBLANES

    @pl.when(c == 0)
    def _():
        cbuf[base - tail:base, :] = conv0_ref[0]
        for g in range(SSM_G):
            hT[g] = ssm0_ref[0, g * SSM_GW:(g + 1) * SSM_GW, :].T

    cbuf[base:base + Q, :] = xbc_ref[0].astype(f32)
    acc = cb_ref[...] + cw_ref[CONV_W - 1:CONV_W, :] * cbuf[base:base + Q, :]
    for j in range(tail):
        acc = acc + cw_ref[j:j + 1, :] * cbuf[pl.ds(base - tail + j, Q), :]
    xc = _silu(acc)
    convo_ref[0] = cbuf[pl.ds(base + Q - tail, tail), :]
    cbuf[0:SUBLANES, :] = cbuf[Q:Q + SUBLANES, :]

    xs = xc[:, :D_INNER]
    xs_b = xs.astype(bf16)
    e_mat = e_ref[...]

    dt = dt_ref[0]
    dtT = dtT_ref[0]
    a = dt * (-jnp.exp(alog_ref[...]))
    aT = dtT * (-jnp.exp(alogT_ref[...]))
    r_i = lax.broadcasted_iota(i32, (Q, Q), 0)
    c_i = lax.broadcasted_iota(i32, (Q, Q), 1)
    tril = (c_i <= r_i)
    tril_b = tril.astype(bf16)
    triu_b = (r_i <= c_i).astype(bf16)
    acs = sum(jnp.dot(tril_b, part, preferred_element_type=f32) for part in _split3(a))
    acsT = sum(jnp.dot(part, triu_b, preferred_element_type=f32) for part in _split3(aT))
    acs_last = acs[Q - 1:Q, :]
    e_acs = jnp.exp(acs)
    w_end = jnp.exp(acs_last - acs) * dt

    lane = lax.broadcasted_iota(i32, (Q, LANES), 1)
    first_head = lane < SSM_P
    y_cols = []
    new_h = []
    for g in range(SSM_G):
        bg = xc[:, D_INNER + g * D_STATE:D_INNER + (g + 1) * D_STATE].astype(bf16)
        cg = xc[:, D_INNER + SSM_G * D_STATE + g * D_STATE:D_INNER + SSM_G * D_STATE + (g + 1) * D_STATE].astype(bf16)
        cbm = lax.dot_general(cg, bg, (((1,), (1,)), ((), ())), preferred_element_type=f32)
        y_inter = jnp.dot(cg, hT[g].astype(bf16), preferred_element_type=f32)
        pair_cols = []
        for pr in range(SSM_GW // LANES):
            xp = xs_b[:, g * SSM_GW + pr * LANES:g * SSM_GW + (pr + 1) * LANES]
            ys = []
            for hh in range(2):
                h = g * (SSM_GW // SSM_P) + 2 * pr + hh
                seg = acs[:, h:h + 1] - acsT[h:h + 1, :]
                lm = jnp.exp(jnp.where(tril, seg, NEG))
                mm = (cbm * lm * dtT[h:h + 1, :]).astype(bf16)
                ys.append(jnp.dot(mm, xp, preferred_element_type=f32))
            pair_cols.append(jnp.where(first_head, ys[0], ys[1]))
        y_cols.append((jnp.concatenate(pair_cols, axis=1), y_inter))
        new_h.append((bg, g))

    e_acs_x = jnp.dot(e_acs.astype(bf16), e_mat, preferred_element_type=f32)
    w_end_x = jnp.dot(w_end.astype(bf16), e_mat, preferred_element_type=f32)
    xw = (xs * w_end_x).astype(bf16)
    dec8 = jnp.broadcast_to(jnp.exp(acs_last), (SUBLANES, LANES))
    dec_hi = dec8.astype(bf16)
    dec_lo = (dec8 - dec_hi.astype(f32)).astype(bf16)
    dec_x = (jnp.dot(dec_hi, e_mat, preferred_element_type=f32)
             + jnp.dot(dec_lo, e_mat, preferred_element_type=f32))[0:1, :]

    y_parts = []
    for g in range(SSM_G):
        y_intra, y_inter = y_cols[g]
        sl = slice(g * SSM_GW, (g + 1) * SSM_GW)
        yg = y_intra + y_inter * e_acs_x[:, sl] + dskip_ref[:, sl] * xs[:, sl]
        yg = yg * zs_ref[0][:, sl].astype(f32)
        y_parts.append(_rms(yg, gssm_ref[:, sl], SSM_EPS))
        bg = new_h[g][0]
        upd = lax.dot_general(bg, xw[:, sl], (((0,), (0,)), ((), ())), preferred_element_type=f32)
        hT[g] = hT[g] * dec_x[:, sl] + upd
    y_ref[0] = jnp.concatenate(y_parts, axis=1).astype(bf16)

    @pl.when(c == pl.num_programs(1) - 1)
    def _():
        for g in range(SSM_G):
            ssmo_ref[0, g * SSM_GW:(g + 1) * SSM_GW, :] = hT[g].T


def _ssd(xbc, zs, dt, conv0, ssm0, conv_w, conv_b, a_log, d_skip, g_ssm, e_mat):
    B, L, _ = xbc.shape
    Q = min(SSD_CHUNK, L)
    assert L % Q == 0 and Q % SUBLANES == 0
    nc = L // Q
    dtT = jnp.swapaxes(dt[:, :, :SSM_H], 1, 2)
    alog_row = jnp.pad(a_log.reshape(1, SSM_H), ((0, 0), (0, LANES - SSM_H)))
    alog_col = a_log.reshape(SSM_H, 1)
    dskip_x = jnp.repeat(d_skip, SSM_P).reshape(1, D_INNER)
    kern = functools.partial(_ssd_kernel, Q=Q)
    cmap = lambda b, c: (0, 0)
    return pl.pallas_call(
        kern,
        out_shape=(jax.ShapeDtypeStruct((B, L, D_INNER), bf16),
                   jax.ShapeDtypeStruct((B, CONV_W - 1, CONV_DIM), f32),
                   jax.ShapeDtypeStruct((B, D_INNER, D_STATE), f32)),
        grid=(B, nc),
        in_specs=[pl.BlockSpec((1, Q, CONV_DIM), lambda b, c: (b, c, 0)),
                  pl.BlockSpec((1, Q, D_INNER), lambda b, c: (b, c, 0)),
                  pl.BlockSpec((1, Q, LANES), lambda b, c: (b, c, 0)),
                  pl.BlockSpec((1, SSM_H, Q), lambda b, c: (b, 0, c)),
                  pl.BlockSpec((1, CONV_W - 1, CONV_DIM), lambda b, c: (b, 0, 0)),
                  pl.BlockSpec((1, D_INNER, D_STATE), lambda b, c: (b, 0, 0)),
                  pl.BlockSpec((CONV_W, CONV_DIM), cmap), pl.BlockSpec((1, CONV_DIM), cmap),
                  pl.BlockSpec((1, LANES), cmap), pl.BlockSpec((SSM_H, 1), cmap),
                  pl.BlockSpec((1, D_INNER), cmap), pl.BlockSpec((1, D_INNER), cmap),
                  pl.BlockSpec((LANES, D_INNER), cmap)],
        out_specs=(pl.BlockSpec((1, Q, D_INNER), lambda b, c: (b, c, 0)),
                   pl.BlockSpec((1, CONV_W - 1, CONV_DIM), lambda b, c: (b, 0, 0)),
                   pl.BlockSpec((1, D_INNER, D_STATE), lambda b, c: (b, 0, 0))),
        scratch_shapes=[pltpu.VMEM((SSM_G, D_STATE, SSM_GW), f32),
                        pltpu.VMEM((SUBLANES + Q, CONV_DIM), f32)],
        compiler_params=_params(("parallel", "arbitrary")),
        name="ssd_scan",
    )(xbc, zs, dt, dtT, conv0, ssm0, conv_w, conv_b.reshape(1, CONV_DIM), alog_row, alog_col,
      dskip_x, g_ssm.reshape(1, D_INNER), e_mat)


def _route(lg):
    lane = lax.broadcasted_iota(i32, lg.shape, 1)
    lanef = lane.astype(f32)
    big = float(LANES)
    gl = jnp.where(lane < N_EGROUPS, lg, NEG)
    gmax = jnp.max(gl, axis=1, keepdims=True)
    gsel = jnp.min(jnp.where(gl == gmax, lanef, big), axis=1, keepdims=True)
    p_g = 1.0 / jnp.sum(jnp.exp(gl - gmax), axis=1, keepdims=True)
    first = N_EGROUPS + E_PER_GROUP * gsel
    emask = (lanef >= first) & (lanef < first + E_PER_GROUP)
    el = jnp.where(emask, lg, NEG)
    v1 = jnp.max(el, axis=1, keepdims=True)
    i1 = jnp.min(jnp.where(emask & (el == v1), lanef, big), axis=1, keepdims=True)
    emask2 = emask & (lanef != i1)
    el2 = jnp.where(emask2, lg, NEG)
    v2 = jnp.max(el2, axis=1, keepdims=True)
    i2 = jnp.min(jnp.where(emask2 & (el2 == v2), lanef, big), axis=1, keepdims=True)
    t = jnp.exp(v2 - v1)
    w1 = p_g / (1.0 + t)
    w2 = p_g * t / (1.0 + t)
    out = jnp.where(lane == 0, i1 - N_EGROUPS, 0.0)
    out = jnp.where(lane == 1, i2 - N_EGROUPS, out)
    out = jnp.where(lane == 2, w1, out)
    out = jnp.where(lane == 3, w2, out)
    return out


def _to_row_tiles(ref, x, rows):
    for j in range(D_MODEL // LANES):
        ref[pl.ds(j, rows, stride=SUBLANES), :] = x[:, j * LANES:(j + 1) * LANES]


def _from_row_tiles(ref, rows):
    return jnp.concatenate([ref[pl.ds(j, rows, stride=SUBLANES), :] for j in range(D_MODEL // LANES)], axis=1)


def _mix_kernel(x_ref, gt1_ref, sc2_ref, sh2_ref, o_ref, ys_ref, ga_ref, gs_ref,
                woa_ref, wos_ref, wo_ref, gpost_ref, gpre_ref, wr_ref, br_ref,
                x1_ref, u2_ref, route_ref):
    tm = x_ref.shape[0]
    ya = jnp.dot(o_ref[...], woa_ref[...], preferred_element_type=f32)
    ys = jnp.dot(ys_ref[...], wos_ref[...], preferred_element_type=f32)
    m = ga_ref[...].astype(f32) * ya + gs_ref[...].astype(f32) * ys
    mo = jnp.dot(m.astype(bf16), wo_ref[...], preferred_element_type=f32)
    x1 = x_ref[...] + gt1_ref[...] * _rms(mo, gpost_ref[...], EPS)
    x1_ref[...] = x1
    u2 = _rms(x1, gpre_ref[...], EPS) * (1.0 + sc2_ref[...]) + sh2_ref[...]
    _to_row_tiles(u2_ref, u2, tm)
    lg = jnp.dot(u2.astype(bf16), wr_ref[...], preferred_element_type=f32) + br_ref[...]
    route_ref[...] = _route(lg)


def _mix(grp, x2, o, ysn, ga, gs, w_oa, w_os, w_o, g_post1, g_pre2, w_r, b_r):
    T, tm = grp.T, grp.tm
    return pl.pallas_call(
        _mix_kernel,
        out_shape=(jax.ShapeDtypeStruct((T, D_MODEL), f32),
                   jax.ShapeDtypeStruct((T * SUBLANES, LANES), f32),
                   jax.ShapeDtypeStruct((T, LANES), f32)),
        grid=(grp.n_tiles,),
        in_specs=[grp.row_spec(D_MODEL), grp.mod_spec(2), grp.mod_spec(4), grp.mod_spec(3),
                  grp.row_spec(ATTN_W), grp.row_spec(D_INNER), grp.row_spec(D_MODEL), grp.row_spec(D_MODEL),
                  _const_spec(w_oa.shape), _const_spec(w_os.shape), _const_spec(w_o.shape),
                  _const_spec((1, D_MODEL)), _const_spec((1, D_MODEL)),
                  _const_spec(w_r.shape), _const_spec((1, LANES))],
        out_specs=(grp.row_spec(D_MODEL),
                   pl.BlockSpec((tm * SUBLANES, LANES), lambda i: (i, 0)),
                   grp.row_spec(LANES)),
        compiler_params=_params(("parallel",)),
        name="mixer_out",
    )(x2, grp.mod, grp.mod, grp.mod, o, ysn, ga, gs, w_oa, w_os, w_o, g_post1, g_pre2, w_r, b_r)


def _count_kernel(route_ref, cnt_ref):
    i = pl.program_id(0)

    @pl.when(i == 0)
    def _():
        cnt_ref[...] = jnp.zeros(cnt_ref.shape, f32)

    r = route_ref[...]
    lanef = lax.broadcasted_iota(i32, r.shape, 1).astype(f32)
    hits = (lanef == r[:, 0:1]).astype(f32) + (lanef == r[:, 1:2]).astype(f32)
    cnt_ref[...] += jnp.sum(hits, axis=0, keepdims=True)


def _expert_counts(route, tm):
    T = route.shape[0]
    return pl.pallas_call(
        _count_kernel,
        out_shape=jax.ShapeDtypeStruct((1, LANES), f32),
        grid=(T // tm,),
        in_specs=[pl.BlockSpec((tm, LANES), lambda i: (i, 0))],
        out_specs=pl.BlockSpec((1, LANES), lambda i: (0, 0)),
        compiler_params=_params(("arbitrary",)),
        name="moe_count",
    )(route)


def _slot_kernel(route_ref, start_ref, dest_ref, carry):
    i = pl.program_id(0)

    @pl.when(i == 0)
    def _():
        carry[...] = jnp.broadcast_to(start_ref[...], carry.shape)

    r = route_ref[...]
    tm = r.shape[0]
    lane = lax.broadcasted_iota(i32, r.shape, 1)
    lanef = lane.astype(f32)
    oh1 = (lanef == r[:, 0:1]).astype(f32)
    oh2 = (lanef == r[:, 1:2]).astype(f32)
    cnt = oh1 + oh2
    ri = lax.broadcasted_iota(i32, (tm, tm), 0)
    ci = lax.broadcasted_iota(i32, (tm, tm), 1)
    before = (ci < ri).astype(bf16)
    pos = jnp.dot(before, cnt.astype(bf16), preferred_element_type=f32) + carry[0:1, :]
    d1 = jnp.sum(oh1 * pos, axis=1, keepdims=True)
    d2 = jnp.sum(oh2 * pos, axis=1, keepdims=True)
    dest_ref[...] = jnp.where(lane == 0, d1, jnp.where(lane == 1, d2, 0.0)).astype(i32)
    carry[...] = carry[...] + jnp.sum(cnt, axis=0, keepdims=True)


def _slots(route, starts, tm):
    T = route.shape[0]
    return pl.pallas_call(
        _slot_kernel,
        out_shape=jax.ShapeDtypeStruct((T, LANES), i32),
        grid=(T // tm,),
        in_specs=[pl.BlockSpec((tm, LANES), lambda i: (i, 0)), pl.BlockSpec((1, LANES), lambda i: (0, 0))],
        out_specs=pl.BlockSpec((tm, LANES), lambda i: (i, 0)),
        scratch_shapes=[pltpu.VMEM((SUBLANES, LANES), f32)],
        compiler_params=_params(("arbitrary",)),
        name="moe_slots",
    )(route, starts)


def _row_copy(src, src_row, dst, dst_row, sem):
    return pltpu.make_async_copy(src.at[pl.ds(pl.multiple_of(src_row * SUBLANES, SUBLANES), SUBLANES), :],
                                 dst.at[pl.ds(pl.multiple_of(dst_row * SUBLANES, SUBLANES), SUBLANES), :], sem)


def _dispatch_kernel(dest_ref, u_ref, xs_in_ref, xs_ref, sem, *, tm):
    del xs_in_ref
    base = pl.program_id(0) * tm

    def issue(t, c):
        for k in range(2):
            _row_copy(u_ref, base + t, xs_ref, dest_ref[0, 0, 2 * t + k], sem).start()
        return c

    lax.fori_loop(0, tm, issue, 0)

    def drain(t, c):
        for k in range(2):
            _row_copy(u_ref, 0, xs_ref, 0, sem).wait()
        return c

    lax.fori_loop(0, tm, drain, 0)


def _dispatch(dest2, u_rows, n_slots, tm):
    T = dest2.shape[0]
    xs0 = jnp.zeros((n_slots * SUBLANES, LANES), f32)
    dest3 = dest2.reshape(T // tm, 1, 2 * tm)
    return pl.pallas_call(
        functools.partial(_dispatch_kernel, tm=tm),
        out_shape=jax.ShapeDtypeStruct((n_slots * SUBLANES, LANES), f32),
        grid=(T // tm,),
        in_specs=[pl.BlockSpec((1, 1, 2 * tm), lambda i: (i, 0, 0), memory_space=pltpu.SMEM),
                  pl.BlockSpec(memory_space=pl.ANY), pl.BlockSpec(memory_space=pl.ANY)],
        out_specs=pl.BlockSpec(memory_space=pl.ANY),
        scratch_shapes=[pltpu.SemaphoreType.DMA(())],
        input_output_aliases={2: 0},
        compiler_params=_params(("arbitrary",)),
        name="moe_dispatch",
    )(dest3, u_rows, xs0)


def _expert_kernel(be_ref, xs_ref, wg_ref, wu_ref, wd_ref, o_ref, *, blk):
    del be_ref
    x = _from_row_tiles(xs_ref, blk).astype(bf16)
    h = _silu(jnp.dot(x, wg_ref[...], preferred_element_type=f32)) * jnp.dot(x, wu_ref[...], preferred_element_type=f32)
    _to_row_tiles(o_ref, jnp.dot(h.astype(bf16), wd_ref[...], preferred_element_type=f32), blk)


def _experts(xs, blk_e, w_gate, w_up, w_down, blk):
    n_blocks = blk_e.shape[0]
    rows = blk * SUBLANES
    grid_spec = pltpu.PrefetchScalarGridSpec(
        num_scalar_prefetch=1,
        grid=(n_blocks,),
        in_specs=[pl.BlockSpec((rows, LANES), lambda i, be: (i, 0)),
                  pl.BlockSpec((None, D_MODEL, D_FF), lambda i, be: (be[i], 0, 0)),
                  pl.BlockSpec((None, D_MODEL, D_FF), lambda i, be: (be[i], 0, 0)),
                  pl.BlockSpec((None, D_FF, D_MODEL), lambda i, be: (be[i], 0, 0))],
        out_specs=pl.BlockSpec((rows, LANES), lambda i, be: (i, 0)),
    )
    return pl.pallas_call(
        functools.partial(_expert_kernel, blk=blk),
        out_shape=jax.ShapeDtypeStruct(xs.shape, f32),
        grid_spec=grid_spec,
        compiler_params=_params(("parallel",)),
        name="moe_experts",
    )(blk_e, xs, w_gate, w_up, w_down)


def _combine_kernel(dest_ref, outs_ref, x1_ref, gt2_ref, route_ref, gpost_ref, y_ref, buf_a, buf_b, sem, *, tm):
    def issue(t, c):
        _row_copy(outs_ref, dest_ref[0, 0, 2 * t], buf_a, t, sem).start()
        _row_copy(outs_ref, dest_ref[0, 0, 2 * t + 1], buf_b, t, sem).start()
        return c

    lax.fori_loop(0, tm, issue, 0)

    def drain(t, c):
        _row_copy(outs_ref, 0, buf_a, 0, sem).wait()
        _row_copy(outs_ref, 0, buf_b, 0, sem).wait()
        return c

    lax.fori_loop(0, tm, drain, 0)
    r = route_ref[...]
    f = r[:, 2:3] * _from_row_tiles(buf_a, tm) + r[:, 3:4] * _from_row_tiles(buf_b, tm)
    y_ref[...] = x1_ref[...] + gt2_ref[...] * _rms(f, gpost_ref[...], EPS)


def _combine(grp, dest2, outs, x1, route, g_post2, tm):
    T = grp.T
    dest3 = dest2.reshape(T // tm, 1, 2 * tm)
    return pl.pallas_call(
        functools.partial(_combine_kernel, tm=tm),
        out_shape=jax.ShapeDtypeStruct((T, D_MODEL), f32),
        grid=(T // tm,),
        in_specs=[pl.BlockSpec((1, 1, 2 * tm), lambda i: (i, 0, 0), memory_space=pltpu.SMEM),
                  pl.BlockSpec(memory_space=pl.ANY),
                  grp.row_spec(D_MODEL), grp.mod_spec(5), grp.row_spec(LANES), _const_spec((1, D_MODEL))],
        out_specs=grp.row_spec(D_MODEL),
        scratch_shapes=[pltpu.VMEM((tm * SUBLANES, LANES), f32), pltpu.VMEM((tm * SUBLANES, LANES), f32),
                        pltpu.SemaphoreType.DMA(())],
        compiler_params=_params(("arbitrary",)),
        name="moe_combine",
    )(dest3, outs, x1, grp.mod, route, g_post2)


def _hmoe_and_residual(grp, x1, u_rows, route, w_gate, w_up, w_down, g_post2):
    T = grp.T
    tm = grp.tm
    counts = _expert_counts(route, tm)[0, :N_EXPERTS].astype(i32)
    padded = (counts + MOE_BLOCK - 1) // MOE_BLOCK * MOE_BLOCK
    ends = jnp.cumsum(padded)
    starts = ends - padded
    n_blocks = -(-2 * T // MOE_BLOCK) + N_EXPERTS
    n_slots = n_blocks * MOE_BLOCK
    blk_start = jnp.arange(n_blocks, dtype=i32) * MOE_BLOCK
    blk_e = jnp.minimum(jnp.searchsorted(ends, blk_start, side="right"), N_EXPERTS - 1).astype(i32)
    starts_row = jnp.pad(starts.astype(f32), (0, LANES - N_EXPERTS)).reshape(1, LANES)
    dest2 = _slots(route, starts_row, tm)[:, :2]
    xs = _dispatch(dest2, u_rows, n_slots, tm)
    outs = _experts(xs, blk_e, w_gate, w_up, w_down, MOE_BLOCK)
    return _combine(grp, dest2, outs, x1, route, g_post2, tm)


def _rope_tables(pos):
    half = HEAD_DIM // 2
    inv = ROPE_THETA ** (-2.0 * jnp.arange(half, dtype=f32) / HEAD_DIM)
    ang = pos.astype(f32)[:, None] * inv[None, :]
    cos, sin = jnp.cos(ang), jnp.sin(ang)
    return jnp.tile(cos, (1, LANES // half)), jnp.tile(jnp.concatenate([-sin, sin], axis=1), (1, LANES // HEAD_DIM))


def _layer(grp, x2, lw, lam_init, attend, conv0, ssm0):
    n_seq, seq_len = grp.n_seq, grp.seq_len
    q, k, kb, v, vb = _proj_qkv(grp, x2, lw["g_pre1"], lw["w_qkv"], 1, 0)
    zs, xbc = _proj_zx(grp, x2, lw["g_pre1"], lw["w_z"], lw["w_xbc"], 1, 0)
    ga, gs, dt = _proj_gates(grp, x2, lw["g_pre1"], lw["w_gates"], lw["w_dt"], lw["dt_bias"], 1, 0)
    shp = lambda a: a.reshape(n_seq, seq_len, a.shape[-1])
    o = attend(shp(q), shp(kb), shp(vb))
    ysn, conv_new, ssm_new = _ssd(shp(xbc), shp(zs), shp(dt), conv0, ssm0.reshape(n_seq, D_INNER, D_STATE),
                                  lw["conv_w"], lw["conv_b"], lw["a_log"], lw["d_skip"], lw["g_ssm"], lw["e_mat"])
    x1, u_rows, route = _mix(grp, x2, o.reshape(grp.T, ATTN_W), ysn.reshape(grp.T, D_INNER), ga, gs,
                             lw["w_oa"], lw["w_os"], lw["w_o"], lw["g_post1"], lw["g_pre2"], lw["w_r"], lw["b_r"])
    y2 = _hmoe_and_residual(grp, x1, u_rows, route, lw["w_gate"], lw["w_up"], lw["w_down"], lw["g_post2"])
    return (y2, k.reshape(n_seq, seq_len, N_KV, HEAD_W), v.reshape(n_seq, seq_len, N_KV, HEAD_W), conv_new,
            ssm_new.reshape(n_seq, SSM_H, SSM_P, D_STATE))


def kernel(x_prompt, x_sample, cache_k, cache_v, state_conv, state_ssm, page_table, c_prompt, c_sample, w_ada, b_ada, g_pre1, g_post1, g_pre2, g_post2, w_in, lambda_q1, lambda_k1, lambda_q2, lambda_k2, g_subln, w_oa, conv_w, conv_b, dt_bias, a_log, d_skip, g_ssm, w_os, w_o, w_rg, b_rg, w_re, b_re, w_gate, w_up, w_down):
    depth = w_in.shape[0]
    bp, lp_len, _ = x_prompt.shape
    bs, ls_len, _ = x_sample.shape
    past = page_table.shape[1] * cache_k.shape[2]
    ck = cache_k.reshape(cache_k.shape[0], cache_k.shape[1], cache_k.shape[2], KV_W)
    cv = cache_v.reshape(cache_v.shape[0], cache_v.shape[1], cache_v.shape[2], KV_W)
    rope_p = _rope_tables(jnp.arange(lp_len, dtype=i32))
    rope_s8 = _rope_tables(past + jnp.arange(ls_len, dtype=i32))
    rope_s = tuple(jnp.tile(t, (bs, 1)) for t in rope_s8)
    e_mat = (jnp.arange(LANES, dtype=i32)[:, None] == jnp.arange(D_INNER, dtype=i32)[None, :] // SSM_P).astype(bf16)

    hp = x_prompt.reshape(bp * lp_len, D_MODEL)
    hs = x_sample.reshape(bs * ls_len, D_MODEL)
    c_all = jnp.concatenate([c_prompt, c_sample], axis=0)
    outs = [[] for _ in range(8)]
    for l in range(depth):
        lam_init = 0.8 - 0.6 * math.exp(-0.3 * l)
        w_in_l = w_in[l].astype(bf16)
        lw = dict(
            w_qkv=w_in_l[:, :OFF_Z], w_z=w_in_l[:, OFF_Z:OFF_XBC], w_xbc=w_in_l[:, OFF_XBC:OFF_DT],
            w_dt=jnp.pad(w_in_l[:, OFF_DT:OFF_GA], ((0, 0), (0, LANES - SSM_H))), w_gates=w_in_l[:, OFF_GA:],
            dt_bias=jnp.pad(dt_bias[l].reshape(1, SSM_H), ((0, 0), (0, LANES - SSM_H))),
            g_pre1=g_pre1[l].reshape(1, D_MODEL), g_post1=g_post1[l].reshape(1, D_MODEL),
            g_pre2=g_pre2[l].reshape(1, D_MODEL), g_post2=g_post2[l].reshape(1, D_MODEL),
            w_oa=w_oa[l].astype(bf16), w_os=w_os[l].astype(bf16), w_o=w_o[l].astype(bf16),
            conv_w=conv_w[l], conv_b=conv_b[l], a_log=a_log[l], d_skip=d_skip[l], g_ssm=g_ssm[l], e_mat=e_mat,
            w_r=jnp.pad(jnp.concatenate([w_rg[l], w_re[l]], axis=1),
                        ((0, 0), (0, LANES - N_EGROUPS - N_EXPERTS))).astype(bf16),
            b_r=jnp.pad(jnp.concatenate([b_rg[l], b_re[l]]), (0, LANES - N_EGROUPS - N_EXPERTS)).reshape(1, LANES),
            w_gate=w_gate[l].astype(bf16), w_up=w_up[l].astype(bf16), w_down=w_down[l].astype(bf16),
        )
        lam_p = jnp.stack([lambda_q1[l], lambda_k1[l], lambda_q2[l], lambda_k2[l]])
        gsub = g_subln[l].reshape(1, HEAD_W)
        mod = _ada(c_all, w_ada[l].astype(bf16), b_ada[l])
        grp_p = _Group(bp, lp_len, mod[:bp], rope_p, per_token=False)
        grp_s = _Group(bs, ls_len, jnp.repeat(mod[bp:], ls_len, axis=0), rope_s, per_token=True)

        attend_p = lambda q, kb, vb: _attn_prompt(q, kb, vb, lam_p, gsub, lam_init)
        attend_s = lambda q, kb, vb: _attn_sample(q, kb, vb, ck, cv, page_table, l, lam_p, gsub, lam_init)
        conv0 = jnp.zeros((bp, CONV_W - 1, CONV_DIM), f32)
        ssm0 = jnp.zeros((bp, SSM_H, SSM_P, D_STATE), f32)
        hp, kp, vp, cp, sp = _layer(grp_p, hp, lw, lam_init, attend_p, conv0, ssm0)
        hs, ksm, vsm, csm, ssm = _layer(grp_s, hs, lw, lam_init, attend_s, state_conv[l], state_ssm[l])
        for lst, val in zip(outs, (kp, vp, cp, sp, ksm, vsm, csm, ssm)):
            lst.append(val)
    stacked = [jnp.stack(o) for o in outs]
    return (hp.reshape(bp, lp_len, D_MODEL), hs.reshape(bs, ls_len, D_MODEL), *stacked)
```

```python
import functools
import math

import jax
import jax.numpy as jnp
from jax import lax
from jax.experimental import pallas as pl
from jax.experimental.pallas import tpu as pltpu

f32 = jnp.float32
bf16 = jnp.bfloat16
i32 = jnp.int32

D_MODEL = 1024
N_HEADS = 8
N_KV = 4
HEAD_DIM = 64
HEAD_W = 2 * HEAD_DIM
ATTN_W = N_HEADS * HEAD_W
KV_W = N_KV * HEAD_W
ATTN_SCALE = HEAD_DIM ** -0.5
ROPE_THETA = 10000.0
SUBLN_EPS = 1e-5
D_INNER = 2048
SSM_P = 64
SSM_H = D_INNER // SSM_P
SSM_G = 4
SSM_GW = D_INNER // SSM_G
D_STATE = 128
CONV_W = 4
CONV_DIM = D_INNER + 2 * SSM_G * D_STATE
SSD_CHUNK = 128
SSM_EPS = 1e-5
N_EGROUPS = 4
E_PER_GROUP = 8
N_EXPERTS = 32
D_FF = 512
EPS = 1e-6
OFF_K = ATTN_W
OFF_V = OFF_K + KV_W
OFF_Z = OFF_V + KV_W
OFF_XBC = OFF_Z + D_INNER
OFF_DT = OFF_XBC + CONV_DIM
OFF_GA = OFF_DT + SSM_H
OFF_GS = OFF_GA + D_MODEL

LANES = 128
SUBLANES = 8
VMEM_LIMIT = 56 * 1024 * 1024
NEG = -1e30
LOG2E = math.log2(math.e)

TOKEN_TILE = 512
ATTN_TILE = 512
PAGES_PER_STEP = 8
MOE_BLOCK = 512
DISPATCH_TILE = 512


def _params(sem, vmem=VMEM_LIMIT):
    return pltpu.CompilerParams(dimension_semantics=sem, vmem_limit_bytes=vmem)


def _sigmoid(x):
    return 1.0 / (1.0 + jnp.exp(-x))


def _silu(x):
    return x * _sigmoid(x)


def _rms(x, g, eps):
    return x * lax.rsqrt(jnp.mean(x * x, axis=-1, keepdims=True) + eps) * g


def _bdot(a, b):
    return jnp.dot(a.astype(bf16), b.astype(bf16), preferred_element_type=f32)


class _Group:
    def __init__(self, n_seq, seq_len, mod_rows, rope_rows, per_token):
        self.n_seq, self.seq_len = n_seq, seq_len
        self.T = n_seq * seq_len
        self.tm = min(TOKEN_TILE, self.T if per_token else seq_len)
        assert self.T % self.tm == 0
        self.per_token = per_token
        if per_token:
            self.mod = mod_rows
            self.rope = rope_rows
        else:
            assert seq_len % self.tm == 0
            self.tiles_per_seq = seq_len // self.tm
            self.mod = mod_rows.reshape(n_seq, 1, 6 * D_MODEL)
            self.rope = rope_rows
        self.n_tiles = self.T // self.tm

    def mod_spec(self, col):
        if self.per_token:
            return pl.BlockSpec((self.tm, D_MODEL), lambda i: (i, col))
        tps = self.tiles_per_seq
        return pl.BlockSpec((None, 1, D_MODEL), lambda i: (i // tps, 0, col))

    def rope_spec(self):
        if self.per_token:
            return pl.BlockSpec((self.tm, LANES), lambda i: (i, 0))
        tps = self.tiles_per_seq
        return pl.BlockSpec((self.tm, LANES), lambda i: (i % tps, 0))

    def row_spec(self, width):
        return pl.BlockSpec((self.tm, width), lambda i: (i, 0))


def _const_spec(shape):
    nd = len(shape)
    return pl.BlockSpec(shape, lambda i: (0,) * nd)


def _ada_kernel(c_ref, w_ref, b_ref, o_ref):
    o_ref[...] = _bdot(_silu(c_ref[...]), w_ref[...]) + b_ref[...]


def _ada(c, w_ada, b_ada):
    m = c.shape[0]
    n = w_ada.shape[1]
    tn = D_MODEL
    return pl.pallas_call(
        _ada_kernel,
        out_shape=jax.ShapeDtypeStruct((m, n), f32),
        grid=(n // tn,),
        in_specs=[pl.BlockSpec((m, D_MODEL), lambda j: (0, 0)),
                  pl.BlockSpec((D_MODEL, tn), lambda j: (0, j)),
                  pl.BlockSpec((1, tn), lambda j: (0, j))],
        out_specs=pl.BlockSpec((m, tn), lambda j: (0, j)),
        compiler_params=_params(("parallel",)),
        name="ada_mod",
    )(c, w_ada, b_ada.reshape(1, n))


def _modulated(x_ref, g_ref, sc_ref, sh_ref):
    u = _rms(x_ref[...], g_ref[...], EPS) * (1.0 + sc_ref[...]) + sh_ref[...]
    return u.astype(bf16)


def _rope_cols(x, n_blocks, cos, sin_signed, lo):
    out = []
    for j in range(n_blocks):
        xb = x[:, j * LANES:(j + 1) * LANES]
        partner = jnp.where(lo, pltpu.roll(xb, LANES - HEAD_DIM // 2, 1), pltpu.roll(xb, HEAD_DIM // 2, 1))
        out.append(xb * cos + partner * sin_signed)
    return jnp.concatenate(out, axis=1)


def _qkv_kernel(x_ref, g_ref, sc_ref, sh_ref, w_ref, cos_ref, sin_ref,
                q_ref, k_ref, kb_ref, v_ref, vb_ref):
    u = _modulated(x_ref, g_ref, sc_ref, sh_ref)
    proj = jnp.dot(u, w_ref[...], preferred_element_type=f32)
    cos, sin_signed = cos_ref[...], sin_ref[...]
    lane = lax.broadcasted_iota(i32, cos.shape, 1)
    lo = jnp.bitwise_and(lane, HEAD_DIM - 1) < HEAD_DIM // 2
    q = _rope_cols(proj[:, :OFF_K], N_HEADS, cos, sin_signed, lo)
    k = _rope_cols(proj[:, OFF_K:OFF_V], N_KV, cos, sin_signed, lo)
    v = proj[:, OFF_V:OFF_Z]
    q_ref[...] = (q * (ATTN_SCALE * LOG2E)).astype(bf16)
    kb_ref[...] = k.astype(bf16)
    vb_ref[...] = v.astype(bf16)
    tm = k.shape[0]
    for h in range(N_KV):
        k_ref[pl.ds(h, tm, stride=N_KV), :] = k[:, h * HEAD_W:(h + 1) * HEAD_W]
        v_ref[pl.ds(h, tm, stride=N_KV), :] = v[:, h * HEAD_W:(h + 1) * HEAD_W]


def _proj_qkv(grp, x2, g, w_qkv, col_scale, col_shift):
    T, tm = grp.T, grp.tm
    head_rows = pl.BlockSpec((tm * N_KV, HEAD_W), lambda i: (i, 0))
    return pl.pallas_call(
        _qkv_kernel,
        out_shape=(jax.ShapeDtypeStruct((T, ATTN_W), bf16),
                   jax.ShapeDtypeStruct((T * N_KV, HEAD_W), f32), jax.ShapeDtypeStruct((T, KV_W), bf16),
                   jax.ShapeDtypeStruct((T * N_KV, HEAD_W), f32), jax.ShapeDtypeStruct((T, KV_W), bf16)),
        grid=(grp.n_tiles,),
        in_specs=[grp.row_spec(D_MODEL), _const_spec((1, D_MODEL)),
                  grp.mod_spec(col_scale), grp.mod_spec(col_shift),
                  _const_spec(w_qkv.shape), grp.rope_spec(), grp.rope_spec()],
        out_specs=(grp.row_spec(ATTN_W), head_rows, grp.row_spec(KV_W),
                   head_rows, grp.row_spec(KV_W)),
        compiler_params=_params(("parallel",)),
        name="proj_qkv",
    )(x2, g, grp.mod, grp.mod, w_qkv, grp.rope[0], grp.rope[1])


def _zx_kernel(x_ref, g_ref, sc_ref, sh_ref, wz_ref, wx_ref, zs_ref, xbc_ref):
    u = _modulated(x_ref, g_ref, sc_ref, sh_ref)
    zs_ref[...] = _silu(jnp.dot(u, wz_ref[...], preferred_element_type=f32)).astype(bf16)
    xbc_ref[...] = jnp.dot(u, wx_ref[...], preferred_element_type=f32).astype(bf16)


def _proj_zx(grp, x2, g, w_z, w_xbc, col_scale, col_shift):
    T = grp.T
    return pl.pallas_call(
        _zx_kernel,
        out_shape=(jax.ShapeDtypeStruct((T, D_INNER), bf16), jax.ShapeDtypeStruct((T, CONV_DIM), bf16)),
        grid=(grp.n_tiles,),
        in_specs=[grp.row_spec(D_MODEL), _const_spec((1, D_MODEL)),
                  grp.mod_spec(col_scale), grp.mod_spec(col_shift),
                  _const_spec(w_z.shape), _const_spec(w_xbc.shape)],
        out_specs=(grp.row_spec(D_INNER), grp.row_spec(CONV_DIM)),
        compiler_params=_params(("parallel",)),
        name="proj_zx",
    )(x2, g, grp.mod, grp.mod, w_z, w_xbc)


def _gates_kernel(x_ref, g_ref, sc_ref, sh_ref, wg_ref, wdt_ref, dtb_ref, ga_ref, gs_ref, dt_ref):
    u = _modulated(x_ref, g_ref, sc_ref, sh_ref)
    gates = _sigmoid(jnp.dot(u, wg_ref[...], preferred_element_type=f32))
    ga_ref[...] = gates[:, :D_MODEL].astype(bf16)
    gs_ref[...] = gates[:, D_MODEL:].astype(bf16)
    raw = jnp.dot(u, wdt_ref[...], preferred_element_type=f32) + dtb_ref[...]
    dt_ref[...] = jnp.maximum(raw, 0.0) + jnp.log1p(jnp.exp(-jnp.abs(raw)))


def _proj_gates(grp, x2, g, w_gates, w_dt, dt_bias, col_scale, col_shift):
    T = grp.T
    return pl.pallas_call(
        _gates_kernel,
        out_shape=(jax.ShapeDtypeStruct((T, D_MODEL), bf16), jax.ShapeDtypeStruct((T, D_MODEL), bf16),
                   jax.ShapeDtypeStruct((T, LANES), f32)),
        grid=(grp.n_tiles,),
        in_specs=[grp.row_spec(D_MODEL), _const_spec((1, D_MODEL)),
                  grp.mod_spec(col_scale), grp.mod_spec(col_shift),
                  _const_spec(w_gates.shape), _const_spec(w_dt.shape), _const_spec((1, LANES))],
        out_specs=(grp.row_spec(D_MODEL), grp.row_spec(D_MODEL), grp.row_spec(LANES)),
        compiler_params=_params(("parallel",)),
        name="proj_gates",
    )(x2, g, grp.mod, grp.mod, w_gates, w_dt, dt_bias)


def _lambda_full(lam_ref, lam_init):
    lp = lam_ref[...]
    l1 = jnp.sum(lp[0:1] * lp[1:2], axis=1, keepdims=True)
    l2 = jnp.sum(lp[2:3] * lp[3:4], axis=1, keepdims=True)
    return jnp.exp(l1) - jnp.exp(l2) + lam_init


def _with_ones(v):
    return jnp.concatenate([v, jnp.ones(v.shape, bf16)], axis=1)


def _softmax_step(s, v1, m_ref, acc_ref):
    reps = s.shape[1] // LANES if s.shape[1] % LANES == 0 else 0
    m_prev = m_ref[...]
    m_new = jnp.maximum(m_prev, jnp.max(s, axis=1, keepdims=True))
    alpha = jnp.exp2(m_prev - m_new)
    if reps:
        p = jnp.exp2(s - jnp.concatenate([m_new] * reps, axis=1))
    else:
        p = jnp.exp2(s - m_new[:, 0:1])
    acc_ref[...] = (jnp.concatenate([alpha, alpha], axis=1) * acc_ref[...]
                    + jnp.dot(p.astype(bf16), v1, preferred_element_type=f32))
    m_ref[...] = m_new


def _diff_heads(acc, lam, gsub, lam_init, rows):
    outs = []
    for r in range(2):
        a0 = acc[(2 * r) * rows:(2 * r + 1) * rows]
        a1 = acc[(2 * r + 1) * rows:(2 * r + 2) * rows]
        o = a0[:, :HEAD_W] / a0[:, HEAD_W:] - lam * (a1[:, :HEAD_W] / a1[:, HEAD_W:])
        outs.append(_rms(o, gsub, SUBLN_EPS) * (1.0 - lam_init))
    return jnp.concatenate(outs, axis=1)


def _pad_components(q, rows):
    lane = lax.broadcasted_iota(i32, (rows, LANES), 1)
    lo = lane < HEAD_DIM
    blocks = []
    for r in range(2):
        qh = q[:, r * LANES:(r + 1) * LANES].astype(f32)
        blocks.append(jnp.where(lo, qh, 0.0))
        blocks.append(jnp.where(lo, 0.0, qh))
    return jnp.concatenate(blocks, axis=0)


def _stacked_causal(n, copies=4):
    assert n & (n - 1) == 0
    row = jnp.bitwise_and(lax.broadcasted_iota(i32, (copies * n, n), 0), n - 1)
    col = lax.broadcasted_iota(i32, (copies * n, n), 1)
    return col <= row


def _attn_prompt_kernel(lam_ref, gsub_ref, q_ref, k_ref, v_ref, o_ref, qpad, m_s, acc_s, s_a, s_b, *, tq, lam_init):
    i = pl.program_id(2)
    qpad[...] = _pad_components(q_ref[0], tq).astype(bf16)
    m_s[...] = jnp.full(m_s.shape, NEG, f32)
    acc_s[...] = jnp.zeros(acc_s.shape, f32)

    def scores(c):
        kc = k_ref[0, pl.ds(pl.multiple_of(c * tq, tq), tq), :]
        return lax.dot_general(qpad[...], kc, (((1,), (1,)), ((), ())), preferred_element_type=f32)

    def consume(s_ref, c):
        _softmax_step(s_ref[...], _with_ones(v_ref[0, pl.ds(pl.multiple_of(c * tq, tq), tq), :]), m_s, acc_s)

    s_a[...] = jnp.where(_stacked_causal(tq), scores(i), NEG)

    def pair(p, carry):
        s_b[...] = scores(2 * p)
        consume(s_a, jnp.where(p == 0, i, 2 * p - 1))
        s_a[...] = scores(jnp.minimum(2 * p + 1, i - 1))
        consume(s_b, 2 * p)
        return carry

    lax.fori_loop(0, jnp.right_shift(i + 1, 1), pair, 0)

    @pl.when(jnp.bitwise_and(i, 1) == 0)
    def _():
        consume(s_a, jnp.maximum(i - 1, 0))

    lam = _lambda_full(lam_ref, lam_init)
    o_ref[0] = _diff_heads(acc_s[...], lam, gsub_ref[...], lam_init, tq).astype(bf16)


def _attn_prompt(q, kb, vb, lam_p, gsub, lam_init):
    B, L, _ = q.shape
    tq = min(ATTN_TILE, L)
    assert L % tq == 0
    kern = functools.partial(_attn_prompt_kernel, tq=tq, lam_init=lam_init)
    return pl.pallas_call(
        kern,
        out_shape=jax.ShapeDtypeStruct((B, L, ATTN_W), bf16),
        grid=(B, N_KV, L // tq),
        in_specs=[pl.BlockSpec((4, HEAD_DIM), lambda b, g, i: (0, 0)),
                  pl.BlockSpec((1, HEAD_W), lambda b, g, i: (0, 0)),
                  pl.BlockSpec((1, tq, 2 * HEAD_W), lambda b, g, i: (b, i, g)),
                  pl.BlockSpec((1, L, HEAD_W), lambda b, g, i: (b, 0, g)),
                  pl.BlockSpec((1, L, HEAD_W), lambda b, g, i: (b, 0, g))],
        out_specs=pl.BlockSpec((1, tq, 2 * HEAD_W), lambda b, g, i: (b, i, g)),
        scratch_shapes=[pltpu.VMEM((4 * tq, LANES), bf16), pltpu.VMEM((4 * tq, LANES), f32),
                        pltpu.VMEM((4 * tq, 2 * LANES), f32),
                        pltpu.VMEM((4 * tq, tq), f32), pltpu.VMEM((4 * tq, tq), f32)],
        compiler_params=_params(("parallel", "parallel", "arbitrary")),
        name="attn_prompt",
    )(lam_p, gsub, q, kb, vb)


def _attn_sample_kernel(pt_ref, lam_ref, gsub_ref, q_ref, kn_ref, vn_ref, *rest, n_pages_step, seq_new, lam_init):
    del pt_ref
    kp = rest[:n_pages_step]
    vp = rest[n_pages_step:2 * n_pages_step]
    o_ref, qpad, m_s, acc_s = rest[2 * n_pages_step:]
    j = pl.program_id(1)

    @pl.when(j == 0)
    def _():
        q = q_ref[0]
        for g in range(N_KV):
            qpad[g] = _pad_components(q[:, g * 2 * HEAD_W:(g + 1) * 2 * HEAD_W], seq_new)
        m_s[...] = jnp.full(m_s.shape, NEG, f32)
        acc_s[...] = jnp.zeros(acc_s.shape, f32)

    page = kp[0].shape[0] // N_KV
    for g in range(N_KV):
        qg = qpad[g].astype(bf16)
        kg = jnp.concatenate([r[pl.ds(g, page, stride=N_KV), :].astype(bf16) for r in kp], axis=0)
        vg = jnp.concatenate([r[pl.ds(g, page, stride=N_KV), :].astype(bf16) for r in vp], axis=0)
        s = lax.dot_general(qg, kg, (((1,), (1,)), ((), ())), preferred_element_type=f32)
        _softmax_step(s, _with_ones(vg), m_s.at[g], acc_s.at[g])

    @pl.when(j == pl.num_programs(1) - 1)
    def _():
        lam = _lambda_full(lam_ref, lam_init)
        causal = _stacked_causal(seq_new)
        outs = []
        for g in range(N_KV):
            qg = qpad[g].astype(bf16)
            kg = kn_ref[0][:, g * HEAD_W:(g + 1) * HEAD_W]
            vg = vn_ref[0][:, g * HEAD_W:(g + 1) * HEAD_W]
            s = lax.dot_general(qg, kg, (((1,), (1,)), ((), ())), preferred_element_type=f32)
            _softmax_step(jnp.where(causal, s, NEG), _with_ones(vg), m_s.at[g], acc_s.at[g])
            outs.append(_diff_heads(acc_s[g], lam, gsub_ref[...], lam_init, seq_new))
        o_ref[0] = jnp.concatenate(outs, axis=1).astype(bf16)


def _attn_sample(q, kb, vb, cache_k, cache_v, page_table, layer, lam_p, gsub, lam_init):
    S, Ls, _ = q.shape
    n_pages = page_table.shape[1]
    page_rows = cache_k.shape[2]
    pps = PAGES_PER_STEP
    while n_pages % pps:
        pps -= 1
    steps = n_pages // pps

    def page_spec(p):
        return pl.BlockSpec((None, None, page_rows, HEAD_W), lambda s, j, pt: (layer, pt[s, j * pps + p], 0, 0))

    kern = functools.partial(_attn_sample_kernel, n_pages_step=pps, seq_new=Ls, lam_init=lam_init)
    grid_spec = pltpu.PrefetchScalarGridSpec(
        num_scalar_prefetch=1,
        grid=(S, steps),
        in_specs=[pl.BlockSpec((4, HEAD_DIM), lambda s, j, pt: (0, 0)),
                  pl.BlockSpec((1, HEAD_W), lambda s, j, pt: (0, 0)),
                  pl.BlockSpec((1, Ls, ATTN_W), lambda s, j, pt: (s, 0, 0)),
                  pl.BlockSpec((1, Ls, KV_W), lambda s, j, pt: (s, 0, 0)),
                  pl.BlockSpec((1, Ls, KV_W), lambda s, j, pt: (s, 0, 0))]
        + [page_spec(p) for p in range(pps)] + [page_spec(p) for p in range(pps)],
        out_specs=pl.BlockSpec((1, Ls, ATTN_W), lambda s, j, pt: (s, 0, 0)),
        scratch_shapes=[pltpu.VMEM((N_KV, 4 * Ls, LANES), f32), pltpu.VMEM((N_KV, 4 * Ls, LANES), f32),
                        pltpu.VMEM((N_KV, 4 * Ls, 2 * LANES), f32)],
    )
    return pl.pallas_call(
        kern,
        out_shape=jax.ShapeDtypeStruct((S, Ls, ATTN_W), bf16),
        grid_spec=grid_spec,
        compiler_params=_params(("parallel", "arbitrary")),
        name="attn_sample",
    )(page_table, lam_p, gsub, q, kb, vb, *([cache_k] * pps), *([cache_v] * pps))


def _split3(x):
    hi = x.astype(bf16)
    r1 = x - hi.astype(f32)
    mid = r1.astype(bf16)
    lo = (r1 - mid.astype(f32)).astype(bf16)
    return hi, mid, lo


def _ssd_kernel(xbc_ref, zs_ref, dt_ref, dtT_ref, conv0_ref, ssm0_ref, cw_ref, cb_ref,
                alog_ref, alogT_ref, dskip_ref, gssm_ref, e_ref,
                y_ref, convo_ref, ssmo_ref, hT, cbuf, *, Q):
    c = pl.program_id(1)
    tail = CONV_W - 1
    base = SUBLANES

    @pl.when(c == 0)
    def _():
        cbuf[base - tail:base, :] = conv0_ref[0]
        for g in range(SSM_G):
            hT[g] = ssm0_ref[0, g * SSM_GW:(g + 1) * SSM_GW, :].T

    cbuf[base:base + Q, :] = xbc_ref[0].astype(f32)
    acc = cb_ref[...] + cw_ref[CONV_W - 1:CONV_W, :] * cbuf[base:base + Q, :]
    for j in range(tail):
        acc = acc + cw_ref[j:j + 1, :] * cbuf[pl.ds(base - tail + j, Q), :]
    xc = _silu(acc)
    convo_ref[0] = cbuf[pl.ds(base + Q - tail, tail), :]
    cbuf[0:SUBLANES, :] = cbuf[Q:Q + SUBLANES, :]

    xs = xc[:, :D_INNER]
    xs_b = xs.astype(bf16)
    e_mat = e_ref[...]

    dt = dt_ref[0]
    dtT = dtT_ref[0]
    a = dt * (-jnp.exp(alog_ref[...]))
    aT = dtT * (-jnp.exp(alogT_ref[...]))
    r_i = lax.broadcasted_iota(i32, (Q, Q), 0)
    c_i = lax.broadcasted_iota(i32, (Q, Q), 1)
    tril = (c_i <= r_i)
    tril_b = tril.astype(bf16)
    triu_b = (r_i <= c_i).astype(bf16)
    acs = sum(jnp.dot(tril_b, part, preferred_element_type=f32) for part in _split3(a))
    acsT = sum(jnp.dot(part, triu_b, preferred_element_type=f32) for part in _split3(aT))
    acs_last = acs[Q - 1:Q, :]
    e_acs = jnp.exp(acs)
    w_end = jnp.exp(acs_last - acs) * dt

    lane = lax.broadcasted_iota(i32, (Q, LANES), 1)
    first_head = lane < SSM_P
    y_cols = []
    new_h = []
    for g in range(SSM_G):
        bg = xc[:, D_INNER + g * D_STATE:D_INNER + (g + 1) * D_STATE].astype(bf16)
        cg = xc[:, D_INNER + SSM_G * D_STATE + g * D_STATE:D_INNER + SSM_G * D_STATE + (g + 1) * D_STATE].astype(bf16)
        cbm = lax.dot_general(cg, bg, (((1,), (1,)), ((), ())), preferred_element_type=f32)
        y_inter = jnp.dot(cg, hT[g].astype(bf16), preferred_element_type=f32)
        pair_cols = []
        for pr in range(SSM_GW // LANES):
            xp = xs_b[:, g * SSM_GW + pr * LANES:g * SSM_GW + (pr + 1) * LANES]
            ys = []
            for hh in range(2):
                h = g * (SSM_GW // SSM_P) + 2 * pr + hh
                seg = acs[:, h:h + 1] - acsT[h:h + 1, :]
                lm = jnp.exp(jnp.where(tril, seg, NEG))
                mm = (cbm * lm * dtT[h:h + 1, :]).astype(bf16)
                ys.append(jnp.dot(mm, xp, preferred_element_type=f32))
            pair_cols.append(jnp.where(first_head, ys[0], ys[1]))
        y_cols.append((jnp.concatenate(pair_cols, axis=1), y_inter))
        new_h.append((bg, g))

    e_acs_x = jnp.dot(e_acs.astype(bf16), e_mat, preferred_element_type=f32)
    w_end_x = jnp.dot(w_end.astype(bf16), e_mat, preferred_element_type=f32)
    xw = (xs * w_end_x).astype(bf16)
    dec8 = jnp.broadcast_to(jnp.exp(acs_last), (SUBLANES, LANES))
    dec_hi = dec8.astype(bf16)
    dec_lo = (dec8 - dec_hi.astype(f32)).astype(bf16)
    dec_x = (jnp.dot(dec_hi, e_mat, preferred_element_type=f32)
             + jnp.dot(dec_lo, e_mat, preferred_element_type=f32))[0:1, :]

    y_parts = []
    for g in range(SSM_G):
        y_intra, y_inter = y_cols[g]
        sl = slice(g * SSM_GW, (g + 1) * SSM_GW)
        yg = y_intra + y_inter * e_acs_x[:, sl] + dskip_ref[:, sl] * xs[:, sl]
        yg = yg * zs_ref[0][:, sl].astype(f32)
        y_parts.append(_rms(yg, gssm_ref[:, sl], SSM_EPS))
        bg = new_h[g][0]
        upd = lax.dot_general(bg, xw[:, sl], (((0,), (0,)), ((), ())), preferred_element_type=f32)
        hT[g] = hT[g] * dec_x[:, sl] + upd
    y_ref[0] = jnp.concatenate(y_parts, axis=1).astype(bf16)

    @pl.when(c == pl.num_programs(1) - 1)
    def _():
        for g in range(SSM_G):
            ssmo_ref[0, g * SSM_GW:(g + 1) * SSM_GW, :] = hT[g].T


def _ssd(xbc, zs, dt, conv0, ssm0, conv_w, conv_b, a_log, d_skip, g_ssm, e_mat):
    B, L, _ = xbc.shape
    Q = min(SSD_CHUNK, L)
    assert L % Q == 0 and Q % SUBLANES == 0
    nc = L // Q
    dtT = jnp.swapaxes(dt[:, :, :SSM_H], 1, 2)
    alog_row = jnp.pad(a_log.reshape(1, SSM_H), ((0, 0), (0, LANES - SSM_H)))
    alog_col = a_log.reshape(SSM_H, 1)
    dskip_x = jnp.repeat(d_skip, SSM_P).reshape(1, D_INNER)
    kern = functools.partial(_ssd_kernel, Q=Q)
    cmap = lambda b, c: (0, 0)
    return pl.pallas_call(
        kern,
        out_shape=(jax.ShapeDtypeStruct((B, L, D_INNER), bf16),
                   jax.ShapeDtypeStruct((B, CONV_W - 1, CONV_DIM), f32),
                   jax.ShapeDtypeStruct((B, D_INNER, D_STATE), f32)),
        grid=(B, nc),
        in_specs=[pl.BlockSpec((1, Q, CONV_DIM), lambda b, c: (b, c, 0)),
                  pl.BlockSpec((1, Q, D_INNER), lambda b, c: (b, c, 0)),
                  pl.BlockSpec((1, Q, LANES), lambda b, c: (b, c, 0)),
                  pl.BlockSpec((1, SSM_H, Q), lambda b, c: (b, 0, c)),
                  pl.BlockSpec((1, CONV_W - 1, CONV_DIM), lambda b, c: (b, 0, 0)),
                  pl.BlockSpec((1, D_INNER, D_STATE), lambda b, c: (b, 0, 0)),
                  pl.BlockSpec((CONV_W, CONV_DIM), cmap), pl.BlockSpec((1, CONV_DIM), cmap),
                  pl.BlockSpec((1, LANES), cmap), pl.BlockSpec((SSM_H, 1), cmap),
                  pl.BlockSpec((1, D_INNER), cmap), pl.BlockSpec((1, D_INNER), cmap),
                  pl.BlockSpec((LANES, D_INNER), cmap)],
        out_specs=(pl.BlockSpec((1, Q, D_INNER), lambda b, c: (b, c, 0)),
                   pl.BlockSpec((1, CONV_W - 1, CONV_DIM), lambda b, c: (b, 0, 0)),
                   pl.BlockSpec((1, D_INNER, D_STATE), lambda b, c: (b, 0, 0))),
        scratch_shapes=[pltpu.VMEM((SSM_G, D_STATE, SSM_GW), f32),
                        pltpu.VMEM((SUBLANES + Q, CONV_DIM), f32)],
        compiler_params=_params(("parallel", "arbitrary")),
        name="ssd_scan",
    )(xbc, zs, dt, dtT, conv0, ssm0, conv_w, conv_b.reshape(1, CONV_DIM), alog_row, alog_col,
      dskip_x, g_ssm.reshape(1, D_INNER), e_mat)


def _route(lg):
    lane = lax.broadcasted_iota(i32, lg.shape, 1)
    lanef = lane.astype(f32)
    big = float(LANES)
    gl = jnp.where(lane < N_EGROUPS, lg, NEG)
    gmax = jnp.max(gl, axis=1, keepdims=True)
    gsel = jnp.min(jnp.where(gl == gmax, lanef, big), axis=1, keepdims=True)
    p_g = 1.0 / jnp.sum(jnp.exp(gl - gmax), axis=1, keepdims=True)
    first = N_EGROUPS + E_PER_GROUP * gsel
    emask = (lanef >= first) & (lanef < first + E_PER_GROUP)
    el = jnp.where(emask, lg, NEG)
    v1 = jnp.max(el, axis=1, keepdims=True)
    i1 = jnp.min(jnp.where(emask & (el == v1), lanef, big), axis=1, keepdims=True)
    emask2 = emask & (lanef != i1)
    el2 = jnp.where(emask2, lg, NEG)
    v2 = jnp.max(el2, axis=1, keepdims=True)
    i2 = jnp.min(jnp.where(emask2 & (el2 == v2), lanef, big), axis=1, keepdims=True)
    t = jnp.exp(v2 - v1)
    w1 = p_g / (1.0 + t)
    w2 = p_g * t / (1.0 + t)
    out = jnp.where(lane == 0, i1 - N_EGROUPS, 0.0)
    out = jnp.where(lane == 1, i2 - N_EGROUPS, out)
    out = jnp.where(lane == 2, w1, out)
    out = jnp.where(lane == 3, w2, out)
    return out


def _to_row_tiles(ref, x, rows):
    for j in range(D_MODEL // LANES):
        ref[pl.ds(j, rows, stride=SUBLANES), :] = x[:, j * LANES:(j + 1) * LANES]


def _from_row_tiles(ref, rows):
    return jnp.concatenate([ref[pl.ds(j, rows, stride=SUBLANES), :] for j in range(D_MODEL // LANES)], axis=1)


def _mix_kernel(x_ref, gt1_ref, sc2_ref, sh2_ref, o_ref, ys_ref, ga_ref, gs_ref,
                woa_ref, wos_ref, wo_ref, gpost_ref, gpre_ref, wr_ref, br_ref,
                x1_ref, u2_ref, route_ref):
    tm = x_ref.shape[0]
    ya = jnp.dot(o_ref[...], woa_ref[...], preferred_element_type=f32)
    ys = jnp.dot(ys_ref[...], wos_ref[...], preferred_element_type=f32)
    m = ga_ref[...].astype(f32) * ya + gs_ref[...].astype(f32) * ys
    mo = jnp.dot(m.astype(bf16), wo_ref[...], preferred_element_type=f32)
    x1 = x_ref[...] + gt1_ref[...] * _rms(mo, gpost_ref[...], EPS)
    x1_ref[...] = x1
    u2 = _rms(x1, gpre_ref[...], EPS) * (1.0 + sc2_ref[...]) + sh2_ref[...]
    _to_row_tiles(u2_ref, u2, tm)
    lg = jnp.dot(u2.astype(bf16), wr_ref[...], preferred_element_type=f32) + br_ref[...]
    route_ref[...] = _route(lg)


def _mix(grp, x2, o, ysn, ga, gs, w_oa, w_os, w_o, g_post1, g_pre2, w_r, b_r):
    T, tm = grp.T, grp.tm
    return pl.pallas_call(
        _mix_kernel,
        out_shape=(jax.ShapeDtypeStruct((T, D_MODEL), f32),
                   jax.ShapeDtypeStruct((T * SUBLANES, LANES), f32),
                   jax.ShapeDtypeStruct((T, LANES), f32)),
        grid=(grp.n_tiles,),
        in_specs=[grp.row_spec(D_MODEL), grp.mod_spec(2), grp.mod_spec(4), grp.mod_spec(3),
                  grp.row_spec(ATTN_W), grp.row_spec(D_INNER), grp.row_spec(D_MODEL), grp.row_spec(D_MODEL),
                  _const_spec(w_oa.shape), _const_spec(w_os.shape), _const_spec(w_o.shape),
                  _const_spec((1, D_MODEL)), _const_spec((1, D_MODEL)),
                  _const_spec(w_r.shape), _const_spec((1, LANES))],
        out_specs=(grp.row_spec(D_MODEL),
                   pl.BlockSpec((tm * SUBLANES, LANES), lambda i: (i, 0)),
                   grp.row_spec(LANES)),
        compiler_params=_params(("parallel",)),
        name="mixer_out",
    )(x2, grp.mod, grp.mod, grp.mod, o, ysn, ga, gs, w_oa, w_os, w_o, g_post1, g_pre2, w_r, b_r)


def _count_kernel(route_ref, cnt_ref):
    i = pl.program_id(0)

    @pl.when(i == 0)
    def _():
        cnt_ref[...] = jnp.zeros(cnt_ref.shape, f32)

    r = route_ref[...]
    lanef = lax.broadcasted_iota(i32, r.shape, 1).astype(f32)
    hits = (lanef == r[:, 0:1]).astype(f32) + (lanef == r[:, 1:2]).astype(f32)
    cnt_ref[...] += jnp.sum(hits, axis=0, keepdims=True)


def _expert_counts(route, tm):
    T = route.shape[0]
    return pl.pallas_call(
        _count_kernel,
        out_shape=jax.ShapeDtypeStruct((1, LANES), f32),
        grid=(T // tm,),
        in_specs=[pl.BlockSpec((tm, LANES), lambda i: (i, 0))],
        out_specs=pl.BlockSpec((1, LANES), lambda i: (0, 0)),
        compiler_params=_params(("arbitrary",)),
        name="moe_count",
    )(route)


def _slot_kernel(route_ref, start_ref, dest_ref, carry):
    i = pl.program_id(0)

    @pl.when(i == 0)
    def _():
        carry[...] = jnp.broadcast_to(start_ref[...], carry.shape)

    r = route_ref[...]
    tm = r.shape[0]
    lane = lax.broadcasted_iota(i32, r.shape, 1)
    lanef = lane.astype(f32)
    oh1 = (lanef == r[:, 0:1]).astype(f32)
    oh2 = (lanef == r[:, 1:2]).astype(f32)
    cnt = oh1 + oh2
    ri = lax.broadcasted_iota(i32, (tm, tm), 0)
    ci = lax.broadcasted_iota(i32, (tm, tm), 1)
    before = (ci < ri).astype(bf16)
    pos = jnp.dot(before, cnt.astype(bf16), preferred_element_type=f32) + carry[0:1, :]
    d1 = jnp.sum(oh1 * pos, axis=1, keepdims=True)
    d2 = jnp.sum(oh2 * pos, axis=1, keepdims=True)
    dest_ref[...] = jnp.where(lane == 0, d1, jnp.where(lane == 1, d2, 0.0)).astype(i32)
    carry[...] = carry[...] + jnp.sum(cnt, axis=0, keepdims=True)


def _slots(route, starts, tm):
    T = route.shape[0]
    return pl.pallas_call(
        _slot_kernel,
        out_shape=jax.ShapeDtypeStruct((T, LANES), i32),
        grid=(T // tm,),
        in_specs=[pl.BlockSpec((tm, LANES), lambda i: (i, 0)), pl.BlockSpec((1, LANES), lambda i: (0, 0))],
        out_specs=pl.BlockSpec((tm, LANES), lambda i: (i, 0)),
        scratch_shapes=[pltpu.VMEM((SUBLANES, LANES), f32)],
        compiler_params=_params(("arbitrary",)),
        name="moe_slots",
    )(route, starts)


def _row_copy(src, src_row, dst, dst_row, sem):
    return pltpu.make_async_copy(src.at[pl.ds(pl.multiple_of(src_row * SUBLANES, SUBLANES), SUBLANES), :],
                                 dst.at[pl.ds(pl.multiple_of(dst_row * SUBLANES, SUBLANES), SUBLANES), :], sem)


def _dispatch_kernel(dest_ref, u_ref, xs_in_ref, xs_ref, sem, *, tm):
    del xs_in_ref

    def issue(t, c):
        for k in range(2):
            _row_copy(u_ref, t, xs_ref, dest_ref[0, 0, 2 * t + k], sem).start()
        return c

    lax.fori_loop(0, tm, issue, 0)

    def drain(t, c):
        for k in range(2):
            _row_copy(u_ref, 0, xs_ref, 0, sem).wait()
        return c

    lax.fori_loop(0, tm, drain, 0)


def _dispatch(dest2, u_rows, n_slots, tm):
    T = dest2.shape[0]
    xs0 = jnp.zeros((n_slots * SUBLANES, LANES), f32)
    dest3 = dest2.reshape(T // tm, 1, 2 * tm)
    return pl.pallas_call(
        functools.partial(_dispatch_kernel, tm=tm),
        out_shape=jax.ShapeDtypeStruct((n_slots * SUBLANES, LANES), f32),
        grid=(T // tm,),
        in_specs=[pl.BlockSpec((1, 1, 2 * tm), lambda i: (i, 0, 0), memory_space=pltpu.SMEM),
                  pl.BlockSpec((tm * SUBLANES, LANES), lambda i: (i, 0)), pl.BlockSpec(memory_space=pl.ANY)],
        out_specs=pl.BlockSpec(memory_space=pl.ANY),
        scratch_shapes=[pltpu.SemaphoreType.DMA(())],
        input_output_aliases={2: 0},
        compiler_params=_params(("arbitrary",)),
        name="moe_dispatch",
    )(dest3, u_rows, xs0)


def _expert_kernel(be_ref, xs_ref, wg_ref, wu_ref, wd_ref, o_ref, *, blk):
    del be_ref
    x = _from_row_tiles(xs_ref, blk).astype(bf16)
    h = _silu(jnp.dot(x, wg_ref[...], preferred_element_type=f32)) * jnp.dot(x, wu_ref[...], preferred_element_type=f32)
    _to_row_tiles(o_ref, jnp.dot(h.astype(bf16), wd_ref[...], preferred_element_type=f32), blk)


def _experts(xs, blk_e, w_gate, w_up, w_down, blk):
    n_blocks = blk_e.shape[0]
    rows = blk * SUBLANES
    grid_spec = pltpu.PrefetchScalarGridSpec(
        num_scalar_prefetch=1,
        grid=(n_blocks,),
        in_specs=[pl.BlockSpec((rows, LANES), lambda i, be: (i, 0)),
                  pl.BlockSpec((None, D_MODEL, D_FF), lambda i, be: (be[i], 0, 0)),
                  pl.BlockSpec((None, D_MODEL, D_FF), lambda i, be: (be[i], 0, 0)),
                  pl.BlockSpec((None, D_FF, D_MODEL), lambda i, be: (be[i], 0, 0))],
        out_specs=pl.BlockSpec((rows, LANES), lambda i, be: (i, 0)),
    )
    return pl.pallas_call(
        functools.partial(_expert_kernel, blk=blk),
        out_shape=jax.ShapeDtypeStruct(xs.shape, f32),
        grid_spec=grid_spec,
        compiler_params=_params(("parallel",)),
        name="moe_experts",
    )(blk_e, xs, w_gate, w_up, w_down)


def _combine_kernel(dest_ref, outs_ref, x1_ref, gt2_ref, route_ref, gpost_ref, y_ref, buf_a, buf_b, sem, *, tm):
    def issue(t, c):
        _row_copy(outs_ref, dest_ref[0, 0, 2 * t], buf_a, t, sem).start()
        _row_copy(outs_ref, dest_ref[0, 0, 2 * t + 1], buf_b, t, sem).start()
        return c

    lax.fori_loop(0, tm, issue, 0)

    def drain(t, c):
        _row_copy(outs_ref, 0, buf_a, 0, sem).wait()
        _row_copy(outs_ref, 0, buf_b, 0, sem).wait()
        return c

    lax.fori_loop(0, tm, drain, 0)
    r = route_ref[...]
    f = r[:, 2:3] * _from_row_tiles(buf_a, tm) + r[:, 3:4] * _from_row_tiles(buf_b, tm)
    y_ref[...] = x1_ref[...] + gt2_ref[...] * _rms(f, gpost_ref[...], EPS)


def _combine(grp, dest2, outs, x1, route, g_post2, tm):
    T = grp.T
    dest3 = dest2.reshape(T // tm, 1, 2 * tm)
    return pl.pallas_call(
        functools.partial(_combine_kernel, tm=tm),
        out_shape=jax.ShapeDtypeStruct((T, D_MODEL), f32),
        grid=(T // tm,),
        in_specs=[pl.BlockSpec((1, 1, 2 * tm), lambda i: (i, 0, 0), memory_space=pltpu.SMEM),
                  pl.BlockSpec(memory_space=pl.ANY),
                  grp.row_spec(D_MODEL), grp.mod_spec(5), grp.row_spec(LANES), _const_spec((1, D_MODEL))],
        out_specs=grp.row_spec(D_MODEL),
        scratch_shapes=[pltpu.VMEM((tm * SUBLANES, LANES), f32), pltpu.VMEM((tm * SUBLANES, LANES), f32),
                        pltpu.SemaphoreType.DMA(())],
        compiler_params=_params(("arbitrary",)),
        name="moe_combine",
    )(dest3, outs, x1, grp.mod, route, g_post2)


def _hmoe_and_residual(grp, x1, u_rows, route, w_gate, w_up, w_down, g_post2):
    T = grp.T
    tm = grp.tm
    counts = _expert_counts(route, tm)[0, :N_EXPERTS].astype(i32)
    padded = (counts + MOE_BLOCK - 1) // MOE_BLOCK * MOE_BLOCK
    ends = jnp.cumsum(padded)
    starts = ends - padded
    n_blocks = -(-2 * T // MOE_BLOCK) + N_EXPERTS
    n_slots = n_blocks * MOE_BLOCK
    blk_start = jnp.arange(n_blocks, dtype=i32) * MOE_BLOCK
    blk_e = jnp.minimum(jnp.sum(blk_start[:, None] >= ends[None, :], axis=1), N_EXPERTS - 1).astype(i32)
    starts_row = jnp.pad(starts.astype(f32), (0, LANES - N_EXPERTS)).reshape(1, LANES)
    dest2 = _slots(route, starts_row, tm)[:, :2]
    xs = _dispatch(dest2, u_rows, n_slots, tm)
    outs = _experts(xs, blk_e, w_gate, w_up, w_down, MOE_BLOCK)
    return _combine(grp, dest2, outs, x1, route, g_post2, tm)


def _rope_tables(pos):
    half = HEAD_DIM // 2
    inv = ROPE_THETA ** (-2.0 * jnp.arange(half, dtype=f32) / HEAD_DIM)
    ang = pos.astype(f32)[:, None] * inv[None, :]
    cos, sin = jnp.cos(ang), jnp.sin(ang)
    return jnp.tile(cos, (1, LANES // half)), jnp.tile(jnp.concatenate([-sin, sin], axis=1), (1, LANES // HEAD_DIM))


def _layer(grp, x2, lw, lam_init, attend, conv0, ssm0):
    n_seq, seq_len = grp.n_seq, grp.seq_len
    q, k, kb, v, vb = _proj_qkv(grp, x2, lw["g_pre1"], lw["w_qkv"], 1, 0)
    zs, xbc = _proj_zx(grp, x2, lw["g_pre1"], lw["w_z"], lw["w_xbc"], 1, 0)
    ga, gs, dt = _proj_gates(grp, x2, lw["g_pre1"], lw["w_gates"], lw["w_dt"], lw["dt_bias"], 1, 0)
    shp = lambda a: a.reshape(n_seq, seq_len, a.shape[-1])
    o = attend(shp(q), shp(kb), shp(vb))
    ysn, conv_new, ssm_new = _ssd(shp(xbc), shp(zs), shp(dt), conv0, ssm0.reshape(n_seq, D_INNER, D_STATE),
                                  lw["conv_w"], lw["conv_b"], lw["a_log"], lw["d_skip"], lw["g_ssm"], lw["e_mat"])
    x1, u_rows, route = _mix(grp, x2, o.reshape(grp.T, ATTN_W), ysn.reshape(grp.T, D_INNER), ga, gs,
                             lw["w_oa"], lw["w_os"], lw["w_o"], lw["g_post1"], lw["g_pre2"], lw["w_r"], lw["b_r"])
    y2 = _hmoe_and_residual(grp, x1, u_rows, route, lw["w_gate"], lw["w_up"], lw["w_down"], lw["g_post2"])
    return (y2, k.reshape(n_seq, seq_len, N_KV, HEAD_W), v.reshape(n_seq, seq_len, N_KV, HEAD_W), conv_new,
            ssm_new.reshape(n_seq, SSM_H, SSM_P, D_STATE))


def kernel(x_prompt, x_sample, cache_k, cache_v, state_conv, state_ssm, page_table, c_prompt, c_sample, w_ada, b_ada, g_pre1, g_post1, g_pre2, g_post2, w_in, lambda_q1, lambda_k1, lambda_q2, lambda_k2, g_subln, w_oa, conv_w, conv_b, dt_bias, a_log, d_skip, g_ssm, w_os, w_o, w_rg, b_rg, w_re, b_re, w_gate, w_up, w_down):
    depth = w_in.shape[0]
    bp, lp_len, _ = x_prompt.shape
    bs, ls_len, _ = x_sample.shape
    past = page_table.shape[1] * cache_k.shape[2]
    ck = cache_k.reshape(cache_k.shape[0], cache_k.shape[1], cache_k.shape[2] * N_KV, HEAD_W)
    cv = cache_v.reshape(cache_v.shape[0], cache_v.shape[1], cache_v.shape[2] * N_KV, HEAD_W)
    rope_p = _rope_tables(jnp.arange(lp_len, dtype=i32))
    rope_s8 = _rope_tables(past + jnp.arange(ls_len, dtype=i32))
    rope_s = tuple(jnp.tile(t, (bs, 1)) for t in rope_s8)
    e_mat = (jnp.arange(LANES, dtype=i32)[:, None] == jnp.arange(D_INNER, dtype=i32)[None, :] // SSM_P).astype(bf16)

    hp = x_prompt.reshape(bp * lp_len, D_MODEL)
    hs = x_sample.reshape(bs * ls_len, D_MODEL)
    c_all = jnp.concatenate([c_prompt, c_sample], axis=0)
    outs = [[] for _ in range(8)]
    for l in range(depth):
        lam_init = 0.8 - 0.6 * math.exp(-0.3 * l)
        w_in_l = w_in[l].astype(bf16)
        lw = dict(
            w_qkv=w_in_l[:, :OFF_Z], w_z=w_in_l[:, OFF_Z:OFF_XBC], w_xbc=w_in_l[:, OFF_XBC:OFF_DT],
            w_dt=jnp.pad(w_in_l[:, OFF_DT:OFF_GA], ((0, 0), (0, LANES - SSM_H))), w_gates=w_in_l[:, OFF_GA:],
            dt_bias=jnp.pad(dt_bias[l].reshape(1, SSM_H), ((0, 0), (0, LANES - SSM_H))),
            g_pre1=g_pre1[l].reshape(1, D_MODEL), g_post1=g_post1[l].reshape(1, D_MODEL),
            g_pre2=g_pre2[l].reshape(1, D_MODEL), g_post2=g_post2[l].reshape(1, D_MODEL),
            w_oa=w_oa[l].astype(bf16), w_os=w_os[l].astype(bf16), w_o=w_o[l].astype(bf16),
            conv_w=conv_w[l], conv_b=conv_b[l], a_log=a_log[l], d_skip=d_skip[l], g_ssm=g_ssm[l], e_mat=e_mat,
            w_r=jnp.pad(jnp.concatenate([w_rg[l], w_re[l]], axis=1),
                        ((0, 0), (0, LANES - N_EGROUPS - N_EXPERTS))).astype(bf16),
            b_r=jnp.pad(jnp.concatenate([b_rg[l], b_re[l]]), (0, LANES - N_EGROUPS - N_EXPERTS)).reshape(1, LANES),
            w_gate=w_gate[l].astype(bf16), w_up=w_up[l].astype(bf16), w_down=w_down[l].astype(bf16),
        )
        lam_p = jnp.stack([lambda_q1[l], lambda_k1[l], lambda_q2[l], lambda_k2[l]])
        gsub = g_subln[l].reshape(1, HEAD_W)
        mod = _ada(c_all, w_ada[l].astype(bf16), b_ada[l])
        grp_p = _Group(bp, lp_len, mod[:bp], rope_p, per_token=False)
        grp_s = _Group(bs, ls_len, jnp.repeat(mod[bp:], ls_len, axis=0), rope_s, per_token=True)

        attend_p = lambda q, kb, vb: _attn_prompt(q, kb, vb, lam_p, gsub, lam_init)
        attend_s = lambda q, kb, vb: _attn_sample(q, kb, vb, ck, cv, page_table, l, lam_p, gsub, lam_init)
        conv0 = jnp.zeros((bp, CONV_W - 1, CONV_DIM), f32)
        ssm0 = jnp.zeros((bp, SSM_H, SSM_P, D_STATE), f32)
        hp, kp, vp, cp, sp = _layer(grp_p, hp, lw, lam_init, attend_p, conv0, ssm0)
        hs, ksm, vsm, csm, ssm = _layer(grp_s, hs, lw, lam_init, attend_s, state_conv[l], state_ssm[l])
        for lst, val in zip(outs, (kp, vp, cp, sp, ksm, vsm, csm, ssm)):
            lst.append(val)
    stacked = [jnp.stack(o) for o in outs]
    return (hp.reshape(bp, lp_len, D_MODEL), hs.reshape(bs, ls_len, D_MODEL), *stacked)
```

```python
import functools
import math

import jax
import jax.numpy as jnp
from jax import lax
from jax.experimental import pallas as pl
from jax.experimental.pallas import tpu as pltpu

f32 = jnp.float32
bf16 = jnp.bfloat16
i32 = jnp.int32

D_MODEL = 1024
N_HEADS = 8
N_KV = 4
HEAD_DIM = 64
HEAD_W = 2 * HEAD_DIM
ATTN_W = N_HEADS * HEAD_W
KV_W = N_KV * HEAD_W
ATTN_SCALE = HEAD_DIM ** -0.5
ROPE_THETA = 10000.0
SUBLN_EPS = 1e-5
D_INNER = 2048
SSM_P = 64
SSM_H = D_INNER // SSM_P
SSM_G = 4
SSM_GW = D_INNER // SSM_G
D_STATE = 128
CONV_W = 4
CONV_DIM = D_INNER + 2 * SSM_G * D_STATE
SSD_CHUNK = 128
SSM_EPS = 1e-5
N_EGROUPS = 4
E_PER_GROUP = 8
N_EXPERTS = 32
D_FF = 512
EPS = 1e-6
OFF_K = ATTN_W
OFF_V = OFF_K + KV_W
OFF_Z = OFF_V + KV_W
OFF_XBC = OFF_Z + D_INNER
OFF_DT = OFF_XBC + CONV_DIM
OFF_GA = OFF_DT + SSM_H
OFF_GS = OFF_GA + D_MODEL

LANES = 128
SUBLANES = 8
VMEM_LIMIT = 56 * 1024 * 1024
NEG = -1e30
LOG2E = math.log2(math.e)

TOKEN_TILE = 512
ATTN_TILE = 512
PAGES_PER_STEP = 8
MOE_BLOCK = 512
SSD_SEQS_PER_STEP = 1


def _params(sem, vmem=VMEM_LIMIT):
    return pltpu.CompilerParams(dimension_semantics=sem, vmem_limit_bytes=vmem)


def _sigmoid(x):
    return 1.0 / (1.0 + jnp.exp(-x))


def _silu(x):
    return x * _sigmoid(x)


def _rms(x, g, eps):
    return x * lax.rsqrt(jnp.mean(x * x, axis=-1, keepdims=True) + eps) * g


def _bdot(a, b):
    return jnp.dot(a.astype(bf16), b.astype(bf16), preferred_element_type=f32)


class _Group:
    def __init__(self, n_seq, seq_len, mod_rows, rope_rows, per_token):
        self.n_seq, self.seq_len = n_seq, seq_len
        self.T = n_seq * seq_len
        self.tm = min(TOKEN_TILE, self.T if per_token else seq_len)
        assert self.T % self.tm == 0
        self.per_token = per_token
        if per_token:
            self.mod = mod_rows
            self.rope = rope_rows
        else:
            assert seq_len % self.tm == 0
            self.tiles_per_seq = seq_len // self.tm
            self.mod = mod_rows.reshape(n_seq, 1, 6 * D_MODEL)
            self.rope = rope_rows
        self.n_tiles = self.T // self.tm

    def mod_spec(self, col):
        if self.per_token:
            return pl.BlockSpec((self.tm, D_MODEL), lambda i: (i, col))
        tps = self.tiles_per_seq
        return pl.BlockSpec((None, 1, D_MODEL), lambda i: (i // tps, 0, col))

    def rope_spec(self):
        if self.per_token:
            return pl.BlockSpec((self.tm, LANES), lambda i: (i, 0))
        tps = self.tiles_per_seq
        return pl.BlockSpec((self.tm, LANES), lambda i: (i % tps, 0))

    def row_spec(self, width):
        return pl.BlockSpec((self.tm, width), lambda i: (i, 0))


def _const_spec(shape):
    nd = len(shape)
    return pl.BlockSpec(shape, lambda i: (0,) * nd)


def _ada_kernel(c_ref, w_ref, b_ref, o_ref):
    o_ref[...] = _bdot(_silu(c_ref[...]), w_ref[...]) + b_ref[...]


def _ada(c, w_ada, b_ada):
    m = c.shape[0]
    n = w_ada.shape[1]
    tn = D_MODEL
    return pl.pallas_call(
        _ada_kernel,
        out_shape=jax.ShapeDtypeStruct((m, n), f32),
        grid=(n // tn,),
        in_specs=[pl.BlockSpec((m, D_MODEL), lambda j: (0, 0)),
                  pl.BlockSpec((D_MODEL, tn), lambda j: (0, j)),
                  pl.BlockSpec((1, tn), lambda j: (0, j))],
        out_specs=pl.BlockSpec((m, tn), lambda j: (0, j)),
        compiler_params=_params(("parallel",)),
        name="ada_mod",
    )(c, w_ada, b_ada.reshape(1, n))


def _modulated(x_ref, g_ref, sc_ref, sh_ref):
    u = _rms(x_ref[...], g_ref[...], EPS) * (1.0 + sc_ref[...]) + sh_ref[...]
    return u.astype(bf16)


def _rope_cols(x, n_blocks, cos, sin_signed, lo):
    out = []
    for j in range(n_blocks):
        xb = x[:, j * LANES:(j + 1) * LANES]
        partner = jnp.where(lo, pltpu.roll(xb, LANES - HEAD_DIM // 2, 1), pltpu.roll(xb, HEAD_DIM // 2, 1))
        out.append(xb * cos + partner * sin_signed)
    return jnp.concatenate(out, axis=1)


def _qkv_kernel(x_ref, g_ref, sc_ref, sh_ref, w_ref, cos_ref, sin_ref,
                q_ref, k_ref, kb_ref, v_ref, vb_ref):
    u = _modulated(x_ref, g_ref, sc_ref, sh_ref)
    proj = jnp.dot(u, w_ref[...], preferred_element_type=f32)
    cos, sin_signed = cos_ref[...], sin_ref[...]
    lane = lax.broadcasted_iota(i32, cos.shape, 1)
    lo = jnp.bitwise_and(lane, HEAD_DIM - 1) < HEAD_DIM // 2
    q = _rope_cols(proj[:, :OFF_K], N_HEADS, cos, sin_signed, lo)
    k = _rope_cols(proj[:, OFF_K:OFF_V], N_KV, cos, sin_signed, lo)
    v = proj[:, OFF_V:OFF_Z]
    q_ref[...] = (q * (ATTN_SCALE * LOG2E)).astype(bf16)
    kb_ref[...] = k.astype(bf16)
    vb_ref[...] = v.astype(bf16)
    tm = k.shape[0]
    for h in range(N_KV):
        k_ref[pl.ds(h, tm, stride=N_KV), :] = k[:, h * HEAD_W:(h + 1) * HEAD_W]
        v_ref[pl.ds(h, tm, stride=N_KV), :] = v[:, h * HEAD_W:(h + 1) * HEAD_W]


def _proj_qkv(grp, x2, g, w_qkv, col_scale, col_shift):
    T, tm = grp.T, grp.tm
    head_rows = pl.BlockSpec((tm * N_KV, HEAD_W), lambda i: (i, 0))
    return pl.pallas_call(
        _qkv_kernel,
        out_shape=(jax.ShapeDtypeStruct((T, ATTN_W), bf16),
                   jax.ShapeDtypeStruct((T * N_KV, HEAD_W), f32), jax.ShapeDtypeStruct((T, KV_W), bf16),
                   jax.ShapeDtypeStruct((T * N_KV, HEAD_W), f32), jax.ShapeDtypeStruct((T, KV_W), bf16)),
        grid=(grp.n_tiles,),
        in_specs=[grp.row_spec(D_MODEL), _const_spec((1, D_MODEL)),
                  grp.mod_spec(col_scale), grp.mod_spec(col_shift),
                  _const_spec(w_qkv.shape), grp.rope_spec(), grp.rope_spec()],
        out_specs=(grp.row_spec(ATTN_W), head_rows, grp.row_spec(KV_W),
                   head_rows, grp.row_spec(KV_W)),
        compiler_params=_params(("parallel",)),
        name="proj_qkv",
    )(x2, g, grp.mod, grp.mod, w_qkv, grp.rope[0], grp.rope[1])


def _zx_kernel(x_ref, g_ref, sc_ref, sh_ref, wz_ref, wx_ref, zs_ref, xbc_ref):
    u = _modulated(x_ref, g_ref, sc_ref, sh_ref)
    zs_ref[...] = _silu(jnp.dot(u, wz_ref[...], preferred_element_type=f32)).astype(bf16)
    xbc_ref[...] = jnp.dot(u, wx_ref[...], preferred_element_type=f32).astype(bf16)


def _proj_zx(grp, x2, g, w_z, w_xbc, col_scale, col_shift):
    T = grp.T
    return pl.pallas_call(
        _zx_kernel,
        out_shape=(jax.ShapeDtypeStruct((T, D_INNER), bf16), jax.ShapeDtypeStruct((T, CONV_DIM), bf16)),
        grid=(grp.n_tiles,),
        in_specs=[grp.row_spec(D_MODEL), _const_spec((1, D_MODEL)),
                  grp.mod_spec(col_scale), grp.mod_spec(col_shift),
                  _const_spec(w_z.shape), _const_spec(w_xbc.shape)],
        out_specs=(grp.row_spec(D_INNER), grp.row_spec(CONV_DIM)),
        compiler_params=_params(("parallel",)),
        name="proj_zx",
    )(x2, g, grp.mod, grp.mod, w_z, w_xbc)


def _gates_kernel(x_ref, g_ref, sc_ref, sh_ref, wg_ref, wdt_ref, dtb_ref, ga_ref, gs_ref, dt_ref):
    u = _modulated(x_ref, g_ref, sc_ref, sh_ref)
    gates = _sigmoid(jnp.dot(u, wg_ref[...], preferred_element_type=f32))
    ga_ref[...] = gates[:, :D_MODEL].astype(bf16)
    gs_ref[...] = gates[:, D_MODEL:].astype(bf16)
    raw = jnp.dot(u, wdt_ref[...], preferred_element_type=f32) + dtb_ref[...]
    dt_ref[...] = jnp.maximum(raw, 0.0) + jnp.log1p(jnp.exp(-jnp.abs(raw)))


def _proj_gates(grp, x2, g, w_gates, w_dt, dt_bias, col_scale, col_shift):
    T = grp.T
    return pl.pallas_call(
        _gates_kernel,
        out_shape=(jax.ShapeDtypeStruct((T, D_MODEL), bf16), jax.ShapeDtypeStruct((T, D_MODEL), bf16),
                   jax.ShapeDtypeStruct((T, LANES), f32)),
        grid=(grp.n_tiles,),
        in_specs=[grp.row_spec(D_MODEL), _const_spec((1, D_MODEL)),
                  grp.mod_spec(col_scale), grp.mod_spec(col_shift),
                  _const_spec(w_gates.shape), _const_spec(w_dt.shape), _const_spec((1, LANES))],
        out_specs=(grp.row_spec(D_MODEL), grp.row_spec(D_MODEL), grp.row_spec(LANES)),
        compiler_params=_params(("parallel",)),
        name="proj_gates",
    )(x2, g, grp.mod, grp.mod, w_gates, w_dt, dt_bias)


def _lambda_full(lam_ref, lam_init):
    lp = lam_ref[...]
    l1 = jnp.sum(lp[0:1] * lp[1:2], axis=1, keepdims=True)
    l2 = jnp.sum(lp[2:3] * lp[3:4], axis=1, keepdims=True)
    return jnp.exp(l1) - jnp.exp(l2) + lam_init


def _with_ones(v):
    return jnp.concatenate([v, jnp.ones(v.shape, bf16)], axis=1)


def _softmax_step(s, v1, m_ref, acc_ref):
    reps = s.shape[1] // LANES if s.shape[1] % LANES == 0 else 0
    m_prev = m_ref[...]
    m_new = jnp.maximum(m_prev, jnp.max(s, axis=1, keepdims=True))
    alpha = jnp.exp2(m_prev - m_new)
    if reps:
        p = jnp.exp2(s - jnp.concatenate([m_new] * reps, axis=1))
    else:
        p = jnp.exp2(s - m_new[:, 0:1])
    pv = jnp.dot(p.astype(bf16), v1, preferred_element_type=f32)
    if v1.shape[1] == HEAD_W:
        pv = jnp.concatenate([pv, jnp.broadcast_to(jnp.sum(p, axis=1, keepdims=True), pv.shape)], axis=1)
    acc_ref[...] = jnp.concatenate([alpha, alpha], axis=1) * acc_ref[...] + pv
    m_ref[...] = m_new


def _diff_heads(acc, lam, gsub, lam_init, rows):
    outs = []
    for r in range(2):
        a0 = acc[(2 * r) * rows:(2 * r + 1) * rows]
        a1 = acc[(2 * r + 1) * rows:(2 * r + 2) * rows]
        o = a0[:, :HEAD_W] / a0[:, HEAD_W:] - lam * (a1[:, :HEAD_W] / a1[:, HEAD_W:])
        outs.append(_rms(o, gsub, SUBLN_EPS) * (1.0 - lam_init))
    return jnp.concatenate(outs, axis=1)


def _pad_components(q, rows):
    lane = lax.broadcasted_iota(i32, (rows, LANES), 1)
    lo = lane < HEAD_DIM
    blocks = []
    for r in range(2):
        qh = q[:, r * LANES:(r + 1) * LANES].astype(f32)
        blocks.append(jnp.where(lo, qh, 0.0))
        blocks.append(jnp.where(lo, 0.0, qh))
    return jnp.concatenate(blocks, axis=0)


def _stacked_causal(n, copies=4):
    assert n & (n - 1) == 0
    row = jnp.bitwise_and(lax.broadcasted_iota(i32, (copies * n, n), 0), n - 1)
    col = lax.broadcasted_iota(i32, (copies * n, n), 1)
    return col <= row


def _attn_prompt_kernel(lam_ref, gsub_ref, q_ref, k_ref, v_ref, o_ref, qpad, m_s, acc_s, s_a, s_b, *, tq, lam_init):
    i = pl.program_id(2)
    qpad[...] = _pad_components(q_ref[0], tq).astype(bf16)
    m_s[...] = jnp.full(m_s.shape, NEG, f32)
    acc_s[...] = jnp.zeros(acc_s.shape, f32)

    def scores(c):
        kc = k_ref[0, pl.ds(pl.multiple_of(c * tq, tq), tq), :]
        return lax.dot_general(qpad[...], kc, (((1,), (1,)), ((), ())), preferred_element_type=f32)

    def consume(s_ref, c):
        _softmax_step(s_ref[...], _with_ones(v_ref[0, pl.ds(pl.multiple_of(c * tq, tq), tq), :]), m_s, acc_s)

    s_a[...] = jnp.where(_stacked_causal(tq), scores(i), NEG)

    def pair(p, carry):
        s_b[...] = scores(2 * p)
        consume(s_a, jnp.where(p == 0, i, 2 * p - 1))
        s_a[...] = scores(jnp.minimum(2 * p + 1, i - 1))
        consume(s_b, 2 * p)
        return carry

    lax.fori_loop(0, jnp.right_shift(i + 1, 1), pair, 0)

    @pl.when(jnp.bitwise_and(i, 1) == 0)
    def _():
        consume(s_a, jnp.maximum(i - 1, 0))

    lam = _lambda_full(lam_ref, lam_init)
    o_ref[0] = _diff_heads(acc_s[...], lam, gsub_ref[...], lam_init, tq).astype(bf16)


def _attn_prompt(q, kb, vb, lam_p, gsub, lam_init):
    B, L, _ = q.shape
    tq = min(ATTN_TILE, L)
    assert L % tq == 0
    kern = functools.partial(_attn_prompt_kernel, tq=tq, lam_init=lam_init)
    return pl.pallas_call(
        kern,
        out_shape=jax.ShapeDtypeStruct((B, L, ATTN_W), bf16),
        grid=(B, N_KV, L // tq),
        in_specs=[pl.BlockSpec((4, HEAD_DIM), lambda b, g, i: (0, 0)),
                  pl.BlockSpec((1, HEAD_W), lambda b, g, i: (0, 0)),
                  pl.BlockSpec((1, tq, 2 * HEAD_W), lambda b, g, i: (b, i, g)),
                  pl.BlockSpec((1, L, HEAD_W), lambda b, g, i: (b, 0, g)),
                  pl.BlockSpec((1, L, HEAD_W), lambda b, g, i: (b, 0, g))],
        out_specs=pl.BlockSpec((1, tq, 2 * HEAD_W), lambda b, g, i: (b, i, g)),
        scratch_shapes=[pltpu.VMEM((4 * tq, LANES), bf16), pltpu.VMEM((4 * tq, LANES), f32),
                        pltpu.VMEM((4 * tq, 2 * LANES), f32),
                        pltpu.VMEM((4 * tq, tq), f32), pltpu.VMEM((4 * tq, tq), f32)],
        compiler_params=_params(("parallel", "parallel", "arbitrary")),
        name="attn_prompt",
    )(lam_p, gsub, q, kb, vb)


def _attn_sample_kernel(pt_ref, lam_ref, gsub_ref, q_ref, kn_ref, vn_ref, *rest, n_pages_step, seq_new, lam_init):
    del pt_ref
    kp = rest[:n_pages_step]
    vp = rest[n_pages_step:2 * n_pages_step]
    o_ref, qpad, m_s, acc_s = rest[2 * n_pages_step:]
    j = pl.program_id(1)

    @pl.when(j == 0)
    def _():
        q = q_ref[0]
        for g in range(N_KV):
            qpad[g] = _pad_components(q[:, g * 2 * HEAD_W:(g + 1) * 2 * HEAD_W], seq_new)
        m_s[...] = jnp.full(m_s.shape, NEG, f32)
        acc_s[...] = jnp.zeros(acc_s.shape, f32)

    page = kp[0].shape[0] // N_KV
    for g in range(N_KV):
        qg = qpad[g].astype(bf16)
        kg = jnp.concatenate([r[pl.ds(g, page, stride=N_KV), :].astype(bf16) for r in kp], axis=0)
        vg = jnp.concatenate([r[pl.ds(g, page, stride=N_KV), :].astype(bf16) for r in vp], axis=0)
        s = lax.dot_general(qg, kg, (((1,), (1,)), ((), ())), preferred_element_type=f32)
        _softmax_step(s, vg, m_s.at[g], acc_s.at[g])

    @pl.when(j == pl.num_programs(1) - 1)
    def _():
        lam = _lambda_full(lam_ref, lam_init)
        causal = _stacked_causal(seq_new)
        outs = []
        for g in range(N_KV):
            qg = qpad[g].astype(bf16)
            kg = kn_ref[0][:, g * HEAD_W:(g + 1) * HEAD_W]
            vg = vn_ref[0][:, g * HEAD_W:(g + 1) * HEAD_W]
            s = lax.dot_general(qg, kg, (((1,), (1,)), ((), ())), preferred_element_type=f32)
            _softmax_step(jnp.where(causal, s, NEG), vg, m_s.at[g], acc_s.at[g])
            outs.append(_diff_heads(acc_s[g], lam, gsub_ref[...], lam_init, seq_new))
        o_ref[0] = jnp.concatenate(outs, axis=1).astype(bf16)


def _attn_sample(q, kb, vb, cache_k, cache_v, page_table, layer, lam_p, gsub, lam_init):
    S, Ls, _ = q.shape
    n_pages = page_table.shape[1]
    page_rows = cache_k.shape[2]
    pps = PAGES_PER_STEP
    while n_pages % pps:
        pps -= 1
    steps = n_pages // pps

    def page_spec(p):
        return pl.BlockSpec((None, None, page_rows, HEAD_W), lambda s, j, pt: (layer, pt[s, j * pps + p], 0, 0))

    kern = functools.partial(_attn_sample_kernel, n_pages_step=pps, seq_new=Ls, lam_init=lam_init)
    grid_spec = pltpu.PrefetchScalarGridSpec(
        num_scalar_prefetch=1,
        grid=(S, steps),
        in_specs=[pl.BlockSpec((4, HEAD_DIM), lambda s, j, pt: (0, 0)),
                  pl.BlockSpec((1, HEAD_W), lambda s, j, pt: (0, 0)),
                  pl.BlockSpec((1, Ls, ATTN_W), lambda s, j, pt: (s, 0, 0)),
                  pl.BlockSpec((1, Ls, KV_W), lambda s, j, pt: (s, 0, 0)),
                  pl.BlockSpec((1, Ls, KV_W), lambda s, j, pt: (s, 0, 0))]
        + [page_spec(p) for p in range(pps)] + [page_spec(p) for p in range(pps)],
        out_specs=pl.BlockSpec((1, Ls, ATTN_W), lambda s, j, pt: (s, 0, 0)),
        scratch_shapes=[pltpu.VMEM((N_KV, 4 * Ls, LANES), f32), pltpu.VMEM((N_KV, 4 * Ls, LANES), f32),
                        pltpu.VMEM((N_KV, 4 * Ls, 2 * LANES), f32)],
    )
    return pl.pallas_call(
        kern,
        out_shape=jax.ShapeDtypeStruct((S, Ls, ATTN_W), bf16),
        grid_spec=grid_spec,
        compiler_params=_params(("parallel", "arbitrary")),
        name="attn_sample",
    )(page_table, lam_p, gsub, q, kb, vb, *([cache_k] * pps), *([cache_v] * pps))


def _split3(x):
    hi = x.astype(bf16)
    r1 = x - hi.astype(f32)
    mid = r1.astype(bf16)
    lo = (r1 - mid.astype(f32)).astype(bf16)
    return hi, mid, lo


def _ssd_kernel(*refs, Q, nb):
    for bi in range(nb):
        _ssd_row(bi, *refs, Q=Q)


def _ssd_row(bi, xbc_ref, zs_ref, dt_ref, dtT_ref, conv0_ref, ssm0_ref, cw_ref, cb_ref,
             alog_ref, alogT_ref, dskip_ref, gssm_ref, e_ref,
             y_ref, convo_ref, ssmo_ref, hT_all, cbuf_all, *, Q):
    c = pl.program_id(1)
    hT, cbuf = hT_all.at[bi], cbuf_all.at[bi]
    tail = CONV_W - 1
    base = SUBLANES

    @pl.when(c == 0)
    def _():
        cbuf[base - tail:base, :] = conv0_ref[bi]
        for g in range(SSM_G):
            hT[g] = ssm0_ref[bi, g * SSM_GW:(g + 1) * SSM_GW, :].T

    cbuf[base:base + Q, :] = xbc_ref[bi].astype(f32)
    acc = cb_ref[...] + cw_ref[CONV_W - 1:CONV_W, :] * cbuf[base:base + Q, :]
    for j in range(tail):
        acc = acc + cw_ref[j:j + 1, :] * cbuf[pl.ds(base - tail + j, Q), :]
    xc = _silu(acc)
    convo_ref[bi] = cbuf[pl.ds(base + Q - tail, tail), :]
    cbuf[0:SUBLANES, :] = cbuf[Q:Q + SUBLANES, :]

    xs = xc[:, :D_INNER]
    xs_b = xs.astype(bf16)
    e_mat = e_ref[...]

    dt = dt_ref[bi]
    dtT = dtT_ref[bi]
    a = dt * (-jnp.exp(alog_ref[...]))
    aT = dtT * (-jnp.exp(alogT_ref[...]))
    r_i = lax.broadcasted_iota(i32, (Q, Q), 0)
    c_i = lax.broadcasted_iota(i32, (Q, Q), 1)
    tril = (c_i <= r_i)
    tril_b = tril.astype(bf16)
    triu_b = (r_i <= c_i).astype(bf16)
    acs = sum(jnp.dot(tril_b, part, preferred_element_type=f32) for part in _split3(a))
    acsT = sum(jnp.dot(part, triu_b, preferred_element_type=f32) for part in _split3(aT))
    acs_last = acs[Q - 1:Q, :]
    e_acs = jnp.exp(acs)
    w_end = jnp.exp(acs_last - acs) * dt

    lane = lax.broadcasted_iota(i32, (Q, LANES), 1)
    first_head = lane < SSM_P
    y_cols = []
    new_h = []
    for g in range(SSM_G):
        bg = xc[:, D_INNER + g * D_STATE:D_INNER + (g + 1) * D_STATE].astype(bf16)
        cg = xc[:, D_INNER + SSM_G * D_STATE + g * D_STATE:D_INNER + SSM_G * D_STATE + (g + 1) * D_STATE].astype(bf16)
        cbm = lax.dot_general(cg, bg, (((1,), (1,)), ((), ())), preferred_element_type=f32)
        y_inter = jnp.dot(cg, hT[g].astype(bf16), preferred_element_type=f32)
        pair_cols = []
        for pr in range(SSM_GW // LANES):
            xp = xs_b[:, g * SSM_GW + pr * LANES:g * SSM_GW + (pr + 1) * LANES]
            ys = []
            for hh in range(2):
                h = g * (SSM_GW // SSM_P) + 2 * pr + hh
                seg = acs[:, h:h + 1] - acsT[h:h + 1, :]
                lm = jnp.exp(jnp.where(tril, seg, NEG))
                mm = (cbm * lm * dtT[h:h + 1, :]).astype(bf16)
                ys.append(jnp.dot(mm, xp, preferred_element_type=f32))
            pair_cols.append(jnp.where(first_head, ys[0], ys[1]))
        y_cols.append((jnp.concatenate(pair_cols, axis=1), y_inter))
        new_h.append((bg, g))

    e_acs_x = jnp.dot(e_acs.astype(bf16), e_mat, preferred_element_type=f32)
    w_end_x = jnp.dot(w_end.astype(bf16), e_mat, preferred_element_type=f32)
    xw = (xs * w_end_x).astype(bf16)
    dec8 = jnp.broadcast_to(jnp.exp(acs_last), (SUBLANES, LANES))
    dec_hi = dec8.astype(bf16)
    dec_lo = (dec8 - dec_hi.astype(f32)).astype(bf16)
    dec_x = (jnp.dot(dec_hi, e_mat, preferred_element_type=f32)
             + jnp.dot(dec_lo, e_mat, preferred_element_type=f32))[0:1, :]

    y_parts = []
    for g in range(SSM_G):
        y_intra, y_inter = y_cols[g]
        sl = slice(g * SSM_GW, (g + 1) * SSM_GW)
        yg = y_intra + y_inter * e_acs_x[:, sl] + dskip_ref[:, sl] * xs[:, sl]
        yg = yg * zs_ref[bi][:, sl].astype(f32)
        y_parts.append(_rms(yg, gssm_ref[:, sl], SSM_EPS))
        bg = new_h[g][0]
        upd = lax.dot_general(bg, xw[:, sl], (((0,), (0,)), ((), ())), preferred_element_type=f32)
        hT[g] = hT[g] * dec_x[:, sl] + upd
    y_ref[bi] = jnp.concatenate(y_parts, axis=1).astype(bf16)

    @pl.when(c == pl.num_programs(1) - 1)
    def _():
        for g in range(SSM_G):
            ssmo_ref[bi, g * SSM_GW:(g + 1) * SSM_GW, :] = hT[g].T


def _ssd(xbc, zs, dt, conv0, ssm0, conv_w, conv_b, a_log, d_skip, g_ssm, e_mat):
    B, L, _ = xbc.shape
    Q = min(SSD_CHUNK, L)
    assert L % Q == 0 and Q % SUBLANES == 0
    nc = L // Q
    dtT = jnp.swapaxes(dt[:, :, :SSM_H], 1, 2)
    alog_row = jnp.pad(a_log.reshape(1, SSM_H), ((0, 0), (0, LANES - SSM_H)))
    alog_col = a_log.reshape(SSM_H, 1)
    dskip_x = jnp.repeat(d_skip, SSM_P).reshape(1, D_INNER)
    nb = SSD_SEQS_PER_STEP if B % SSD_SEQS_PER_STEP == 0 else 1
    kern = functools.partial(_ssd_kernel, Q=Q, nb=nb)
    cmap = lambda b, c: (0, 0)
    return pl.pallas_call(
        kern,
        out_shape=(jax.ShapeDtypeStruct((B, L, D_INNER), bf16),
                   jax.ShapeDtypeStruct((B, CONV_W - 1, CONV_DIM), f32),
                   jax.ShapeDtypeStruct((B, D_INNER, D_STATE), f32)),
        grid=(B // nb, nc),
        in_specs=[pl.BlockSpec((nb, Q, CONV_DIM), lambda b, c: (b, c, 0)),
                  pl.BlockSpec((nb, Q, D_INNER), lambda b, c: (b, c, 0)),
                  pl.BlockSpec((nb, Q, LANES), lambda b, c: (b, c, 0)),
                  pl.BlockSpec((nb, SSM_H, Q), lambda b, c: (b, 0, c)),
                  pl.BlockSpec((nb, CONV_W - 1, CONV_DIM), lambda b, c: (b, 0, 0)),
                  pl.BlockSpec((nb, D_INNER, D_STATE), lambda b, c: (b, 0, 0)),
                  pl.BlockSpec((CONV_W, CONV_DIM), cmap), pl.BlockSpec((1, CONV_DIM), cmap),
                  pl.BlockSpec((1, LANES), cmap), pl.BlockSpec((SSM_H, 1), cmap),
                  pl.BlockSpec((1, D_INNER), cmap), pl.BlockSpec((1, D_INNER), cmap),
                  pl.BlockSpec((LANES, D_INNER), cmap)],
        out_specs=(pl.BlockSpec((nb, Q, D_INNER), lambda b, c: (b, c, 0)),
                   pl.BlockSpec((nb, CONV_W - 1, CONV_DIM), lambda b, c: (b, 0, 0)),
                   pl.BlockSpec((nb, D_INNER, D_STATE), lambda b, c: (b, 0, 0))),
        scratch_shapes=[pltpu.VMEM((nb, SSM_G, D_STATE, SSM_GW), f32),
                        pltpu.VMEM((nb, SUBLANES + Q, CONV_DIM), f32)],
        compiler_params=_params(("parallel", "arbitrary")),
        name="ssd_scan",
    )(xbc, zs, dt, dtT, conv0, ssm0, conv_w, conv_b.reshape(1, CONV_DIM), alog_row, alog_col,
      dskip_x, g_ssm.reshape(1, D_INNER), e_mat)


def _route(lg):
    lane = lax.broadcasted_iota(i32, lg.shape, 1)
    lanef = lane.astype(f32)
    big = float(LANES)
    gl = jnp.where(lane < N_EGROUPS, lg, NEG)
    gmax = jnp.max(gl, axis=1, keepdims=True)
    gsel = jnp.min(jnp.where(gl == gmax, lanef, big), axis=1, keepdims=True)
    p_g = 1.0 / jnp.sum(jnp.exp(gl - gmax), axis=1, keepdims=True)
    first = N_EGROUPS + E_PER_GROUP * gsel
    emask = (lanef >= first) & (lanef < first + E_PER_GROUP)
    el = jnp.where(emask, lg, NEG)
    v1 = jnp.max(el, axis=1, keepdims=True)
    i1 = jnp.min(jnp.where(emask & (el == v1), lanef, big), axis=1, keepdims=True)
    emask2 = emask & (lanef != i1)
    el2 = jnp.where(emask2, lg, NEG)
    v2 = jnp.max(el2, axis=1, keepdims=True)
    i2 = jnp.min(jnp.where(emask2 & (el2 == v2), lanef, big), axis=1, keepdims=True)
    t = jnp.exp(v2 - v1)
    w1 = p_g / (1.0 + t)
    w2 = p_g * t / (1.0 + t)
    out = jnp.where(lane == 0, i1 - N_EGROUPS, 0.0)
    out = jnp.where(lane == 1, i2 - N_EGROUPS, out)
    out = jnp.where(lane == 2, w1, out)
    out = jnp.where(lane == 3, w2, out)
    return out


def _to_row_tiles(ref, x, rows):
    for j in range(D_MODEL // LANES):
        ref[pl.ds(j, rows, stride=SUBLANES), :] = x[:, j * LANES:(j + 1) * LANES]


def _from_row_tiles(ref, rows):
    return jnp.concatenate([ref[pl.ds(j, rows, stride=SUBLANES), :] for j in range(D_MODEL // LANES)], axis=1)


def _mix_kernel(x_ref, gt1_ref, sc2_ref, sh2_ref, o_ref, ys_ref, ga_ref, gs_ref,
                woa_ref, wos_ref, wo_ref, gpost_ref, gpre_ref, wr_ref, br_ref,
                x1_ref, u2_ref, route_ref):
    tm = x_ref.shape[0]
    ya = jnp.dot(o_ref[...], woa_ref[...], preferred_element_type=f32)
    ys = jnp.dot(ys_ref[...], wos_ref[...], preferred_element_type=f32)
    m = ga_ref[...].astype(f32) * ya + gs_ref[...].astype(f32) * ys
    mo = jnp.dot(m.astype(bf16), wo_ref[...], preferred_element_type=f32)
    x1 = x_ref[...] + gt1_ref[...] * _rms(mo, gpost_ref[...], EPS)
    x1_ref[...] = x1
    u2 = _rms(x1, gpre_ref[...], EPS) * (1.0 + sc2_ref[...]) + sh2_ref[...]
    _to_row_tiles(u2_ref, u2, tm)
    lg = jnp.dot(u2.astype(bf16), wr_ref[...], preferred_element_type=f32) + br_ref[...]
    route_ref[...] = _route(lg)


def _mix(grp, x2, o, ysn, ga, gs, w_oa, w_os, w_o, g_post1, g_pre2, w_r, b_r):
    T, tm = grp.T, grp.tm
    return pl.pallas_call(
        _mix_kernel,
        out_shape=(jax.ShapeDtypeStruct((T, D_MODEL), f32),
                   jax.ShapeDtypeStruct((T * SUBLANES, LANES), f32),
                   jax.ShapeDtypeStruct((T, LANES), f32)),
        grid=(grp.n_tiles,),
        in_specs=[grp.row_spec(D_MODEL), grp.mod_spec(2), grp.mod_spec(4), grp.mod_spec(3),
                  grp.row_spec(ATTN_W), grp.row_spec(D_INNER), grp.row_spec(D_MODEL), grp.row_spec(D_MODEL),
                  _const_spec(w_oa.shape), _const_spec(w_os.shape), _const_spec(w_o.shape),
                  _const_spec((1, D_MODEL)), _const_spec((1, D_MODEL)),
                  _const_spec(w_r.shape), _const_spec((1, LANES))],
        out_specs=(grp.row_spec(D_MODEL),
                   pl.BlockSpec((tm * SUBLANES, LANES), lambda i: (i, 0)),
                   grp.row_spec(LANES)),
        compiler_params=_params(("parallel",)),
        name="mixer_out",
    )(x2, grp.mod, grp.mod, grp.mod, o, ysn, ga, gs, w_oa, w_os, w_o, g_post1, g_pre2, w_r, b_r)


def _count_kernel(route_ref, cnt_ref):
    i = pl.program_id(0)

    @pl.when(i == 0)
    def _():
        cnt_ref[...] = jnp.zeros(cnt_ref.shape, f32)

    r = route_ref[...]
    lanef = lax.broadcasted_iota(i32, r.shape, 1).astype(f32)
    hits = (lanef == r[:, 0:1]).astype(f32) + (lanef == r[:, 1:2]).astype(f32)
    cnt_ref[...] += jnp.sum(hits, axis=0, keepdims=True)


def _expert_counts(route, tm):
    T = route.shape[0]
    return pl.pallas_call(
        _count_kernel,
        out_shape=jax.ShapeDtypeStruct((1, LANES), f32),
        grid=(T // tm,),
        in_specs=[pl.BlockSpec((tm, LANES), lambda i: (i, 0))],
        out_specs=pl.BlockSpec((1, LANES), lambda i: (0, 0)),
        compiler_params=_params(("arbitrary",)),
        name="moe_count",
    )(route)


def _slot_kernel(route_ref, start_ref, dest_ref, carry):
    i = pl.program_id(0)

    @pl.when(i == 0)
    def _():
        carry[...] = jnp.broadcast_to(start_ref[...], carry.shape)

    r = route_ref[...]
    tm = r.shape[0]
    lane = lax.broadcasted_iota(i32, r.shape, 1)
    lanef = lane.astype(f32)
    oh1 = (lanef == r[:, 0:1]).astype(f32)
    oh2 = (lanef == r[:, 1:2]).astype(f32)
    cnt = oh1 + oh2
    ri = lax.broadcasted_iota(i32, (tm, tm), 0)
    ci = lax.broadcasted_iota(i32, (tm, tm), 1)
    before = (ci < ri).astype(bf16)
    pos = jnp.dot(before, cnt.astype(bf16), preferred_element_type=f32) + carry[0:1, :]
    d1 = jnp.sum(oh1 * pos, axis=1, keepdims=True)
    d2 = jnp.sum(oh2 * pos, axis=1, keepdims=True)
    dest_ref[...] = jnp.where(lane == 0, d1, jnp.where(lane == 1, d2, 0.0)).astype(i32)
    carry[...] = carry[...] + jnp.sum(cnt, axis=0, keepdims=True)


def _slots(route, starts, tm):
    T = route.shape[0]
    return pl.pallas_call(
        _slot_kernel,
        out_shape=jax.ShapeDtypeStruct((T, LANES), i32),
        grid=(T // tm,),
        in_specs=[pl.BlockSpec((tm, LANES), lambda i: (i, 0)), pl.BlockSpec((1, LANES), lambda i: (0, 0))],
        out_specs=pl.BlockSpec((tm, LANES), lambda i: (i, 0)),
        scratch_shapes=[pltpu.VMEM((SUBLANES, LANES), f32)],
        compiler_params=_params(("arbitrary",)),
        name="moe_slots",
    )(route, starts)


def _row_copy(src, src_row, dst, dst_row, sem):
    return pltpu.make_async_copy(src.at[pl.ds(pl.multiple_of(src_row * SUBLANES, SUBLANES), SUBLANES), :],
                                 dst.at[pl.ds(pl.multiple_of(dst_row * SUBLANES, SUBLANES), SUBLANES), :], sem)


def _dispatch_kernel(dest_ref, u_ref, xs_in_ref, xs_ref, sem, *, tm):
    del xs_in_ref

    def issue(t, c):
        for k in range(2):
            _row_copy(u_ref, t, xs_ref, dest_ref[0, 0, 2 * t + k], sem).start()
        return c

    lax.fori_loop(0, tm, issue, 0)
    rows = 2 * tm * SUBLANES
    pltpu.make_async_copy(xs_ref.at[pl.ds(0, rows), :], xs_ref.at[pl.ds(0, rows), :], sem).wait()


def _dispatch(dest2, u_rows, n_slots, tm):
    T = dest2.shape[0]
    xs0 = jnp.zeros((n_slots * SUBLANES, LANES), f32)
    dest3 = dest2.reshape(T // tm, 1, 2 * tm)
    return pl.pallas_call(
        functools.partial(_dispatch_kernel, tm=tm),
        out_shape=jax.ShapeDtypeStruct((n_slots * SUBLANES, LANES), f32),
        grid=(T // tm,),
        in_specs=[pl.BlockSpec((1, 1, 2 * tm), lambda i: (i, 0, 0), memory_space=pltpu.SMEM),
                  pl.BlockSpec((tm * SUBLANES, LANES), lambda i: (i, 0)), pl.BlockSpec(memory_space=pl.ANY)],
        out_specs=pl.BlockSpec(memory_space=pl.ANY),
        scratch_shapes=[pltpu.SemaphoreType.DMA(())],
        input_output_aliases={2: 0},
        compiler_params=_params(("arbitrary",)),
        name="moe_dispatch",
    )(dest3, u_rows, xs0)


def _expert_kernel(be_ref, xs_ref, wg_ref, wu_ref, wd_ref, o_ref, *, blk):
    del be_ref
    x = _from_row_tiles(xs_ref, blk).astype(bf16)
    h = _silu(jnp.dot(x, wg_ref[...], preferred_element_type=f32)) * jnp.dot(x, wu_ref[...], preferred_element_type=f32)
    _to_row_tiles(o_ref, jnp.dot(h.astype(bf16), wd_ref[...], preferred_element_type=f32), blk)


def _experts(xs, blk_e, w_gate, w_up, w_down, blk):
    n_blocks = blk_e.shape[0]
    rows = blk * SUBLANES
    grid_spec = pltpu.PrefetchScalarGridSpec(
        num_scalar_prefetch=1,
        grid=(n_blocks,),
        in_specs=[pl.BlockSpec((rows, LANES), lambda i, be: (i, 0)),
                  pl.BlockSpec((None, D_MODEL, D_FF), lambda i, be: (be[i], 0, 0)),
                  pl.BlockSpec((None, D_MODEL, D_FF), lambda i, be: (be[i], 0, 0)),
                  pl.BlockSpec((None, D_FF, D_MODEL), lambda i, be: (be[i], 0, 0))],
        out_specs=pl.BlockSpec((rows, LANES), lambda i, be: (i, 0)),
    )
    return pl.pallas_call(
        functools.partial(_expert_kernel, blk=blk),
        out_shape=jax.ShapeDtypeStruct(xs.shape, f32),
        grid_spec=grid_spec,
        compiler_params=_params(("parallel",)),
        name="moe_experts",
    )(blk_e, xs, w_gate, w_up, w_down)


def _combine_kernel(dest_ref, dest_next_ref, outs_ref, x1_ref, gt2_ref, route_ref, gpost_ref, y_ref, buf, sem, *, tm):
    i = pl.program_id(0)
    slot = jnp.bitwise_and(i, 1)

    def gather(d_ref, s):
        def issue(t, c):
            for k in range(2):
                _row_copy(outs_ref, d_ref[0, 0, 2 * t + k], buf.at[s, k], t, sem.at[s]).start()
            return c
        lax.fori_loop(0, tm, issue, 0)

    @pl.when(i == 0)
    def _():
        gather(dest_ref, 0)

    @pl.when(i + 1 < pl.num_programs(0))
    def _():
        gather(dest_next_ref, 1 - slot)

    pltpu.make_async_copy(buf.at[slot], buf.at[slot], sem.at[slot]).wait()
    r = route_ref[...]
    f = r[:, 2:3] * _from_row_tiles(buf.at[slot, 0], tm) + r[:, 3:4] * _from_row_tiles(buf.at[slot, 1], tm)
    y_ref[...] = x1_ref[...] + gt2_ref[...] * _rms(f, gpost_ref[...], EPS)


def _combine(grp, dest2, outs, x1, route, g_post2, tm):
    T = grp.T
    n = T // tm
    dest3 = dest2.reshape(n, 1, 2 * tm)
    return pl.pallas_call(
        functools.partial(_combine_kernel, tm=tm),
        out_shape=jax.ShapeDtypeStruct((T, D_MODEL), f32),
        grid=(n,),
        in_specs=[pl.BlockSpec((1, 1, 2 * tm), lambda i: (i, 0, 0), memory_space=pltpu.SMEM),
                  pl.BlockSpec((1, 1, 2 * tm), lambda i: (jnp.minimum(i + 1, n - 1), 0, 0), memory_space=pltpu.SMEM),
                  pl.BlockSpec(memory_space=pl.ANY),
                  grp.row_spec(D_MODEL), grp.mod_spec(5), grp.row_spec(LANES), _const_spec((1, D_MODEL))],
        out_specs=grp.row_spec(D_MODEL),
        scratch_shapes=[pltpu.VMEM((2, 2, tm * SUBLANES, LANES), f32), pltpu.SemaphoreType.DMA((2,))],
        compiler_params=_params(("arbitrary",)),
        name="moe_combine",
    )(dest3, dest3, outs, x1, grp.mod, route, g_post2)


def _hmoe_and_residual(grp, x1, u_rows, route, w_gate, w_up, w_down, g_post2):
    T = grp.T
    tm = grp.tm
    counts = _expert_counts(route, tm)[0, :N_EXPERTS].astype(i32)
    padded = (counts + MOE_BLOCK - 1) // MOE_BLOCK * MOE_BLOCK
    ends = jnp.cumsum(padded)
    starts = ends - padded
    n_blocks = -(-2 * T // MOE_BLOCK) + N_EXPERTS
    n_slots = n_blocks * MOE_BLOCK
    blk_start = jnp.arange(n_blocks, dtype=i32) * MOE_BLOCK
    blk_e = jnp.minimum(jnp.sum(blk_start[:, None] >= ends[None, :], axis=1), N_EXPERTS - 1).astype(i32)
    starts_row = jnp.pad(starts.astype(f32), (0, LANES - N_EXPERTS)).reshape(1, LANES)
    dest2 = _slots(route, starts_row, tm)[:, :2]
    xs = _dispatch(dest2, u_rows, n_slots, tm)
    outs = _experts(xs, blk_e, w_gate, w_up, w_down, MOE_BLOCK)
    return _combine(grp, dest2, outs, x1, route, g_post2, tm)


def _rope_tables(pos):
    half = HEAD_DIM // 2
    inv = ROPE_THETA ** (-2.0 * jnp.arange(half, dtype=f32) / HEAD_DIM)
    ang = pos.astype(f32)[:, None] * inv[None, :]
    cos, sin = jnp.cos(ang), jnp.sin(ang)
    return jnp.tile(cos, (1, LANES // half)), jnp.tile(jnp.concatenate([-sin, sin], axis=1), (1, LANES // HEAD_DIM))


def _layer(grp, x2, lw, lam_init, attend, conv0, ssm0):
    n_seq, seq_len = grp.n_seq, grp.seq_len
    q, k, kb, v, vb = _proj_qkv(grp, x2, lw["g_pre1"], lw["w_qkv"], 1, 0)
    zs, xbc = _proj_zx(grp, x2, lw["g_pre1"], lw["w_z"], lw["w_xbc"], 1, 0)
    ga, gs, dt = _proj_gates(grp, x2, lw["g_pre1"], lw["w_gates"], lw["w_dt"], lw["dt_bias"], 1, 0)
    shp = lambda a: a.reshape(n_seq, seq_len, a.shape[-1])
    o = attend(shp(q), shp(kb), shp(vb))
    ysn, conv_new, ssm_new = _ssd(shp(xbc), shp(zs), shp(dt), conv0, ssm0.reshape(n_seq, D_INNER, D_STATE),
                                  lw["conv_w"], lw["conv_b"], lw["a_log"], lw["d_skip"], lw["g_ssm"], lw["e_mat"])
    x1, u_rows, route = _mix(grp, x2, o.reshape(grp.T, ATTN_W), ysn.reshape(grp.T, D_INNER), ga, gs,
                             lw["w_oa"], lw["w_os"], lw["w_o"], lw["g_post1"], lw["g_pre2"], lw["w_r"], lw["b_r"])
    y2 = _hmoe_and_residual(grp, x1, u_rows, route, lw["w_gate"], lw["w_up"], lw["w_down"], lw["g_post2"])
    return (y2, k.reshape(n_seq, seq_len, N_KV, HEAD_W), v.reshape(n_seq, seq_len, N_KV, HEAD_W), conv_new,
            ssm_new.reshape(n_seq, SSM_H, SSM_P, D_STATE))


def kernel(x_prompt, x_sample, cache_k, cache_v, state_conv, state_ssm, page_table, c_prompt, c_sample, w_ada, b_ada, g_pre1, g_post1, g_pre2, g_post2, w_in, lambda_q1, lambda_k1, lambda_q2, lambda_k2, g_subln, w_oa, conv_w, conv_b, dt_bias, a_log, d_skip, g_ssm, w_os, w_o, w_rg, b_rg, w_re, b_re, w_gate, w_up, w_down):
    depth = w_in.shape[0]
    bp, lp_len, _ = x_prompt.shape
    bs, ls_len, _ = x_sample.shape
    past = page_table.shape[1] * cache_k.shape[2]
    ck = cache_k.reshape(cache_k.shape[0], cache_k.shape[1], cache_k.shape[2] * N_KV, HEAD_W)
    cv = cache_v.reshape(cache_v.shape[0], cache_v.shape[1], cache_v.shape[2] * N_KV, HEAD_W)
    rope_p = _rope_tables(jnp.arange(lp_len, dtype=i32))
    rope_s8 = _rope_tables(past + jnp.arange(ls_len, dtype=i32))
    rope_s = tuple(jnp.tile(t, (bs, 1)) for t in rope_s8)
    e_mat = (jnp.arange(LANES, dtype=i32)[:, None] == jnp.arange(D_INNER, dtype=i32)[None, :] // SSM_P).astype(bf16)

    hp = x_prompt.reshape(bp * lp_len, D_MODEL)
    hs = x_sample.reshape(bs * ls_len, D_MODEL)
    c_all = jnp.concatenate([c_prompt, c_sample], axis=0)
    outs = [[] for _ in range(8)]
    for l in range(depth):
        lam_init = 0.8 - 0.6 * math.exp(-0.3 * l)
        w_in_l = w_in[l].astype(bf16)
        lw = dict(
            w_qkv=w_in_l[:, :OFF_Z], w_z=w_in_l[:, OFF_Z:OFF_XBC], w_xbc=w_in_l[:, OFF_XBC:OFF_DT],
            w_dt=jnp.pad(w_in_l[:, OFF_DT:OFF_GA], ((0, 0), (0, LANES - SSM_H))), w_gates=w_in_l[:, OFF_GA:],
            dt_bias=jnp.pad(dt_bias[l].reshape(1, SSM_H), ((0, 0), (0, LANES - SSM_H))),
            g_pre1=g_pre1[l].reshape(1, D_MODEL), g_post1=g_post1[l].reshape(1, D_MODEL),
            g_pre2=g_pre2[l].reshape(1, D_MODEL), g_post2=g_post2[l].reshape(1, D_MODEL),
            w_oa=w_oa[l].astype(bf16), w_os=w_os[l].astype(bf16), w_o=w_o[l].astype(bf16),
            conv_w=conv_w[l], conv_b=conv_b[l], a_log=a_log[l], d_skip=d_skip[l], g_ssm=g_ssm[l], e_mat=e_mat,
            w_r=jnp.pad(jnp.concatenate([w_rg[l], w_re[l]], axis=1),
                        ((0, 0), (0, LANES - N_EGROUPS - N_EXPERTS))).astype(bf16),
            b_r=jnp.pad(jnp.concatenate([b_rg[l], b_re[l]]), (0, LANES - N_EGROUPS - N_EXPERTS)).reshape(1, LANES),
            w_gate=w_gate[l].astype(bf16), w_up=w_up[l].astype(bf16), w_down=w_down[l].astype(bf16),
        )
        lam_p = jnp.stack([lambda_q1[l], lambda_k1[l], lambda_q2[l], lambda_k2[l]])
        gsub = g_subln[l].reshape(1, HEAD_W)
        mod = _ada(c_all, w_ada[l].astype(bf16), b_ada[l])
        grp_p = _Group(bp, lp_len, mod[:bp], rope_p, per_token=False)
        grp_s = _Group(bs, ls_len, jnp.repeat(mod[bp:], ls_len, axis=0), rope_s, per_token=True)

        attend_p = lambda q, kb, vb: _attn_prompt(q, kb, vb, lam_p, gsub, lam_init)
        attend_s = lambda q, kb, vb: _attn_sample(q, kb, vb, ck, cv, page_table, l, lam_p, gsub, lam_init)
        conv0 = jnp.zeros((bp, CONV_W - 1, CONV_DIM), f32)
        ssm0 = jnp.zeros((bp, SSM_H, SSM_P, D_STATE), f32)
        hp, kp, vp, cp, sp = _layer(grp_p, hp, lw, lam_init, attend_p, conv0, ssm0)
        hs, ksm, vsm, csm, ssm = _layer(grp_s, hs, lw, lam_init, attend_s, state_conv[l], state_ssm[l])
        for lst, val in zip(outs, (kp, vp, cp, sp, ksm, vsm, csm, ssm)):
            lst.append(val)
    stacked = [jnp.stack(o) for o in outs]
    return (hp.reshape(bp, lp_len, D_MODEL), hs.reshape(bs, ls_len, D_MODEL), *stacked)
```

```python
import functools
import math

import jax
import jax.numpy as jnp
from jax import lax
from jax.experimental import pallas as pl
from jax.experimental.pallas import tpu as pltpu

f32 = jnp.float32
bf16 = jnp.bfloat16
i32 = jnp.int32

D_MODEL = 1024
N_HEADS = 8
N_KV = 4
HEAD_DIM = 64
HEAD_W = 2 * HEAD_DIM
ATTN_W = N_HEADS * HEAD_W
KV_W = N_KV * HEAD_W
ATTN_SCALE = HEAD_DIM ** -0.5
ROPE_THETA = 10000.0
SUBLN_EPS = 1e-5
D_INNER = 2048
SSM_P = 64
SSM_H = D_INNER // SSM_P
SSM_G = 4
SSM_GW = D_INNER // SSM_G
D_STATE = 128
CONV_W = 4
CONV_DIM = D_INNER + 2 * SSM_G * D_STATE
SSD_CHUNK = 128
SSM_EPS = 1e-5
N_EGROUPS = 4
E_PER_GROUP = 8
N_EXPERTS = 32
D_FF = 512
EPS = 1e-6
OFF_K = ATTN_W
OFF_V = OFF_K + KV_W
OFF_Z = OFF_V + KV_W
OFF_XBC = OFF_Z + D_INNER
OFF_DT = OFF_XBC + CONV_DIM
OFF_GA = OFF_DT + SSM_H
OFF_GS = OFF_GA + D_MODEL

LANES = 128
SUBLANES = 8
VMEM_LIMIT = 56 * 1024 * 1024
NEG = -1e30
LOG2E = math.log2(math.e)

TOKEN_TILE = 512
ATTN_TILE = 512
PAGES_PER_STEP = 8
MOE_BLOCK = 512
SSD_SEQS_PER_STEP = 1
MIX_ROW_PARTS = 1


def _params(sem, vmem=VMEM_LIMIT):
    return pltpu.CompilerParams(dimension_semantics=sem, vmem_limit_bytes=vmem)


def _sigmoid(x):
    return 1.0 / (1.0 + jnp.exp(-x))


def _silu(x):
    return x * _sigmoid(x)


def _rms(x, g, eps):
    return x * lax.rsqrt(jnp.mean(x * x, axis=-1, keepdims=True) + eps) * g


def _bdot(a, b):
    return jnp.dot(a.astype(bf16), b.astype(bf16), preferred_element_type=f32)


class _Group:
    def __init__(self, n_seq, seq_len, mod_rows, rope_rows, per_token):
        self.n_seq, self.seq_len = n_seq, seq_len
        self.T = n_seq * seq_len
        self.tm = min(TOKEN_TILE, self.T if per_token else seq_len)
        assert self.T % self.tm == 0
        self.per_token = per_token
        if per_token:
            self.mod = mod_rows
            self.rope = rope_rows
        else:
            assert seq_len % self.tm == 0
            self.tiles_per_seq = seq_len // self.tm
            self.mod = mod_rows.reshape(n_seq, 1, 6 * D_MODEL)
            self.rope = rope_rows
        self.n_tiles = self.T // self.tm

    def mod_spec(self, col):
        if self.per_token:
            return pl.BlockSpec((self.tm, D_MODEL), lambda i: (i, col))
        tps = self.tiles_per_seq
        return pl.BlockSpec((None, 1, D_MODEL), lambda i: (i // tps, 0, col))

    def rope_spec(self):
        if self.per_token:
            return pl.BlockSpec((self.tm, LANES), lambda i: (i, 0))
        tps = self.tiles_per_seq
        return pl.BlockSpec((self.tm, LANES), lambda i: (i % tps, 0))

    def row_spec(self, width):
        return pl.BlockSpec((self.tm, width), lambda i: (i, 0))


def _const_spec(shape):
    nd = len(shape)
    return pl.BlockSpec(shape, lambda i: (0,) * nd)


def _ada_kernel(c_ref, w_ref, b_ref, o_ref):
    o_ref[...] = _bdot(_silu(c_ref[...]), w_ref[...]) + b_ref[...]


def _ada(c, w_ada, b_ada):
    m = c.shape[0]
    n = w_ada.shape[1]
    tn = D_MODEL
    return pl.pallas_call(
        _ada_kernel,
        out_shape=jax.ShapeDtypeStruct((m, n), f32),
        grid=(n // tn,),
        in_specs=[pl.BlockSpec((m, D_MODEL), lambda j: (0, 0)),
                  pl.BlockSpec((D_MODEL, tn), lambda j: (0, j)),
                  pl.BlockSpec((1, tn), lambda j: (0, j))],
        out_specs=pl.BlockSpec((m, tn), lambda j: (0, j)),
        compiler_params=_params(("parallel",)),
        name="ada_mod",
    )(c, w_ada, b_ada.reshape(1, n))


def _modulated(x_ref, g_ref, sc_ref, sh_ref):
    u = _rms(x_ref[...], g_ref[...], EPS) * (1.0 + sc_ref[...]) + sh_ref[...]
    return u.astype(bf16)


def _rope_cols(x, n_blocks, cos, sin_signed, lo):
    out = []
    for j in range(n_blocks):
        xb = x[:, j * LANES:(j + 1) * LANES]
        partner = jnp.where(lo, pltpu.roll(xb, LANES - HEAD_DIM // 2, 1), pltpu.roll(xb, HEAD_DIM // 2, 1))
        out.append(xb * cos + partner * sin_signed)
    return jnp.concatenate(out, axis=1)


def _qkv_kernel(x_ref, g_ref, sc_ref, sh_ref, w_ref, cos_ref, sin_ref,
                q_ref, k_ref, kb_ref, v_ref, vb_ref):
    u = _modulated(x_ref, g_ref, sc_ref, sh_ref)
    proj = jnp.dot(u, w_ref[...], preferred_element_type=f32)
    cos, sin_signed = cos_ref[...], sin_ref[...]
    lane = lax.broadcasted_iota(i32, cos.shape, 1)
    lo = jnp.bitwise_and(lane, HEAD_DIM - 1) < HEAD_DIM // 2
    q = _rope_cols(proj[:, :OFF_K], N_HEADS, cos, sin_signed, lo)
    k = _rope_cols(proj[:, OFF_K:OFF_V], N_KV, cos, sin_signed, lo)
    v = proj[:, OFF_V:OFF_Z]
    q_ref[...] = (q * (ATTN_SCALE * LOG2E)).astype(bf16)
    kb_ref[...] = k.astype(bf16)
    vb_ref[...] = v.astype(bf16)
    tm = k.shape[0]
    for h in range(N_KV):
        k_ref[pl.ds(h, tm, stride=N_KV), :] = k[:, h * HEAD_W:(h + 1) * HEAD_W]
        v_ref[pl.ds(h, tm, stride=N_KV), :] = v[:, h * HEAD_W:(h + 1) * HEAD_W]


def _proj_qkv(grp, x2, g, w_qkv, col_scale, col_shift):
    T, tm = grp.T, grp.tm
    head_rows = pl.BlockSpec((tm * N_KV, HEAD_W), lambda i: (i, 0))
    return pl.pallas_call(
        _qkv_kernel,
        out_shape=(jax.ShapeDtypeStruct((T, ATTN_W), bf16),
                   jax.ShapeDtypeStruct((T * N_KV, HEAD_W), f32), jax.ShapeDtypeStruct((T, KV_W), bf16),
                   jax.ShapeDtypeStruct((T * N_KV, HEAD_W), f32), jax.ShapeDtypeStruct((T, KV_W), bf16)),
        grid=(grp.n_tiles,),
        in_specs=[grp.row_spec(D_MODEL), _const_spec((1, D_MODEL)),
                  grp.mod_spec(col_scale), grp.mod_spec(col_shift),
                  _const_spec(w_qkv.shape), grp.rope_spec(), grp.rope_spec()],
        out_specs=(grp.row_spec(ATTN_W), head_rows, grp.row_spec(KV_W),
                   head_rows, grp.row_spec(KV_W)),
        compiler_params=_params(("parallel",)),
        name="proj_qkv",
    )(x2, g, grp.mod, grp.mod, w_qkv, grp.rope[0], grp.rope[1])


def _zx_kernel(x_ref, g_ref, sc_ref, sh_ref, wz_ref, wx_ref, zs_ref, xbc_ref):
    u = _modulated(x_ref, g_ref, sc_ref, sh_ref)
    zs_ref[...] = _silu(jnp.dot(u, wz_ref[...], preferred_element_type=f32)).astype(bf16)
    xbc_ref[...] = jnp.dot(u, wx_ref[...], preferred_element_type=f32).astype(bf16)


def _proj_zx(grp, x2, g, w_z, w_xbc, col_scale, col_shift):
    T = grp.T
    return pl.pallas_call(
        _zx_kernel,
        out_shape=(jax.ShapeDtypeStruct((T, D_INNER), bf16), jax.ShapeDtypeStruct((T, CONV_DIM), bf16)),
        grid=(grp.n_tiles,),
        in_specs=[grp.row_spec(D_MODEL), _const_spec((1, D_MODEL)),
                  grp.mod_spec(col_scale), grp.mod_spec(col_shift),
                  _const_spec(w_z.shape), _const_spec(w_xbc.shape)],
        out_specs=(grp.row_spec(D_INNER), grp.row_spec(CONV_DIM)),
        compiler_params=_params(("parallel",)),
        name="proj_zx",
    )(x2, g, grp.mod, grp.mod, w_z, w_xbc)


def _gates_kernel(x_ref, g_ref, sc_ref, sh_ref, wg_ref, wdt_ref, dtb_ref, ga_ref, gs_ref, dt_ref):
    u = _modulated(x_ref, g_ref, sc_ref, sh_ref)
    gates = _sigmoid(jnp.dot(u, wg_ref[...], preferred_element_type=f32))
    ga_ref[...] = gates[:, :D_MODEL].astype(bf16)
    gs_ref[...] = gates[:, D_MODEL:].astype(bf16)
    raw = jnp.dot(u, wdt_ref[...], preferred_element_type=f32) + dtb_ref[...]
    dt_ref[...] = jnp.maximum(raw, 0.0) + jnp.log1p(jnp.exp(-jnp.abs(raw)))


def _proj_gates(grp, x2, g, w_gates, w_dt, dt_bias, col_scale, col_shift):
    T = grp.T
    return pl.pallas_call(
        _gates_kernel,
        out_shape=(jax.ShapeDtypeStruct((T, D_MODEL), bf16), jax.ShapeDtypeStruct((T, D_MODEL), bf16),
                   jax.ShapeDtypeStruct((T, LANES), f32)),
        grid=(grp.n_tiles,),
        in_specs=[grp.row_spec(D_MODEL), _const_spec((1, D_MODEL)),
                  grp.mod_spec(col_scale), grp.mod_spec(col_shift),
                  _const_spec(w_gates.shape), _const_spec(w_dt.shape), _const_spec((1, LANES))],
        out_specs=(grp.row_spec(D_MODEL), grp.row_spec(D_MODEL), grp.row_spec(LANES)),
        compiler_params=_params(("parallel",)),
        name="proj_gates",
    )(x2, g, grp.mod, grp.mod, w_gates, w_dt, dt_bias)


def _lambda_full(lam_ref, lam_init):
    lp = lam_ref[...]
    l1 = jnp.sum(lp[0:1] * lp[1:2], axis=1, keepdims=True)
    l2 = jnp.sum(lp[2:3] * lp[3:4], axis=1, keepdims=True)
    return jnp.exp(l1) - jnp.exp(l2) + lam_init


def _with_ones(v):
    return jnp.concatenate([v, jnp.ones(v.shape, bf16)], axis=1)


def _softmax_step(s, v1, m_ref, acc_ref):
    reps = s.shape[1] // LANES if s.shape[1] % LANES == 0 else 0
    m_prev = m_ref[...]
    m_new = jnp.maximum(m_prev, jnp.max(s, axis=1, keepdims=True))
    alpha = jnp.exp2(m_prev - m_new)
    if reps:
        p = jnp.exp2(s - jnp.concatenate([m_new] * reps, axis=1))
    else:
        p = jnp.exp2(s - m_new[:, 0:1])
    acc_ref[...] = (jnp.concatenate([alpha, alpha], axis=1) * acc_ref[...]
                    + jnp.dot(p.astype(bf16), v1, preferred_element_type=f32))
    m_ref[...] = m_new


def _diff_heads(acc, lam, gsub, lam_init, rows):
    outs = []
    for r in range(2):
        a0 = acc[(2 * r) * rows:(2 * r + 1) * rows]
        a1 = acc[(2 * r + 1) * rows:(2 * r + 2) * rows]
        o = a0[:, :HEAD_W] / a0[:, HEAD_W:] - lam * (a1[:, :HEAD_W] / a1[:, HEAD_W:])
        outs.append(_rms(o, gsub, SUBLN_EPS) * (1.0 - lam_init))
    return jnp.concatenate(outs, axis=1)


def _pad_components(q, rows):
    lane = lax.broadcasted_iota(i32, (rows, LANES), 1)
    lo = lane < HEAD_DIM
    blocks = []
    for r in range(2):
        qh = q[:, r * LANES:(r + 1) * LANES].astype(f32)
        blocks.append(jnp.where(lo, qh, 0.0))
        blocks.append(jnp.where(lo, 0.0, qh))
    return jnp.concatenate(blocks, axis=0)


def _stacked_causal(n, copies=4):
    assert n & (n - 1) == 0
    row = jnp.bitwise_and(lax.broadcasted_iota(i32, (copies * n, n), 0), n - 1)
    col = lax.broadcasted_iota(i32, (copies * n, n), 1)
    return col <= row


def _attn_prompt_kernel(lam_ref, gsub_ref, q_ref, k_ref, v_ref, o_ref, qpad, m_s, acc_s, s_a, s_b, *, tq, lam_init):
    i = pl.program_id(2)
    qpad[...] = _pad_components(q_ref[0], tq).astype(bf16)
    m_s[...] = jnp.full(m_s.shape, NEG, f32)
    acc_s[...] = jnp.zeros(acc_s.shape, f32)

    def scores(c):
        kc = k_ref[0, pl.ds(pl.multiple_of(c * tq, tq), tq), :]
        return lax.dot_general(qpad[...], kc, (((1,), (1,)), ((), ())), preferred_element_type=f32)

    def consume(s_ref, c):
        _softmax_step(s_ref[...], _with_ones(v_ref[0, pl.ds(pl.multiple_of(c * tq, tq), tq), :]), m_s, acc_s)

    s_a[...] = jnp.where(_stacked_causal(tq), scores(i), NEG)

    def pair(p):
        s_b[...] = scores(2 * p)
        consume(s_a, jnp.where(p == 0, i, 2 * p - 1))
        s_a[...] = scores(jnp.minimum(2 * p + 1, i - 1))
        consume(s_b, 2 * p)

    def two_pairs(t, carry):
        pair(2 * t)
        pair(2 * t + 1)
        return carry

    n_pairs = jnp.right_shift(i + 1, 1)
    lax.fori_loop(0, jnp.right_shift(n_pairs, 1), two_pairs, 0)

    @pl.when(jnp.bitwise_and(n_pairs, 1) == 1)
    def _():
        pair(n_pairs - 1)

    @pl.when(jnp.bitwise_and(i, 1) == 0)
    def _():
        consume(s_a, jnp.maximum(i - 1, 0))

    lam = _lambda_full(lam_ref, lam_init)
    o_ref[0] = _diff_heads(acc_s[...], lam, gsub_ref[...], lam_init, tq).astype(bf16)


def _attn_prompt(q, kb, vb, lam_p, gsub, lam_init):
    B, L, _ = q.shape
    tq = min(ATTN_TILE, L)
    assert L % tq == 0
    kern = functools.partial(_attn_prompt_kernel, tq=tq, lam_init=lam_init)
    return pl.pallas_call(
        kern,
        out_shape=jax.ShapeDtypeStruct((B, L, ATTN_W), bf16),
        grid=(B, N_KV, L // tq),
        in_specs=[pl.BlockSpec((4, HEAD_DIM), lambda b, g, i: (0, 0)),
                  pl.BlockSpec((1, HEAD_W), lambda b, g, i: (0, 0)),
                  pl.BlockSpec((1, tq, 2 * HEAD_W), lambda b, g, i: (b, i, g)),
                  pl.BlockSpec((1, L, HEAD_W), lambda b, g, i: (b, 0, g)),
                  pl.BlockSpec((1, L, HEAD_W), lambda b, g, i: (b, 0, g))],
        out_specs=pl.BlockSpec((1, tq, 2 * HEAD_W), lambda b, g, i: (b, i, g)),
        scratch_shapes=[pltpu.VMEM((4 * tq, LANES), bf16), pltpu.VMEM((4 * tq, LANES), f32),
                        pltpu.VMEM((4 * tq, 2 * LANES), f32),
                        pltpu.VMEM((4 * tq, tq), f32), pltpu.VMEM((4 * tq, tq), f32)],
        compiler_params=_params(("parallel", "parallel", "arbitrary")),
        name="attn_prompt",
    )(lam_p, gsub, q, kb, vb)


def _attn_sample_kernel(pt_ref, lam_ref, gsub_ref, q_ref, kn_ref, vn_ref, *rest, n_pages_step, seq_new, lam_init):
    del pt_ref
    kp = rest[:n_pages_step]
    vp = rest[n_pages_step:2 * n_pages_step]
    o_ref, qpad, m_s, acc_s = rest[2 * n_pages_step:]
    j = pl.program_id(1)

    @pl.when(j == 0)
    def _():
        q = q_ref[0]
        for g in range(N_KV):
            qpad[g] = _pad_components(q[:, g * 2 * HEAD_W:(g + 1) * 2 * HEAD_W], seq_new)
        m_s[...] = jnp.full(m_s.shape, NEG, f32)
        acc_s[...] = jnp.zeros(acc_s.shape, f32)

    page = kp[0].shape[0] // N_KV
    for g in range(N_KV):
        qg = qpad[g].astype(bf16)
        kg = jnp.concatenate([r[pl.ds(g, page, stride=N_KV), :].astype(bf16) for r in kp], axis=0)
        vg = jnp.concatenate([r[pl.ds(g, page, stride=N_KV), :].astype(bf16) for r in vp], axis=0)
        s = lax.dot_general(qg, kg, (((1,), (1,)), ((), ())), preferred_element_type=f32)
        _softmax_step(s, _with_ones(vg), m_s.at[g], acc_s.at[g])

    @pl.when(j == pl.num_programs(1) - 1)
    def _():
        lam = _lambda_full(lam_ref, lam_init)
        causal = _stacked_causal(seq_new)
        outs = []
        for g in range(N_KV):
            qg = qpad[g].astype(bf16)
            kg = kn_ref[0][:, g * HEAD_W:(g + 1) * HEAD_W]
            vg = vn_ref[0][:, g * HEAD_W:(g + 1) * HEAD_W]
            s = lax.dot_general(qg, kg, (((1,), (1,)), ((), ())), preferred_element_type=f32)
            _softmax_step(jnp.where(causal, s, NEG), _with_ones(vg), m_s.at[g], acc_s.at[g])
            outs.append(_diff_heads(acc_s[g], lam, gsub_ref[...], lam_init, seq_new))
        o_ref[0] = jnp.concatenate(outs, axis=1).astype(bf16)


def _attn_sample(q, kb, vb, cache_k, cache_v, page_table, layer, lam_p, gsub, lam_init):
    S, Ls, _ = q.shape
    n_pages = page_table.shape[1]
    page_rows = cache_k.shape[2]
    pps = PAGES_PER_STEP
    while n_pages % pps:
        pps -= 1
    steps = n_pages // pps

    def page_spec(p):
        return pl.BlockSpec((None, None, page_rows, HEAD_W), lambda s, j, pt: (layer, pt[s, j * pps + p], 0, 0))

    kern = functools.partial(_attn_sample_kernel, n_pages_step=pps, seq_new=Ls, lam_init=lam_init)
    grid_spec = pltpu.PrefetchScalarGridSpec(
        num_scalar_prefetch=1,
        grid=(S, steps),
        in_specs=[pl.BlockSpec((4, HEAD_DIM), lambda s, j, pt: (0, 0)),
                  pl.BlockSpec((1, HEAD_W), lambda s, j, pt: (0, 0)),
                  pl.BlockSpec((1, Ls, ATTN_W), lambda s, j, pt: (s, 0, 0)),
                  pl.BlockSpec((1, Ls, KV_W), lambda s, j, pt: (s, 0, 0)),
                  pl.BlockSpec((1, Ls, KV_W), lambda s, j, pt: (s, 0, 0))]
        + [page_spec(p) for p in range(pps)] + [page_spec(p) for p in range(pps)],
        out_specs=pl.BlockSpec((1, Ls, ATTN_W), lambda s, j, pt: (s, 0, 0)),
        scratch_shapes=[pltpu.VMEM((N_KV, 4 * Ls, LANES), f32), pltpu.VMEM((N_KV, 4 * Ls, LANES), f32),
                        pltpu.VMEM((N_KV, 4 * Ls, 2 * LANES), f32)],
    )
    return pl.pallas_call(
        kern,
        out_shape=jax.ShapeDtypeStruct((S, Ls, ATTN_W), bf16),
        grid_spec=grid_spec,
        compiler_params=_params(("parallel", "arbitrary")),
        name="attn_sample",
    )(page_table, lam_p, gsub, q, kb, vb, *([cache_k] * pps), *([cache_v] * pps))


def _split3(x):
    hi = x.astype(bf16)
    r1 = x - hi.astype(f32)
    mid = r1.astype(bf16)
    lo = (r1 - mid.astype(f32)).astype(bf16)
    return hi, mid, lo


def _ssd_kernel(*refs, Q, nb):
    for bi in range(nb):
        _ssd_row(bi, *refs, Q=Q)


def _ssd_row(bi, xbc_ref, zs_ref, dt_ref, dtT_ref, conv0_ref, ssm0_ref, cw_ref, cb_ref,
             alog_ref, alogT_ref, dskip_ref, gssm_ref, e_ref,
             y_ref, convo_ref, ssmo_ref, hT_all, cbuf_all, *, Q):
    c = pl.program_id(1)
    hT, cbuf = hT_all.at[bi], cbuf_all.at[bi]
    tail = CONV_W - 1
    base = SUBLANES

    @pl.when(c == 0)
    def _():
        cbuf[base - tail:base, :] = conv0_ref[bi]
        for g in range(SSM_G):
            hT[g] = ssm0_ref[bi, g * SSM_GW:(g + 1) * SSM_GW, :].T

    cbuf[base:base + Q, :] = xbc_ref[bi].astype(f32)
    acc = cb_ref[...] + cw_ref[CONV_W - 1:CONV_W, :] * cbuf[base:base + Q, :]
    for j in range(tail):
        acc = acc + cw_ref[j:j + 1, :] * cbuf[pl.ds(base - tail + j, Q), :]
    xc = _silu(acc)
    convo_ref[bi] = cbuf[pl.ds(base + Q - tail, tail), :]
    cbuf[0:SUBLANES, :] = cbuf[Q:Q + SUBLANES, :]

    xs = xc[:, :D_INNER]
    xs_b = xs.astype(bf16)
    e_mat = e_ref[...]

    dt = dt_ref[bi]
    dtT = dtT_ref[bi]
    a = dt * (-jnp.exp(alog_ref[...]))
    aT = dtT * (-jnp.exp(alogT_ref[...]))
    r_i = lax.broadcasted_iota(i32, (Q, Q), 0)
    c_i = lax.broadcasted_iota(i32, (Q, Q), 1)
    tril = (c_i <= r_i)
    tril_b = tril.astype(bf16)
    triu_b = (r_i <= c_i).astype(bf16)
    acs = sum(jnp.dot(tril_b, part, preferred_element_type=f32) for part in _split3(a))
    acsT = sum(jnp.dot(part, triu_b, preferred_element_type=f32) for part in _split3(aT))
    acs_last = acs[Q - 1:Q, :]
    e_acs = jnp.exp(acs)
    w_end = jnp.exp(acs_last - acs) * dt

    lane = lax.broadcasted_iota(i32, (Q, LANES), 1)
    first_head = lane < SSM_P
    y_cols = []
    new_h = []
    for g in range(SSM_G):
        bg = xc[:, D_INNER + g * D_STATE:D_INNER + (g + 1) * D_STATE].astype(bf16)
        cg = xc[:, D_INNER + SSM_G * D_STATE + g * D_STATE:D_INNER + SSM_G * D_STATE + (g + 1) * D_STATE].astype(bf16)
        cbm = lax.dot_general(cg, bg, (((1,), (1,)), ((), ())), preferred_element_type=f32)
        y_inter = jnp.dot(cg, hT[g].astype(bf16), preferred_element_type=f32)
        pair_cols = []
        for pr in range(SSM_GW // LANES):
            xp = xs_b[:, g * SSM_GW + pr * LANES:g * SSM_GW + (pr + 1) * LANES]
            ys = []
            for hh in range(2):
                h = g * (SSM_GW // SSM_P) + 2 * pr + hh
                seg = acs[:, h:h + 1] - acsT[h:h + 1, :]
                lm = jnp.exp(jnp.where(tril, seg, NEG))
                mm = (cbm * lm * dtT[h:h + 1, :]).astype(bf16)
                ys.append(jnp.dot(mm, xp, preferred_element_type=f32))
            pair_cols.append(jnp.where(first_head, ys[0], ys[1]))
        y_cols.append((jnp.concatenate(pair_cols, axis=1), y_inter))
        new_h.append((bg, g))

    e_acs_x = jnp.dot(e_acs.astype(bf16), e_mat, preferred_element_type=f32)
    w_end_x = jnp.dot(w_end.astype(bf16), e_mat, preferred_element_type=f32)
    xw = (xs * w_end_x).astype(bf16)
    dec8 = jnp.broadcast_to(jnp.exp(acs_last), (SUBLANES, LANES))
    dec_hi = dec8.astype(bf16)
    dec_lo = (dec8 - dec_hi.astype(f32)).astype(bf16)
    dec_x = (jnp.dot(dec_hi, e_mat, preferred_element_type=f32)
             + jnp.dot(dec_lo, e_mat, preferred_element_type=f32))[0:1, :]

    y_parts = []
    for g in range(SSM_G):
        y_intra, y_inter = y_cols[g]
        sl = slice(g * SSM_GW, (g + 1) * SSM_GW)
        yg = y_intra + y_inter * e_acs_x[:, sl] + dskip_ref[:, sl] * xs[:, sl]
        yg = yg * zs_ref[bi][:, sl].astype(f32)
        y_parts.append(_rms(yg, gssm_ref[:, sl], SSM_EPS))
        bg = new_h[g][0]
        upd = lax.dot_general(bg, xw[:, sl], (((0,), (0,)), ((), ())), preferred_element_type=f32)
        hT[g] = hT[g] * dec_x[:, sl] + upd
    y_ref[bi] = jnp.concatenate(y_parts, axis=1).astype(bf16)

    @pl.when(c == pl.num_programs(1) - 1)
    def _():
        for g in range(SSM_G):
            ssmo_ref[bi, g * SSM_GW:(g + 1) * SSM_GW, :] = hT[g].T


def _ssd(xbc, zs, dt, conv0, ssm0, conv_w, conv_b, a_log, d_skip, g_ssm, e_mat):
    B, L, _ = xbc.shape
    Q = min(SSD_CHUNK, L)
    assert L % Q == 0 and Q % SUBLANES == 0
    nc = L // Q
    dtT = jnp.swapaxes(dt[:, :, :SSM_H], 1, 2)
    alog_row = jnp.pad(a_log.reshape(1, SSM_H), ((0, 0), (0, LANES - SSM_H)))
    alog_col = a_log.reshape(SSM_H, 1)
    dskip_x = jnp.repeat(d_skip, SSM_P).reshape(1, D_INNER)
    nb = SSD_SEQS_PER_STEP if B % SSD_SEQS_PER_STEP == 0 else 1
    kern = functools.partial(_ssd_kernel, Q=Q, nb=nb)
    cmap = lambda b, c: (0, 0)
    return pl.pallas_call(
        kern,
        out_shape=(jax.ShapeDtypeStruct((B, L, D_INNER), bf16),
                   jax.ShapeDtypeStruct((B, CONV_W - 1, CONV_DIM), f32),
                   jax.ShapeDtypeStruct((B, D_INNER, D_STATE), f32)),
        grid=(B // nb, nc),
        in_specs=[pl.BlockSpec((nb, Q, CONV_DIM), lambda b, c: (b, c, 0)),
                  pl.BlockSpec((nb, Q, D_INNER), lambda b, c: (b, c, 0)),
                  pl.BlockSpec((nb, Q, LANES), lambda b, c: (b, c, 0)),
                  pl.BlockSpec((nb, SSM_H, Q), lambda b, c: (b, 0, c)),
                  pl.BlockSpec((nb, CONV_W - 1, CONV_DIM), lambda b, c: (b, 0, 0)),
                  pl.BlockSpec((nb, D_INNER, D_STATE), lambda b, c: (b, 0, 0)),
                  pl.BlockSpec((CONV_W, CONV_DIM), cmap), pl.BlockSpec((1, CONV_DIM), cmap),
                  pl.BlockSpec((1, LANES), cmap), pl.BlockSpec((SSM_H, 1), cmap),
                  pl.BlockSpec((1, D_INNER), cmap), pl.BlockSpec((1, D_INNER), cmap),
                  pl.BlockSpec((LANES, D_INNER), cmap)],
        out_specs=(pl.BlockSpec((nb, Q, D_INNER), lambda b, c: (b, c, 0)),
                   pl.BlockSpec((nb, CONV_W - 1, CONV_DIM), lambda b, c: (b, 0, 0)),
                   pl.BlockSpec((nb, D_INNER, D_STATE), lambda b, c: (b, 0, 0))),
        scratch_shapes=[pltpu.VMEM((nb, SSM_G, D_STATE, SSM_GW), f32),
                        pltpu.VMEM((nb, SUBLANES + Q, CONV_DIM), f32)],
        compiler_params=_params(("parallel", "arbitrary")),
        name="ssd_scan",
    )(xbc, zs, dt, dtT, conv0, ssm0, conv_w, conv_b.reshape(1, CONV_DIM), alog_row, alog_col,
      dskip_x, g_ssm.reshape(1, D_INNER), e_mat)


def _route(lg):
    lane = lax.broadcasted_iota(i32, lg.shape, 1)
    lanef = lane.astype(f32)
    big = float(LANES)
    gl = jnp.where(lane < N_EGROUPS, lg, NEG)
    gmax = jnp.max(gl, axis=1, keepdims=True)
    gsel = jnp.min(jnp.where(gl == gmax, lanef, big), axis=1, keepdims=True)
    p_g = 1.0 / jnp.sum(jnp.exp(gl - gmax), axis=1, keepdims=True)
    first = N_EGROUPS + E_PER_GROUP * gsel
    emask = (lanef >= first) & (lanef < first + E_PER_GROUP)
    el = jnp.where(emask, lg, NEG)
    v1 = jnp.max(el, axis=1, keepdims=True)
    i1 = jnp.min(jnp.where(emask & (el == v1), lanef, big), axis=1, keepdims=True)
    emask2 = emask & (lanef != i1)
    el2 = jnp.where(emask2, lg, NEG)
    v2 = jnp.max(el2, axis=1, keepdims=True)
    i2 = jnp.min(jnp.where(emask2 & (el2 == v2), lanef, big), axis=1, keepdims=True)
    t = jnp.exp(v2 - v1)
    w1 = p_g / (1.0 + t)
    w2 = p_g * t / (1.0 + t)
    out = jnp.where(lane == 0, i1 - N_EGROUPS, 0.0)
    out = jnp.where(lane == 1, i2 - N_EGROUPS, out)
    out = jnp.where(lane == 2, w1, out)
    out = jnp.where(lane == 3, w2, out)
    return out


def _to_row_tiles(ref, x, rows, first_row=0):
    for j in range(D_MODEL // LANES):
        ref[pl.ds(first_row * SUBLANES + j, rows, stride=SUBLANES), :] = x[:, j * LANES:(j + 1) * LANES]


def _from_row_tiles(ref, rows):
    return jnp.concatenate([ref[pl.ds(j, rows, stride=SUBLANES), :] for j in range(D_MODEL // LANES)], axis=1)


def _mix_kernel(x_ref, gt1_ref, sc2_ref, sh2_ref, o_ref, ys_ref, ga_ref, gs_ref,
                woa_ref, wos_ref, wo_ref, gpost_ref, gpre_ref, wr_ref, br_ref,
                x1_ref, u2_ref, route_ref):
    tm = x_ref.shape[0]
    n_parts = MIX_ROW_PARTS if tm % (MIX_ROW_PARTS * SUBLANES) == 0 else 1
    h = tm // n_parts

    def rows_of(ref, sl):
        return ref[sl, :] if ref.shape[0] == tm else ref[...]

    for part in range(n_parts):
        sl = slice(part * h, (part + 1) * h)
        ya = jnp.dot(o_ref[sl, :], woa_ref[...], preferred_element_type=f32)
        ys = jnp.dot(ys_ref[sl, :], wos_ref[...], preferred_element_type=f32)
        m = ga_ref[sl, :].astype(f32) * ya + gs_ref[sl, :].astype(f32) * ys
        mo = jnp.dot(m.astype(bf16), wo_ref[...], preferred_element_type=f32)
        x1 = x_ref[sl, :] + rows_of(gt1_ref, sl) * _rms(mo, gpost_ref[...], EPS)
        x1_ref[sl, :] = x1
        u2 = _rms(x1, gpre_ref[...], EPS) * (1.0 + rows_of(sc2_ref, sl)) + rows_of(sh2_ref, sl)
        _to_row_tiles(u2_ref, u2, h, first_row=part * h)
        lg = jnp.dot(u2.astype(bf16), wr_ref[...], preferred_element_type=f32) + br_ref[...]
        route_ref[sl, :] = _route(lg)


def _mix(grp, x2, o, ysn, ga, gs, w_oa, w_os, w_o, g_post1, g_pre2, w_r, b_r):
    T, tm = grp.T, grp.tm
    return pl.pallas_call(
        _mix_kernel,
        out_shape=(jax.ShapeDtypeStruct((T, D_MODEL), f32),
                   jax.ShapeDtypeStruct((T * SUBLANES, LANES), f32),
                   jax.ShapeDtypeStruct((T, LANES), f32)),
        grid=(grp.n_tiles,),
        in_specs=[grp.row_spec(D_MODEL), grp.mod_spec(2), grp.mod_spec(4), grp.mod_spec(3),
                  grp.row_spec(ATTN_W), grp.row_spec(D_INNER), grp.row_spec(D_MODEL), grp.row_spec(D_MODEL),
                  _const_spec(w_oa.shape), _const_spec(w_os.shape), _const_spec(w_o.shape),
                  _const_spec((1, D_MODEL)), _const_spec((1, D_MODEL)),
                  _const_spec(w_r.shape), _const_spec((1, LANES))],
        out_specs=(grp.row_spec(D_MODEL),
                   pl.BlockSpec((tm * SUBLANES, LANES), lambda i: (i, 0)),
                   grp.row_spec(LANES)),
        compiler_params=_params(("parallel",)),
        name="mixer_out",
    )(x2, grp.mod, grp.mod, grp.mod, o, ysn, ga, gs, w_oa, w_os, w_o, g_post1, g_pre2, w_r, b_r)


def _count_kernel(route_ref, cnt_ref):
    i = pl.program_id(0)

    @pl.when(i == 0)
    def _():
        cnt_ref[...] = jnp.zeros(cnt_ref.shape, f32)

    r = route_ref[...]
    lanef = lax.broadcasted_iota(i32, r.shape, 1).astype(f32)
    hits = (lanef == r[:, 0:1]).astype(f32) + (lanef == r[:, 1:2]).astype(f32)
    cnt_ref[...] += jnp.sum(hits, axis=0, keepdims=True)


def _expert_counts(route, tm):
    T = route.shape[0]
    return pl.pallas_call(
        _count_kernel,
        out_shape=jax.ShapeDtypeStruct((1, LANES), f32),
        grid=(T // tm,),
        in_specs=[pl.BlockSpec((tm, LANES), lambda i: (i, 0))],
        out_specs=pl.BlockSpec((1, LANES), lambda i: (0, 0)),
        compiler_params=_params(("arbitrary",)),
        name="moe_count",
    )(route)


def _slot_kernel(route_ref, start_ref, dest_ref, carry):
    i = pl.program_id(0)

    @pl.when(i == 0)
    def _():
        carry[...] = jnp.broadcast_to(start_ref[...], carry.shape)

    r = route_ref[...]
    tm = r.shape[0]
    lane = lax.broadcasted_iota(i32, r.shape, 1)
    lanef = lane.astype(f32)
    oh1 = (lanef == r[:, 0:1]).astype(f32)
    oh2 = (lanef == r[:, 1:2]).astype(f32)
    cnt = oh1 + oh2
    ri = lax.broadcasted_iota(i32, (tm, tm), 0)
    ci = lax.broadcasted_iota(i32, (tm, tm), 1)
    before = (ci < ri).astype(bf16)
    pos = jnp.dot(before, cnt.astype(bf16), preferred_element_type=f32) + carry[0:1, :]
    d1 = jnp.sum(oh1 * pos, axis=1, keepdims=True)
    d2 = jnp.sum(oh2 * pos, axis=1, keepdims=True)
    dest_ref[...] = jnp.where(lane == 0, d1, jnp.where(lane == 1, d2, 0.0)).astype(i32)
    carry[...] = carry[...] + jnp.sum(cnt, axis=0, keepdims=True)


def _slots(route, starts, tm):
    T = route.shape[0]
    return pl.pallas_call(
        _slot_kernel,
        out_shape=jax.ShapeDtypeStruct((T, LANES), i32),
        grid=(T // tm,),
        in_specs=[pl.BlockSpec((tm, LANES), lambda i: (i, 0)), pl.BlockSpec((1, LANES), lambda i: (0, 0))],
        out_specs=pl.BlockSpec((tm, LANES), lambda i: (i, 0)),
        scratch_shapes=[pltpu.VMEM((SUBLANES, LANES), f32)],
        compiler_params=_params(("arbitrary",)),
        name="moe_slots",
    )(route, starts)


def _row_copy(src, src_row, dst, dst_row, sem):
    return pltpu.make_async_copy(src.at[pl.ds(pl.multiple_of(src_row * SUBLANES, SUBLANES), SUBLANES), :],
                                 dst.at[pl.ds(pl.multiple_of(dst_row * SUBLANES, SUBLANES), SUBLANES), :], sem)


def _dispatch_kernel(dest_ref, u_ref, xs_in_ref, xs_ref, sem, *, tm):
    del xs_in_ref

    def issue(t, c):
        for k in range(2):
            _row_copy(u_ref, t, xs_ref, dest_ref[0, 0, 2 * t + k], sem).start(priority=k)
        return c

    lax.fori_loop(0, tm, issue, 0)
    rows = 2 * tm * SUBLANES
    pltpu.make_async_copy(xs_ref.at[pl.ds(0, rows), :], xs_ref.at[pl.ds(0, rows), :], sem).wait()


def _dispatch(dest2, u_rows, n_slots, tm):
    T = dest2.shape[0]
    xs0 = jnp.zeros((n_slots * SUBLANES, LANES), f32)
    dest3 = dest2.reshape(T // tm, 1, 2 * tm)
    return pl.pallas_call(
        functools.partial(_dispatch_kernel, tm=tm),
        out_shape=jax.ShapeDtypeStruct((n_slots * SUBLANES, LANES), f32),
        grid=(T // tm,),
        in_specs=[pl.BlockSpec((1, 1, 2 * tm), lambda i: (i, 0, 0), memory_space=pltpu.SMEM),
                  pl.BlockSpec((tm * SUBLANES, LANES), lambda i: (i, 0)), pl.BlockSpec(memory_space=pl.ANY)],
        out_specs=pl.BlockSpec(memory_space=pl.ANY),
        scratch_shapes=[pltpu.SemaphoreType.DMA(())],
        input_output_aliases={2: 0},
        compiler_params=_params(("arbitrary",)),
        name="moe_dispatch",
    )(dest3, u_rows, xs0)


def _expert_kernel(be_ref, xs_ref, wg_ref, wu_ref, wd_ref, o_ref, *, blk):
    del be_ref
    x = _from_row_tiles(xs_ref, blk).astype(bf16)
    h = _silu(jnp.dot(x, wg_ref[...], preferred_element_type=f32)) * jnp.dot(x, wu_ref[...], preferred_element_type=f32)
    _to_row_tiles(o_ref, jnp.dot(h.astype(bf16), wd_ref[...], preferred_element_type=f32), blk)


def _experts(xs, blk_e, w_gate, w_up, w_down, blk):
    n_blocks = blk_e.shape[0]
    rows = blk * SUBLANES
    grid_spec = pltpu.PrefetchScalarGridSpec(
        num_scalar_prefetch=1,
        grid=(n_blocks,),
        in_specs=[pl.BlockSpec((rows, LANES), lambda i, be: (i, 0)),
                  pl.BlockSpec((None, D_MODEL, D_FF), lambda i, be: (be[i], 0, 0)),
                  pl.BlockSpec((None, D_MODEL, D_FF), lambda i, be: (be[i], 0, 0)),
                  pl.BlockSpec((None, D_FF, D_MODEL), lambda i, be: (be[i], 0, 0))],
        out_specs=pl.BlockSpec((rows, LANES), lambda i, be: (i, 0)),
    )
    return pl.pallas_call(
        functools.partial(_expert_kernel, blk=blk),
        out_shape=jax.ShapeDtypeStruct(xs.shape, f32),
        grid_spec=grid_spec,
        compiler_params=_params(("parallel",)),
        name="moe_experts",
    )(blk_e, xs, w_gate, w_up, w_down)


def _combine_kernel(dest_ref, dest_next_ref, outs_ref, x1_ref, gt2_ref, route_ref, gpost_ref, y_ref, buf, sem, *, tm):
    i = pl.program_id(0)
    slot = jnp.bitwise_and(i, 1)

    def gather(d_ref, s):
        def issue(t, c):
            for k in range(2):
                _row_copy(outs_ref, d_ref[0, 0, 2 * t + k], buf.at[s, k], t, sem.at[s]).start()
            return c
        lax.fori_loop(0, tm, issue, 0)

    @pl.when(i == 0)
    def _():
        gather(dest_ref, 0)

    @pl.when(i + 1 < pl.num_programs(0))
    def _():
        gather(dest_next_ref, 1 - slot)

    pltpu.make_async_copy(buf.at[slot], buf.at[slot], sem.at[slot]).wait()
    r = route_ref[...]
    f = r[:, 2:3] * _from_row_tiles(buf.at[slot, 0], tm) + r[:, 3:4] * _from_row_tiles(buf.at[slot, 1], tm)
    y_ref[...] = x1_ref[...] + gt2_ref[...] * _rms(f, gpost_ref[...], EPS)


def _combine(grp, dest2, outs, x1, route, g_post2, tm):
    T = grp.T
    n = T // tm
    dest3 = dest2.reshape(n, 1, 2 * tm)
    return pl.pallas_call(
        functools.partial(_combine_kernel, tm=tm),
        out_shape=jax.ShapeDtypeStruct((T, D_MODEL), f32),
        grid=(n,),
        in_specs=[pl.BlockSpec((1, 1, 2 * tm), lambda i: (i, 0, 0), memory_space=pltpu.SMEM),
                  pl.BlockSpec((1, 1, 2 * tm), lambda i: (jnp.minimum(i + 1, n - 1), 0, 0), memory_space=pltpu.SMEM),
                  pl.BlockSpec(memory_space=pl.ANY),
                  grp.row_spec(D_MODEL), grp.mod_spec(5), grp.row_spec(LANES), _const_spec((1, D_MODEL))],
        out_specs=grp.row_spec(D_MODEL),
        scratch_shapes=[pltpu.VMEM((2, 2, tm * SUBLANES, LANES), f32), pltpu.SemaphoreType.DMA((2,))],
        compiler_params=_params(("arbitrary",)),
        name="moe_combine",
    )(dest3, dest3, outs, x1, grp.mod, route, g_post2)


def _hmoe_and_residual(grp, x1, u_rows, route, w_gate, w_up, w_down, g_post2):
    T = grp.T
    tm = grp.tm
    counts = _expert_counts(route, tm)[0, :N_EXPERTS].astype(i32)
    padded = (counts + MOE_BLOCK - 1) // MOE_BLOCK * MOE_BLOCK
    ends = jnp.cumsum(padded)
    starts = ends - padded
    n_blocks = -(-2 * T // MOE_BLOCK) + N_EXPERTS
    n_slots = n_blocks * MOE_BLOCK
    blk_start = jnp.arange(n_blocks, dtype=i32) * MOE_BLOCK
    blk_e = jnp.minimum(jnp.sum(blk_start[:, None] >= ends[None, :], axis=1), N_EXPERTS - 1).astype(i32)
    starts_row = jnp.pad(starts.astype(f32), (0, LANES - N_EXPERTS)).reshape(1, LANES)
    dest2 = _slots(route, starts_row, tm)[:, :2]
    xs = _dispatch(dest2, u_rows, n_slots, tm)
    outs = _experts(xs, blk_e, w_gate, w_up, w_down, MOE_BLOCK)
    return _combine(grp, dest2, outs, x1, route, g_post2, tm)


def _rope_tables(pos):
    half = HEAD_DIM // 2
    inv = ROPE_THETA ** (-2.0 * jnp.arange(half, dtype=f32) / HEAD_DIM)
    ang = pos.astype(f32)[:, None] * inv[None, :]
    cos, sin = jnp.cos(ang), jnp.sin(ang)
    return jnp.tile(cos, (1, LANES // half)), jnp.tile(jnp.concatenate([-sin, sin], axis=1), (1, LANES // HEAD_DIM))


def _layer(grp, x2, lw, lam_init, attend, conv0, ssm0):
    n_seq, seq_len = grp.n_seq, grp.seq_len
    q, k, kb, v, vb = _proj_qkv(grp, x2, lw["g_pre1"], lw["w_qkv"], 1, 0)
    zs, xbc = _proj_zx(grp, x2, lw["g_pre1"], lw["w_z"], lw["w_xbc"], 1, 0)
    ga, gs, dt = _proj_gates(grp, x2, lw["g_pre1"], lw["w_gates"], lw["w_dt"], lw["dt_bias"], 1, 0)
    shp = lambda a: a.reshape(n_seq, seq_len, a.shape[-1])
    o = attend(shp(q), shp(kb), shp(vb))
    ysn, conv_new, ssm_new = _ssd(shp(xbc), shp(zs), shp(dt), conv0, ssm0.reshape(n_seq, D_INNER, D_STATE),
                                  lw["conv_w"], lw["conv_b"], lw["a_log"], lw["d_skip"], lw["g_ssm"], lw["e_mat"])
    x1, u_rows, route = _mix(grp, x2, o.reshape(grp.T, ATTN_W), ysn.reshape(grp.T, D_INNER), ga, gs,
                             lw["w_oa"], lw["w_os"], lw["w_o"], lw["g_post1"], lw["g_pre2"], lw["w_r"], lw["b_r"])
    y2 = _hmoe_and_residual(grp, x1, u_rows, route, lw["w_gate"], lw["w_up"], lw["w_down"], lw["g_post2"])
    return (y2, k.reshape(n_seq, seq_len, N_KV, HEAD_W), v.reshape(n_seq, seq_len, N_KV, HEAD_W), conv_new,
            ssm_new.reshape(n_seq, SSM_H, SSM_P, D_STATE))


def kernel(x_prompt, x_sample, cache_k, cache_v, state_conv, state_ssm, page_table, c_prompt, c_sample, w_ada, b_ada, g_pre1, g_post1, g_pre2, g_post2, w_in, lambda_q1, lambda_k1, lambda_q2, lambda_k2, g_subln, w_oa, conv_w, conv_b, dt_bias, a_log, d_skip, g_ssm, w_os, w_o, w_rg, b_rg, w_re, b_re, w_gate, w_up, w_down):
    depth = w_in.shape[0]
    bp, lp_len, _ = x_prompt.shape
    bs, ls_len, _ = x_sample.shape
    past = page_table.shape[1] * cache_k.shape[2]
    ck = cache_k.reshape(cache_k.shape[0], cache_k.shape[1], cache_k.shape[2] * N_KV, HEAD_W)
    cv = cache_v.reshape(cache_v.shape[0], cache_v.shape[1], cache_v.shape[2] * N_KV, HEAD_W)
    rope_p = _rope_tables(jnp.arange(lp_len, dtype=i32))
    rope_s8 = _rope_tables(past + jnp.arange(ls_len, dtype=i32))
    rope_s = tuple(jnp.tile(t, (bs, 1)) for t in rope_s8)
    e_mat = (jnp.arange(LANES, dtype=i32)[:, None] == jnp.arange(D_INNER, dtype=i32)[None, :] // SSM_P).astype(bf16)

    hp = x_prompt.reshape(bp * lp_len, D_MODEL)
    hs = x_sample.reshape(bs * ls_len, D_MODEL)
    c_all = jnp.concatenate([c_prompt, c_sample], axis=0)
    outs = [[] for _ in range(8)]
    for l in range(depth):
        lam_init = 0.8 - 0.6 * math.exp(-0.3 * l)
        w_in_l = w_in[l].astype(bf16)
        lw = dict(
            w_qkv=w_in_l[:, :OFF_Z], w_z=w_in_l[:, OFF_Z:OFF_XBC], w_xbc=w_in_l[:, OFF_XBC:OFF_DT],
            w_dt=jnp.pad(w_in_l[:, OFF_DT:OFF_GA], ((0, 0), (0, LANES - SSM_H))), w_gates=w_in_l[:, OFF_GA:],
            dt_bias=jnp.pad(dt_bias[l].reshape(1, SSM_H), ((0, 0), (0, LANES - SSM_H))),
            g_pre1=g_pre1[l].reshape(1, D_MODEL), g_post1=g_post1[l].reshape(1, D_MODEL),
            g_pre2=g_pre2[l].reshape(1, D_MODEL), g_post2=g_post2[l].reshape(1, D_MODEL),
            w_oa=w_oa[l].astype(bf16), w_os=w_os[l].astype(bf16), w_o=w_o[l].astype(bf16),
            conv_w=conv_w[l], conv_b=conv_b[l], a_log=a_log[l], d_skip=d_skip[l], g_ssm=g_ssm[l], e_mat=e_mat,
            w_r=jnp.pad(jnp.concatenate([w_rg[l], w_re[l]], axis=1),
                        ((0, 0), (0, LANES - N_EGROUPS - N_EXPERTS))).astype(bf16),
            b_r=jnp.pad(jnp.concatenate([b_rg[l], b_re[l]]), (0, LANES - N_EGROUPS - N_EXPERTS)).reshape(1, LANES),
            w_gate=w_gate[l].astype(bf16), w_up=w_up[l].astype(bf16), w_down=w_down[l].astype(bf16),
        )
        lam_p = jnp.stack([lambda_q1[l], lambda_k1[l], lambda_q2[l], lambda_k2[l]])
        gsub = g_subln[l].reshape(1, HEAD_W)
        mod = _ada(c_all, w_ada[l].astype(bf16), b_ada[l])
        grp_p = _Group(bp, lp_len, mod[:bp], rope_p, per_token=False)
        grp_s = _Group(bs, ls_len, jnp.repeat(mod[bp:], ls_len, axis=0), rope_s, per_token=True)

        attend_p = lambda q, kb, vb: _attn_prompt(q, kb, vb, lam_p, gsub, lam_init)
        attend_s = lambda q, kb, vb: _attn_sample(q, kb, vb, ck, cv, page_table, l, lam_p, gsub, lam_init)
        conv0 = jnp.zeros((bp, CONV_W - 1, CONV_DIM), f32)
        ssm0 = jnp.zeros((bp, SSM_H, SSM_P, D_STATE), f32)
        hp, kp, vp, cp, sp = _layer(grp_p, hp, lw, lam_init, attend_p, conv0, ssm0)
        hs, ksm, vsm, csm, ssm = _layer(grp_s, hs, lw, lam_init, attend_s, state_conv[l], state_ssm[l])
        for lst, val in zip(outs, (kp, vp, cp, sp, ksm, vsm, csm, ssm)):
            lst.append(val)
    stacked = [jnp.stack(o) for o in outs]
    return (hp.reshape(bp, lp_len, D_MODEL), hs.reshape(bs, ls_len, D_MODEL), *stacked)
```

```python
import functools
import math

import jax
import jax.numpy as jnp
from jax import lax
from jax.experimental import pallas as pl
from jax.experimental.pallas import tpu as pltpu

f32 = jnp.float32
bf16 = jnp.bfloat16
i32 = jnp.int32

D_MODEL = 1024
N_HEADS = 8
N_KV = 4
HEAD_DIM = 64
HEAD_W = 2 * HEAD_DIM
ATTN_W = N_HEADS * HEAD_W
KV_W = N_KV * HEAD_W
ATTN_SCALE = HEAD_DIM ** -0.5
ROPE_THETA = 10000.0
SUBLN_EPS = 1e-5
D_INNER = 2048
SSM_P = 64
SSM_H = D_INNER // SSM_P
SSM_G = 4
SSM_GW = D_INNER // SSM_G
D_STATE = 128
CONV_W = 4
CONV_DIM = D_INNER + 2 * SSM_G * D_STATE
SSD_CHUNK = 128
SSM_EPS = 1e-5
N_EGROUPS = 4
E_PER_GROUP = 8
N_EXPERTS = 32
D_FF = 512
EPS = 1e-6
OFF_K = ATTN_W
OFF_V = OFF_K + KV_W
OFF_Z = OFF_V + KV_W
OFF_XBC = OFF_Z + D_INNER
OFF_DT = OFF_XBC + CONV_DIM
OFF_GA = OFF_DT + SSM_H
OFF_GS = OFF_GA + D_MODEL

LANES = 128
SUBLANES = 8
VMEM_LIMIT = 56 * 1024 * 1024
NEG = -1e30
LOG2E = math.log2(math.e)

TOKEN_TILE = 512
ATTN_TILE = 512
PAGES_PER_STEP = 16
MOE_BLOCK = 512
MOE_BLOCK_SMALL = 128
SSD_SHORT_SEQS_PER_STEP = 2
MIX_ROW_PARTS = 1


def _params(sem, vmem=VMEM_LIMIT):
    return pltpu.CompilerParams(dimension_semantics=sem, vmem_limit_bytes=vmem)


def _sigmoid(x):
    return 1.0 / (1.0 + jnp.exp(-x))


def _silu(x):
    return x * _sigmoid(x)


def _rms(x, g, eps):
    return x * lax.rsqrt(jnp.mean(x * x, axis=-1, keepdims=True) + eps) * g


def _bdot(a, b):
    return jnp.dot(a.astype(bf16), b.astype(bf16), preferred_element_type=f32)


class _Group:
    def __init__(self, n_seq, seq_len, mod_rows, rope_rows, per_token):
        self.n_seq, self.seq_len = n_seq, seq_len
        self.T = n_seq * seq_len
        self.tm = min(TOKEN_TILE, self.T if per_token else seq_len)
        assert self.T % self.tm == 0
        self.per_token = per_token
        if per_token:
            self.mod = mod_rows
            self.rope = rope_rows
        else:
            assert seq_len % self.tm == 0
            self.tiles_per_seq = seq_len // self.tm
            self.mod = mod_rows.reshape(n_seq, 1, 6 * D_MODEL)
            self.rope = rope_rows
        self.n_tiles = self.T // self.tm

    def mod_spec(self, col):
        if self.per_token:
            return pl.BlockSpec((self.tm, D_MODEL), lambda i: (i, col))
        tps = self.tiles_per_seq
        return pl.BlockSpec((None, 1, D_MODEL), lambda i: (i // tps, 0, col))

    def rope_spec(self):
        if self.per_token:
            return pl.BlockSpec((self.tm, LANES), lambda i: (i, 0))
        tps = self.tiles_per_seq
        return pl.BlockSpec((self.tm, LANES), lambda i: (i % tps, 0))

    def row_spec(self, width):
        return pl.BlockSpec((self.tm, width), lambda i: (i, 0))


def _const_spec(shape):
    nd = len(shape)
    return pl.BlockSpec(shape, lambda i: (0,) * nd)


def _ada_kernel(c_ref, w_ref, b_ref, o_ref):
    o_ref[...] = _bdot(_silu(c_ref[...]), w_ref[...]) + b_ref[...]


def _ada(c, w_ada, b_ada):
    m = c.shape[0]
    n = w_ada.shape[1]
    tn = D_MODEL
    return pl.pallas_call(
        _ada_kernel,
        out_shape=jax.ShapeDtypeStruct((m, n), f32),
        grid=(n // tn,),
        in_specs=[pl.BlockSpec((m, D_MODEL), lambda j: (0, 0)),
                  pl.BlockSpec((D_MODEL, tn), lambda j: (0, j)),
                  pl.BlockSpec((1, tn), lambda j: (0, j))],
        out_specs=pl.BlockSpec((m, tn), lambda j: (0, j)),
        compiler_params=_params(("parallel",)),
        name="ada_mod",
    )(c, w_ada, b_ada.reshape(1, n))


def _modulated(x_ref, g_ref, sc_ref, sh_ref):
    u = _rms(x_ref[...], g_ref[...], EPS) * (1.0 + sc_ref[...]) + sh_ref[...]
    return u.astype(bf16)


def _rope_cols(x, n_blocks, cos, sin_signed, lo):
    out = []
    for j in range(n_blocks):
        xb = x[:, j * LANES:(j + 1) * LANES]
        partner = jnp.where(lo, pltpu.roll(xb, LANES - HEAD_DIM // 2, 1), pltpu.roll(xb, HEAD_DIM // 2, 1))
        out.append(xb * cos + partner * sin_signed)
    return jnp.concatenate(out, axis=1)


def _qkv_kernel(x_ref, g_ref, sc_ref, sh_ref, w_ref, cos_ref, sin_ref,
                q_ref, k_ref, kb_ref, v_ref, vb_ref):
    u = _modulated(x_ref, g_ref, sc_ref, sh_ref)
    proj = jnp.dot(u, w_ref[...], preferred_element_type=f32)
    cos, sin_signed = cos_ref[...], sin_ref[...]
    lane = lax.broadcasted_iota(i32, cos.shape, 1)
    lo = jnp.bitwise_and(lane, HEAD_DIM - 1) < HEAD_DIM // 2
    q = _rope_cols(proj[:, :OFF_K], N_HEADS, cos, sin_signed, lo)
    k = _rope_cols(proj[:, OFF_K:OFF_V], N_KV, cos, sin_signed, lo)
    v = proj[:, OFF_V:OFF_Z]
    q_ref[...] = (q * (ATTN_SCALE * LOG2E)).astype(bf16)
    kb_ref[...] = k.astype(bf16)
    vb_ref[...] = v.astype(bf16)
    tm = k.shape[0]
    for h in range(N_KV):
        k_ref[pl.ds(h, tm, stride=N_KV), :] = k[:, h * HEAD_W:(h + 1) * HEAD_W]
        v_ref[pl.ds(h, tm, stride=N_KV), :] = v[:, h * HEAD_W:(h + 1) * HEAD_W]


def _proj_qkv(grp, x2, g, w_qkv, col_scale, col_shift):
    T, tm = grp.T, grp.tm
    head_rows = pl.BlockSpec((tm * N_KV, HEAD_W), lambda i: (i, 0))
    return pl.pallas_call(
        _qkv_kernel,
        out_shape=(jax.ShapeDtypeStruct((T, ATTN_W), bf16),
                   jax.ShapeDtypeStruct((T * N_KV, HEAD_W), f32), jax.ShapeDtypeStruct((T, KV_W), bf16),
                   jax.ShapeDtypeStruct((T * N_KV, HEAD_W), f32), jax.ShapeDtypeStruct((T, KV_W), bf16)),
        grid=(grp.n_tiles,),
        in_specs=[grp.row_spec(D_MODEL), _const_spec((1, D_MODEL)),
                  grp.mod_spec(col_scale), grp.mod_spec(col_shift),
                  _const_spec(w_qkv.shape), grp.rope_spec(), grp.rope_spec()],
        out_specs=(grp.row_spec(ATTN_W), head_rows, grp.row_spec(KV_W),
                   head_rows, grp.row_spec(KV_W)),
        compiler_params=_params(("parallel",)),
        name="proj_qkv",
    )(x2, g, grp.mod, grp.mod, w_qkv, grp.rope[0], grp.rope[1])


def _zx_kernel(x_ref, g_ref, sc_ref, sh_ref, wz_ref, wx_ref, zs_ref, xbc_ref):
    u = _modulated(x_ref, g_ref, sc_ref, sh_ref)
    zs_ref[...] = _silu(jnp.dot(u, wz_ref[...], preferred_element_type=f32)).astype(bf16)
    xbc_ref[...] = jnp.dot(u, wx_ref[...], preferred_element_type=f32).astype(bf16)


def _proj_zx(grp, x2, g, w_z, w_xbc, col_scale, col_shift):
    T = grp.T
    return pl.pallas_call(
        _zx_kernel,
        out_shape=(jax.ShapeDtypeStruct((T, D_INNER), bf16), jax.ShapeDtypeStruct((T, CONV_DIM), bf16)),
        grid=(grp.n_tiles,),
        in_specs=[grp.row_spec(D_MODEL), _const_spec((1, D_MODEL)),
                  grp.mod_spec(col_scale), grp.mod_spec(col_shift),
                  _const_spec(w_z.shape), _const_spec(w_xbc.shape)],
        out_specs=(grp.row_spec(D_INNER), grp.row_spec(CONV_DIM)),
        compiler_params=_params(("parallel",)),
        name="proj_zx",
    )(x2, g, grp.mod, grp.mod, w_z, w_xbc)


def _gates_kernel(x_ref, g_ref, sc_ref, sh_ref, wg_ref, wdt_ref, dtb_ref, ga_ref, gs_ref, dt_ref):
    u = _modulated(x_ref, g_ref, sc_ref, sh_ref)
    gates = _sigmoid(jnp.dot(u, wg_ref[...], preferred_element_type=f32))
    ga_ref[...] = gates[:, :D_MODEL].astype(bf16)
    gs_ref[...] = gates[:, D_MODEL:].astype(bf16)
    raw = jnp.dot(u, wdt_ref[...], preferred_element_type=f32) + dtb_ref[...]
    dt_ref[...] = jnp.maximum(raw, 0.0) + jnp.log1p(jnp.exp(-jnp.abs(raw)))


def _proj_gates(grp, x2, g, w_gates, w_dt, dt_bias, col_scale, col_shift):
    T = grp.T
    return pl.pallas_call(
        _gates_kernel,
        out_shape=(jax.ShapeDtypeStruct((T, D_MODEL), bf16), jax.ShapeDtypeStruct((T, D_MODEL), bf16),
                   jax.ShapeDtypeStruct((T, LANES), f32)),
        grid=(grp.n_tiles,),
        in_specs=[grp.row_spec(D_MODEL), _const_spec((1, D_MODEL)),
                  grp.mod_spec(col_scale), grp.mod_spec(col_shift),
                  _const_spec(w_gates.shape), _const_spec(w_dt.shape), _const_spec((1, LANES))],
        out_specs=(grp.row_spec(D_MODEL), grp.row_spec(D_MODEL), grp.row_spec(LANES)),
        compiler_params=_params(("parallel",)),
        name="proj_gates",
    )(x2, g, grp.mod, grp.mod, w_gates, w_dt, dt_bias)


def _lambda_full(lam_ref, lam_init):
    lp = lam_ref[...]
    l1 = jnp.sum(lp[0:1] * lp[1:2], axis=1, keepdims=True)
    l2 = jnp.sum(lp[2:3] * lp[3:4], axis=1, keepdims=True)
    return jnp.exp(l1) - jnp.exp(l2) + lam_init


def _with_ones(v):
    return jnp.concatenate([v, jnp.ones(v.shape, bf16)], axis=1)


def _softmax_step(s, v1, m_ref, acc_ref):
    reps = s.shape[1] // LANES if s.shape[1] % LANES == 0 else 0
    m_prev = m_ref[...]
    m_new = jnp.maximum(m_prev, jnp.max(s, axis=1, keepdims=True))
    alpha = jnp.exp2(m_prev - m_new)
    if reps:
        p = jnp.exp2(s - jnp.concatenate([m_new] * reps, axis=1))
    else:
        p = jnp.exp2(s - m_new[:, 0:1])
    acc_ref[...] = (jnp.concatenate([alpha, alpha], axis=1) * acc_ref[...]
                    + jnp.dot(p.astype(bf16), v1, preferred_element_type=f32))
    m_ref[...] = m_new


def _diff_heads(acc, lam, gsub, lam_init, rows):
    outs = []
    for r in range(2):
        a0 = acc[(2 * r) * rows:(2 * r + 1) * rows]
        a1 = acc[(2 * r + 1) * rows:(2 * r + 2) * rows]
        o = a0[:, :HEAD_W] / a0[:, HEAD_W:] - lam * (a1[:, :HEAD_W] / a1[:, HEAD_W:])
        outs.append(_rms(o, gsub, SUBLN_EPS) * (1.0 - lam_init))
    return jnp.concatenate(outs, axis=1)


def _pad_components(q, rows):
    lane = lax.broadcasted_iota(i32, (rows, LANES), 1)
    lo = lane < HEAD_DIM
    blocks = []
    for r in range(2):
        qh = q[:, r * LANES:(r + 1) * LANES].astype(f32)
        blocks.append(jnp.where(lo, qh, 0.0))
        blocks.append(jnp.where(lo, 0.0, qh))
    return jnp.concatenate(blocks, axis=0)


def _stacked_causal(n, copies=4):
    assert n & (n - 1) == 0
    row = jnp.bitwise_and(lax.broadcasted_iota(i32, (copies * n, n), 0), n - 1)
    col = lax.broadcasted_iota(i32, (copies * n, n), 1)
    return col <= row


def _attn_prompt_kernel(lam_ref, gsub_ref, q_ref, k_ref, v_ref, o_ref, qpad, m_s, acc_s, s_a, s_b, *, tq, lam_init):
    i = pl.program_id(2)
    qpad[...] = _pad_components(q_ref[0], tq).astype(bf16)
    m_s[...] = jnp.full(m_s.shape, NEG, f32)
    acc_s[...] = jnp.zeros(acc_s.shape, f32)

    def scores(c):
        kc = k_ref[0, pl.ds(pl.multiple_of(c * tq, tq), tq), :]
        return lax.dot_general(qpad[...], kc, (((1,), (1,)), ((), ())), preferred_element_type=f32)

    def consume(s_ref, c):
        _softmax_step(s_ref[...], _with_ones(v_ref[0, pl.ds(pl.multiple_of(c * tq, tq), tq), :]), m_s, acc_s)

    s_a[...] = jnp.where(_stacked_causal(tq), scores(i), NEG)

    def pair(p):
        s_b[...] = scores(2 * p)
        consume(s_a, jnp.where(p == 0, i, 2 * p - 1))
        s_a[...] = scores(jnp.minimum(2 * p + 1, i - 1))
        consume(s_b, 2 * p)

    def two_pairs(t, carry):
        pair(2 * t)
        pair(2 * t + 1)
        return carry

    n_pairs = jnp.right_shift(i + 1, 1)
    lax.fori_loop(0, jnp.right_shift(n_pairs, 1), two_pairs, 0)

    @pl.when(jnp.bitwise_and(n_pairs, 1) == 1)
    def _():
        pair(n_pairs - 1)

    @pl.when(jnp.bitwise_and(i, 1) == 0)
    def _():
        consume(s_a, jnp.maximum(i - 1, 0))

    lam = _lambda_full(lam_ref, lam_init)
    o_ref[0] = _diff_heads(acc_s[...], lam, gsub_ref[...], lam_init, tq).astype(bf16)


def _attn_prompt(q, kb, vb, lam_p, gsub, lam_init):
    B, L, _ = q.shape
    tq = min(ATTN_TILE, L)
    assert L % tq == 0
    kern = functools.partial(_attn_prompt_kernel, tq=tq, lam_init=lam_init)
    return pl.pallas_call(
        kern,
        out_shape=jax.ShapeDtypeStruct((B, L, ATTN_W), bf16),
        grid=(B, N_KV, L // tq),
        in_specs=[pl.BlockSpec((4, HEAD_DIM), lambda b, g, i: (0, 0)),
                  pl.BlockSpec((1, HEAD_W), lambda b, g, i: (0, 0)),
                  pl.BlockSpec((1, tq, 2 * HEAD_W), lambda b, g, i: (b, i, g)),
                  pl.BlockSpec((1, L, HEAD_W), lambda b, g, i: (b, 0, g)),
                  pl.BlockSpec((1, L, HEAD_W), lambda b, g, i: (b, 0, g))],
        out_specs=pl.BlockSpec((1, tq, 2 * HEAD_W), lambda b, g, i: (b, i, g)),
        scratch_shapes=[pltpu.VMEM((4 * tq, LANES), bf16), pltpu.VMEM((4 * tq, LANES), f32),
                        pltpu.VMEM((4 * tq, 2 * LANES), f32),
                        pltpu.VMEM((4 * tq, tq), f32), pltpu.VMEM((4 * tq, tq), f32)],
        compiler_params=_params(("parallel", "parallel", "arbitrary")),
        name="attn_prompt",
    )(lam_p, gsub, q, kb, vb)


def _attn_sample_kernel(pt_ref, lam_ref, gsub_ref, q_ref, kn_ref, vn_ref, *rest, n_pages_step, seq_new, lam_init):
    del pt_ref
    kp = rest[:n_pages_step]
    vp = rest[n_pages_step:2 * n_pages_step]
    o_ref, qpad, m_s, acc_s = rest[2 * n_pages_step:]
    j = pl.program_id(1)

    @pl.when(j == 0)
    def _():
        q = q_ref[0]
        for g in range(N_KV):
            qpad[g] = _pad_components(q[:, g * 2 * HEAD_W:(g + 1) * 2 * HEAD_W], seq_new)
        m_s[...] = jnp.full(m_s.shape, NEG, f32)
        acc_s[...] = jnp.zeros(acc_s.shape, f32)

    page = kp[0].shape[0] // N_KV
    for g in range(N_KV):
        qg = qpad[g].astype(bf16)
        kg = jnp.concatenate([r[pl.ds(g, page, stride=N_KV), :].astype(bf16) for r in kp], axis=0)
        vg = jnp.concatenate([r[pl.ds(g, page, stride=N_KV), :].astype(bf16) for r in vp], axis=0)
        s = lax.dot_general(qg, kg, (((1,), (1,)), ((), ())), preferred_element_type=f32)
        _softmax_step(s, _with_ones(vg), m_s.at[g], acc_s.at[g])

    @pl.when(j == pl.num_programs(1) - 1)
    def _():
        lam = _lambda_full(lam_ref, lam_init)
        causal = _stacked_causal(seq_new)
        outs = []
        for g in range(N_KV):
            qg = qpad[g].astype(bf16)
            kg = kn_ref[0][:, g * HEAD_W:(g + 1) * HEAD_W]
            vg = vn_ref[0][:, g * HEAD_W:(g + 1) * HEAD_W]
            s = lax.dot_general(qg, kg, (((1,), (1,)), ((), ())), preferred_element_type=f32)
            _softmax_step(jnp.where(causal, s, NEG), _with_ones(vg), m_s.at[g], acc_s.at[g])
            outs.append(_diff_heads(acc_s[g], lam, gsub_ref[...], lam_init, seq_new))
        o_ref[0] = jnp.concatenate(outs, axis=1).astype(bf16)


def _attn_sample(q, kb, vb, cache_k, cache_v, page_table, layer, lam_p, gsub, lam_init):
    S, Ls, _ = q.shape
    n_pages = page_table.shape[1]
    page_rows = cache_k.shape[2]
    pps = PAGES_PER_STEP
    while n_pages % pps:
        pps -= 1
    steps = n_pages // pps

    def page_spec(p):
        return pl.BlockSpec((None, None, page_rows, HEAD_W), lambda s, j, pt: (layer, pt[s, j * pps + p], 0, 0))

    kern = functools.partial(_attn_sample_kernel, n_pages_step=pps, seq_new=Ls, lam_init=lam_init)
    grid_spec = pltpu.PrefetchScalarGridSpec(
        num_scalar_prefetch=1,
        grid=(S, steps),
        in_specs=[pl.BlockSpec((4, HEAD_DIM), lambda s, j, pt: (0, 0)),
                  pl.BlockSpec((1, HEAD_W), lambda s, j, pt: (0, 0)),
                  pl.BlockSpec((1, Ls, ATTN_W), lambda s, j, pt: (s, 0, 0)),
                  pl.BlockSpec((1, Ls, KV_W), lambda s, j, pt: (s, 0, 0)),
                  pl.BlockSpec((1, Ls, KV_W), lambda s, j, pt: (s, 0, 0))]
        + [page_spec(p) for p in range(pps)] + [page_spec(p) for p in range(pps)],
        out_specs=pl.BlockSpec((1, Ls, ATTN_W), lambda s, j, pt: (s, 0, 0)),
        scratch_shapes=[pltpu.VMEM((N_KV, 4 * Ls, LANES), f32), pltpu.VMEM((N_KV, 4 * Ls, LANES), f32),
                        pltpu.VMEM((N_KV, 4 * Ls, 2 * LANES), f32)],
    )
    return pl.pallas_call(
        kern,
        out_shape=jax.ShapeDtypeStruct((S, Ls, ATTN_W), bf16),
        grid_spec=grid_spec,
        compiler_params=_params(("parallel", "arbitrary")),
        name="attn_sample",
    )(page_table, lam_p, gsub, q, kb, vb, *([cache_k] * pps), *([cache_v] * pps))


def _split3(x):
    hi = x.astype(bf16)
    r1 = x - hi.astype(f32)
    mid = r1.astype(bf16)
    lo = (r1 - mid.astype(f32)).astype(bf16)
    return hi, mid, lo


def _ssd_kernel(*refs, Q, nb):
    for bi in range(nb):
        _ssd_row(bi, *refs, Q=Q)


def _ssd_row(bi, xbc_ref, zs_ref, dt_ref, dtT_ref, conv0_ref, ssm0_ref, cw_ref, cb_ref,
             alog_ref, alogT_ref, dskip_ref, gssm_ref, e_ref,
             y_ref, convo_ref, ssmo_ref, hT_all, cbuf_all, *, Q):
    c = pl.program_id(1)
    hT, cbuf = hT_all.at[bi], cbuf_all.at[bi]
    tail = CONV_W - 1
    base = SUBLANES

    @pl.when(c == 0)
    def _():
        cbuf[base - tail:base, :] = conv0_ref[bi]
        for g in range(SSM_G):
            hT[g] = ssm0_ref[bi, g * SSM_GW:(g + 1) * SSM_GW, :].T

    cbuf[base:base + Q, :] = xbc_ref[bi].astype(f32)
    acc = cb_ref[...] + cw_ref[CONV_W - 1:CONV_W, :] * cbuf[base:base + Q, :]
    for j in range(tail):
        acc = acc + cw_ref[j:j + 1, :] * cbuf[pl.ds(base - tail + j, Q), :]
    xc = _silu(acc)
    convo_ref[bi] = cbuf[pl.ds(base + Q - tail, tail), :]
    cbuf[0:SUBLANES, :] = cbuf[Q:Q + SUBLANES, :]

    xs = xc[:, :D_INNER]
    xs_b = xs.astype(bf16)
    e_mat = e_ref[...]

    dt = dt_ref[bi]
    dtT = dtT_ref[bi]
    a = dt * (-jnp.exp(alog_ref[...]))
    aT = dtT * (-jnp.exp(alogT_ref[...]))
    r_i = lax.broadcasted_iota(i32, (Q, Q), 0)
    c_i = lax.broadcasted_iota(i32, (Q, Q), 1)
    tril = (c_i <= r_i)
    tril_b = tril.astype(bf16)
    triu_b = (r_i <= c_i).astype(bf16)
    acs = sum(jnp.dot(tril_b, part, preferred_element_type=f32) for part in _split3(a))
    acsT = sum(jnp.dot(part, triu_b, preferred_element_type=f32) for part in _split3(aT))
    acs_last = acs[Q - 1:Q, :]
    e_acs = jnp.exp(acs)
    w_end = jnp.exp(acs_last - acs) * dt

    lane = lax.broadcasted_iota(i32, (Q, LANES), 1)
    first_head = lane < SSM_P
    y_cols = []
    new_h = []
    for g in range(SSM_G):
        bg = xc[:, D_INNER + g * D_STATE:D_INNER + (g + 1) * D_STATE].astype(bf16)
        cg = xc[:, D_INNER + SSM_G * D_STATE + g * D_STATE:D_INNER + SSM_G * D_STATE + (g + 1) * D_STATE].astype(bf16)
        cbm = lax.dot_general(cg, bg, (((1,), (1,)), ((), ())), preferred_element_type=f32)
        y_inter = jnp.dot(cg, hT[g].astype(bf16), preferred_element_type=f32)
        pair_cols = []
        for pr in range(SSM_GW // LANES):
            xp = xs_b[:, g * SSM_GW + pr * LANES:g * SSM_GW + (pr + 1) * LANES]
            ys = []
            for hh in range(2):
                h = g * (SSM_GW // SSM_P) + 2 * pr + hh
                seg = acs[:, h:h + 1] - acsT[h:h + 1, :]
                lm = jnp.exp(jnp.where(tril, seg, NEG))
                mm = (cbm * lm * dtT[h:h + 1, :]).astype(bf16)
                ys.append(jnp.dot(mm, xp, preferred_element_type=f32))
            pair_cols.append(jnp.where(first_head, ys[0], ys[1]))
        y_cols.append((jnp.concatenate(pair_cols, axis=1), y_inter))
        new_h.append((bg, g))

    e_acs_x = jnp.dot(e_acs.astype(bf16), e_mat, preferred_element_type=f32)
    w_end_x = jnp.dot(w_end.astype(bf16), e_mat, preferred_element_type=f32)
    xw = (xs * w_end_x).astype(bf16)
    dec8 = jnp.broadcast_to(jnp.exp(acs_last), (SUBLANES, LANES))
    dec_hi = dec8.astype(bf16)
    dec_lo = (dec8 - dec_hi.astype(f32)).astype(bf16)
    dec_x = (jnp.dot(dec_hi, e_mat, preferred_element_type=f32)
             + jnp.dot(dec_lo, e_mat, preferred_element_type=f32))[0:1, :]

    y_parts = []
    for g in range(SSM_G):
        y_intra, y_inter = y_cols[g]
        sl = slice(g * SSM_GW, (g + 1) * SSM_GW)
        yg = y_intra + y_inter * e_acs_x[:, sl] + dskip_ref[:, sl] * xs[:, sl]
        yg = yg * zs_ref[bi][:, sl].astype(f32)
        y_parts.append(_rms(yg, gssm_ref[:, sl], SSM_EPS))
        bg = new_h[g][0]
        upd = lax.dot_general(bg, xw[:, sl], (((0,), (0,)), ((), ())), preferred_element_type=f32)
        hT[g] = hT[g] * dec_x[:, sl] + upd
    y_ref[bi] = jnp.concatenate(y_parts, axis=1).astype(bf16)

    @pl.when(c == pl.num_programs(1) - 1)
    def _():
        for g in range(SSM_G):
            ssmo_ref[bi, g * SSM_GW:(g + 1) * SSM_GW, :] = hT[g].T


def _ssd(xbc, zs, dt, conv0, ssm0, conv_w, conv_b, a_log, d_skip, g_ssm, e_mat):
    B, L, _ = xbc.shape
    Q = min(SSD_CHUNK, L)
    assert L % Q == 0 and Q % SUBLANES == 0
    nc = L // Q
    dtT = jnp.swapaxes(dt[:, :, :SSM_H], 1, 2)
    alog_row = jnp.pad(a_log.reshape(1, SSM_H), ((0, 0), (0, LANES - SSM_H)))
    alog_col = a_log.reshape(SSM_H, 1)
    dskip_x = jnp.repeat(d_skip, SSM_P).reshape(1, D_INNER)
    nb = SSD_SHORT_SEQS_PER_STEP if (nc == 1 and B % SSD_SHORT_SEQS_PER_STEP == 0) else 1
    kern = functools.partial(_ssd_kernel, Q=Q, nb=nb)
    cmap = lambda b, c: (0, 0)
    return pl.pallas_call(
        kern,
        out_shape=(jax.ShapeDtypeStruct((B, L, D_INNER), bf16),
                   jax.ShapeDtypeStruct((B, CONV_W - 1, CONV_DIM), f32),
                   jax.ShapeDtypeStruct((B, D_INNER, D_STATE), f32)),
        grid=(B // nb, nc),
        in_specs=[pl.BlockSpec((nb, Q, CONV_DIM), lambda b, c: (b, c, 0)),
                  pl.BlockSpec((nb, Q, D_INNER), lambda b, c: (b, c, 0)),
                  pl.BlockSpec((nb, Q, LANES), lambda b, c: (b, c, 0)),
                  pl.BlockSpec((nb, SSM_H, Q), lambda b, c: (b, 0, c)),
                  pl.BlockSpec((nb, CONV_W - 1, CONV_DIM), lambda b, c: (b, 0, 0)),
                  pl.BlockSpec((nb, D_INNER, D_STATE), lambda b, c: (b, 0, 0)),
                  pl.BlockSpec((CONV_W, CONV_DIM), cmap), pl.BlockSpec((1, CONV_DIM), cmap),
                  pl.BlockSpec((1, LANES), cmap), pl.BlockSpec((SSM_H, 1), cmap),
                  pl.BlockSpec((1, D_INNER), cmap), pl.BlockSpec((1, D_INNER), cmap),
                  pl.BlockSpec((LANES, D_INNER), cmap)],
        out_specs=(pl.BlockSpec((nb, Q, D_INNER), lambda b, c: (b, c, 0)),
                   pl.BlockSpec((nb, CONV_W - 1, CONV_DIM), lambda b, c: (b, 0, 0)),
                   pl.BlockSpec((nb, D_INNER, D_STATE), lambda b, c: (b, 0, 0))),
        scratch_shapes=[pltpu.VMEM((nb, SSM_G, D_STATE, SSM_GW), f32),
                        pltpu.VMEM((nb, SUBLANES + Q, CONV_DIM), f32)],
        compiler_params=_params(("parallel", "arbitrary")),
        name="ssd_scan",
    )(xbc, zs, dt, dtT, conv0, ssm0, conv_w, conv_b.reshape(1, CONV_DIM), alog_row, alog_col,
      dskip_x, g_ssm.reshape(1, D_INNER), e_mat)


def _route(lg):
    lane = lax.broadcasted_iota(i32, lg.shape, 1)
    lanef = lane.astype(f32)
    big = float(LANES)
    gl = jnp.where(lane < N_EGROUPS, lg, NEG)
    gmax = jnp.max(gl, axis=1, keepdims=True)
    gsel = jnp.min(jnp.where(gl == gmax, lanef, big), axis=1, keepdims=True)
    p_g = 1.0 / jnp.sum(jnp.exp(gl - gmax), axis=1, keepdims=True)
    first = N_EGROUPS + E_PER_GROUP * gsel
    emask = (lanef >= first) & (lanef < first + E_PER_GROUP)
    el = jnp.where(emask, lg, NEG)
    v1 = jnp.max(el, axis=1, keepdims=True)
    i1 = jnp.min(jnp.where(emask & (el == v1), lanef, big), axis=1, keepdims=True)
    emask2 = emask & (lanef != i1)
    el2 = jnp.where(emask2, lg, NEG)
    v2 = jnp.max(el2, axis=1, keepdims=True)
    i2 = jnp.min(jnp.where(emask2 & (el2 == v2), lanef, big), axis=1, keepdims=True)
    t = jnp.exp(v2 - v1)
    w1 = p_g / (1.0 + t)
    w2 = p_g * t / (1.0 + t)
    out = jnp.where(lane == 0, i1 - N_EGROUPS, 0.0)
    out = jnp.where(lane == 1, i2 - N_EGROUPS, out)
    out = jnp.where(lane == 2, w1, out)
    out = jnp.where(lane == 3, w2, out)
    return out


def _to_row_tiles(ref, x, rows, first_row=0):
    for j in range(D_MODEL // LANES):
        ref[pl.ds(first_row * SUBLANES + j, rows, stride=SUBLANES), :] = x[:, j * LANES:(j + 1) * LANES]


def _from_row_tiles(ref, rows):
    return jnp.concatenate([ref[pl.ds(j, rows, stride=SUBLANES), :] for j in range(D_MODEL // LANES)], axis=1)


def _mix_kernel(x_ref, gt1_ref, sc2_ref, sh2_ref, o_ref, ys_ref, ga_ref, gs_ref,
                woa_ref, wos_ref, wo_ref, gpost_ref, gpre_ref, wr_ref, br_ref,
                x1_ref, u2_ref, route_ref):
    tm = x_ref.shape[0]
    n_parts = MIX_ROW_PARTS if tm % (MIX_ROW_PARTS * SUBLANES) == 0 else 1
    h = tm // n_parts

    def rows_of(ref, sl):
        return ref[sl, :] if ref.shape[0] == tm else ref[...]

    for part in range(n_parts):
        sl = slice(part * h, (part + 1) * h)
        ya = jnp.dot(o_ref[sl, :], woa_ref[...], preferred_element_type=f32)
        ys = jnp.dot(ys_ref[sl, :], wos_ref[...], preferred_element_type=f32)
        m = ga_ref[sl, :].astype(f32) * ya + gs_ref[sl, :].astype(f32) * ys
        mo = jnp.dot(m.astype(bf16), wo_ref[...], preferred_element_type=f32)
        x1 = x_ref[sl, :] + rows_of(gt1_ref, sl) * _rms(mo, gpost_ref[...], EPS)
        x1_ref[sl, :] = x1
        u2 = _rms(x1, gpre_ref[...], EPS) * (1.0 + rows_of(sc2_ref, sl)) + rows_of(sh2_ref, sl)
        _to_row_tiles(u2_ref, u2, h, first_row=part * h)
        lg = jnp.dot(u2.astype(bf16), wr_ref[...], preferred_element_type=f32) + br_ref[...]
        route_ref[sl, :] = _route(lg)


def _mix(grp, x2, o, ysn, ga, gs, w_oa, w_os, w_o, g_post1, g_pre2, w_r, b_r):
    T, tm = grp.T, grp.tm
    return pl.pallas_call(
        _mix_kernel,
        out_shape=(jax.ShapeDtypeStruct((T, D_MODEL), f32),
                   jax.ShapeDtypeStruct((T * SUBLANES, LANES), f32),
                   jax.ShapeDtypeStruct((T, LANES), f32)),
        grid=(grp.n_tiles,),
        in_specs=[grp.row_spec(D_MODEL), grp.mod_spec(2), grp.mod_spec(4), grp.mod_spec(3),
                  grp.row_spec(ATTN_W), grp.row_spec(D_INNER), grp.row_spec(D_MODEL), grp.row_spec(D_MODEL),
                  _const_spec(w_oa.shape), _const_spec(w_os.shape), _const_spec(w_o.shape),
                  _const_spec((1, D_MODEL)), _const_spec((1, D_MODEL)),
                  _const_spec(w_r.shape), _const_spec((1, LANES))],
        out_specs=(grp.row_spec(D_MODEL),
                   pl.BlockSpec((tm * SUBLANES, LANES), lambda i: (i, 0)),
                   grp.row_spec(LANES)),
        compiler_params=_params(("parallel",)),
        name="mixer_out",
    )(x2, grp.mod, grp.mod, grp.mod, o, ysn, ga, gs, w_oa, w_os, w_o, g_post1, g_pre2, w_r, b_r)


def _count_kernel(route_ref, cnt_ref):
    i = pl.program_id(0)

    @pl.when(i == 0)
    def _():
        cnt_ref[...] = jnp.zeros(cnt_ref.shape, f32)

    r = route_ref[...]
    lanef = lax.broadcasted_iota(i32, r.shape, 1).astype(f32)
    hits = (lanef == r[:, 0:1]).astype(f32) + (lanef == r[:, 1:2]).astype(f32)
    cnt_ref[...] += jnp.sum(hits, axis=0, keepdims=True)


def _expert_counts(route, tm):
    T = route.shape[0]
    return pl.pallas_call(
        _count_kernel,
        out_shape=jax.ShapeDtypeStruct((1, LANES), f32),
        grid=(T // tm,),
        in_specs=[pl.BlockSpec((tm, LANES), lambda i: (i, 0))],
        out_specs=pl.BlockSpec((1, LANES), lambda i: (0, 0)),
        compiler_params=_params(("arbitrary",)),
        name="moe_count",
    )(route)


def _slot_kernel(route_ref, start_ref, dest_ref, carry):
    i = pl.program_id(0)

    @pl.when(i == 0)
    def _():
        carry[...] = jnp.broadcast_to(start_ref[...], carry.shape)

    r = route_ref[...]
    tm = r.shape[0]
    lane = lax.broadcasted_iota(i32, r.shape, 1)
    lanef = lane.astype(f32)
    oh1 = (lanef == r[:, 0:1]).astype(f32)
    oh2 = (lanef == r[:, 1:2]).astype(f32)
    cnt = oh1 + oh2
    ri = lax.broadcasted_iota(i32, (tm, tm), 0)
    ci = lax.broadcasted_iota(i32, (tm, tm), 1)
    before = (ci < ri).astype(bf16)
    pos = jnp.dot(before, cnt.astype(bf16), preferred_element_type=f32) + carry[0:1, :]
    d1 = jnp.sum(oh1 * pos, axis=1, keepdims=True)
    d2 = jnp.sum(oh2 * pos, axis=1, keepdims=True)
    dest_ref[...] = jnp.where(lane == 0, d1, jnp.where(lane == 1, d2, 0.0)).astype(i32)
    carry[...] = carry[...] + jnp.sum(cnt, axis=0, keepdims=True)


def _slots(route, starts, tm):
    T = route.shape[0]
    return pl.pallas_call(
        _slot_kernel,
        out_shape=jax.ShapeDtypeStruct((T, LANES), i32),
        grid=(T // tm,),
        in_specs=[pl.BlockSpec((tm, LANES), lambda i: (i, 0)), pl.BlockSpec((1, LANES), lambda i: (0, 0))],
        out_specs=pl.BlockSpec((tm, LANES), lambda i: (i, 0)),
        scratch_shapes=[pltpu.VMEM((SUBLANES, LANES), f32)],
        compiler_params=_params(("arbitrary",)),
        name="moe_slots",
    )(route, starts)


def _row_copy(src, src_row, dst, dst_row, sem):
    return pltpu.make_async_copy(src.at[pl.ds(pl.multiple_of(src_row * SUBLANES, SUBLANES), SUBLANES), :],
                                 dst.at[pl.ds(pl.multiple_of(dst_row * SUBLANES, SUBLANES), SUBLANES), :], sem)


def _dispatch_kernel(dest_ref, u_ref, xs_in_ref, xs_ref, sem, *, tm):
    del xs_in_ref

    def issue(t, c):
        for k in range(2):
            _row_copy(u_ref, t, xs_ref, dest_ref[0, 0, 2 * t + k], sem).start(priority=k)
        return c

    lax.fori_loop(0, tm, issue, 0)
    rows = 2 * tm * SUBLANES
    pltpu.make_async_copy(xs_ref.at[pl.ds(0, rows), :], xs_ref.at[pl.ds(0, rows), :], sem).wait()


def _dispatch(dest2, u_rows, n_slots, tm):
    T = dest2.shape[0]
    xs0 = jnp.zeros((n_slots * SUBLANES, LANES), f32)
    dest3 = dest2.reshape(T // tm, 1, 2 * tm)
    return pl.pallas_call(
        functools.partial(_dispatch_kernel, tm=tm),
        out_shape=jax.ShapeDtypeStruct((n_slots * SUBLANES, LANES), f32),
        grid=(T // tm,),
        in_specs=[pl.BlockSpec((1, 1, 2 * tm), lambda i: (i, 0, 0), memory_space=pltpu.SMEM),
                  pl.BlockSpec((tm * SUBLANES, LANES), lambda i: (i, 0)), pl.BlockSpec(memory_space=pl.ANY)],
        out_specs=pl.BlockSpec(memory_space=pl.ANY),
        scratch_shapes=[pltpu.SemaphoreType.DMA(())],
        input_output_aliases={2: 0},
        compiler_params=_params(("arbitrary",)),
        name="moe_dispatch",
    )(dest3, u_rows, xs0)


def _expert_kernel(be_ref, xs_ref, wg_ref, wu_ref, wd_ref, o_ref, *, blk):
    del be_ref
    x = _from_row_tiles(xs_ref, blk).astype(bf16)
    h = _silu(jnp.dot(x, wg_ref[...], preferred_element_type=f32)) * jnp.dot(x, wu_ref[...], preferred_element_type=f32)
    _to_row_tiles(o_ref, jnp.dot(h.astype(bf16), wd_ref[...], preferred_element_type=f32), blk)


def _experts(xs, blk_e, w_gate, w_up, w_down, blk):
    n_blocks = blk_e.shape[0]
    rows = blk * SUBLANES
    grid_spec = pltpu.PrefetchScalarGridSpec(
        num_scalar_prefetch=1,
        grid=(n_blocks,),
        in_specs=[pl.BlockSpec((rows, LANES), lambda i, be: (i, 0)),
                  pl.BlockSpec((None, D_MODEL, D_FF), lambda i, be: (be[i], 0, 0)),
                  pl.BlockSpec((None, D_MODEL, D_FF), lambda i, be: (be[i], 0, 0)),
                  pl.BlockSpec((None, D_FF, D_MODEL), lambda i, be: (be[i], 0, 0))],
        out_specs=pl.BlockSpec((rows, LANES), lambda i, be: (i, 0)),
    )
    return pl.pallas_call(
        functools.partial(_expert_kernel, blk=blk),
        out_shape=jax.ShapeDtypeStruct(xs.shape, f32),
        grid_spec=grid_spec,
        compiler_params=_params(("parallel",)),
        name="moe_experts",
    )(blk_e, xs, w_gate, w_up, w_down)


def _combine_kernel(dest_ref, dest_next_ref, outs_ref, x1_ref, gt2_ref, route_ref, gpost_ref, y_ref, buf, sem, *, tm):
    i = pl.program_id(0)
    slot = jnp.bitwise_and(i, 1)

    def gather(d_ref, s):
        def issue(t, c):
            for k in range(2):
                _row_copy(outs_ref, d_ref[0, 0, 2 * t + k], buf.at[s, k], t, sem.at[s]).start(priority=k)
            return c
        lax.fori_loop(0, tm, issue, 0)

    @pl.when(i == 0)
    def _():
        gather(dest_ref, 0)

    @pl.when(i + 1 < pl.num_programs(0))
    def _():
        gather(dest_next_ref, 1 - slot)

    pltpu.make_async_copy(buf.at[slot], buf.at[slot], sem.at[slot]).wait()
    r = route_ref[...]
    f = r[:, 2:3] * _from_row_tiles(buf.at[slot, 0], tm) + r[:, 3:4] * _from_row_tiles(buf.at[slot, 1], tm)
    y_ref[...] = x1_ref[...] + gt2_ref[...] * _rms(f, gpost_ref[...], EPS)


def _combine(grp, dest2, outs, x1, route, g_post2, tm):
    T = grp.T
    n = T // tm
    dest3 = dest2.reshape(n, 1, 2 * tm)
    return pl.pallas_call(
        functools.partial(_combine_kernel, tm=tm),
        out_shape=jax.ShapeDtypeStruct((T, D_MODEL), f32),
        grid=(n,),
        in_specs=[pl.BlockSpec((1, 1, 2 * tm), lambda i: (i, 0, 0), memory_space=pltpu.SMEM),
                  pl.BlockSpec((1, 1, 2 * tm), lambda i: (jnp.minimum(i + 1, n - 1), 0, 0), memory_space=pltpu.SMEM),
                  pl.BlockSpec(memory_space=pl.ANY),
                  grp.row_spec(D_MODEL), grp.mod_spec(5), grp.row_spec(LANES), _const_spec((1, D_MODEL))],
        out_specs=grp.row_spec(D_MODEL),
        scratch_shapes=[pltpu.VMEM((2, 2, tm * SUBLANES, LANES), f32), pltpu.SemaphoreType.DMA((2,))],
        compiler_params=_params(("arbitrary",)),
        name="moe_combine",
    )(dest3, dest3, outs, x1, grp.mod, route, g_post2)


def _hmoe_and_residual(grp, x1, u_rows, route, w_gate, w_up, w_down, g_post2):
    T = grp.T
    tm = grp.tm
    blk = MOE_BLOCK if 2 * T >= N_EXPERTS * MOE_BLOCK else MOE_BLOCK_SMALL
    counts = _expert_counts(route, tm)[0, :N_EXPERTS].astype(i32)
    padded = (counts + blk - 1) // blk * blk
    ends = jnp.cumsum(padded)
    starts = ends - padded
    n_blocks = -(-2 * T // blk) + N_EXPERTS
    n_slots = n_blocks * blk
    blk_start = jnp.arange(n_blocks, dtype=i32) * blk
    blk_e = jnp.minimum(jnp.sum(blk_start[:, None] >= ends[None, :], axis=1), N_EXPERTS - 1).astype(i32)
    starts_row = jnp.pad(starts.astype(f32), (0, LANES - N_EXPERTS)).reshape(1, LANES)
    dest2 = _slots(route, starts_row, tm)[:, :2]
    xs = _dispatch(dest2, u_rows, n_slots, tm)
    outs = _experts(xs, blk_e, w_gate, w_up, w_down, blk)
    return _combine(grp, dest2, outs, x1, route, g_post2, tm)


def _rope_tables(pos):
    half = HEAD_DIM // 2
    inv = ROPE_THETA ** (-2.0 * jnp.arange(half, dtype=f32) / HEAD_DIM)
    ang = pos.astype(f32)[:, None] * inv[None, :]
    cos, sin = jnp.cos(ang), jnp.sin(ang)
    return jnp.tile(cos, (1, LANES // half)), jnp.tile(jnp.concatenate([-sin, sin], axis=1), (1, LANES // HEAD_DIM))


def _layer(grp, x2, lw, lam_init, attend, conv0, ssm0):
    n_seq, seq_len = grp.n_seq, grp.seq_len
    q, k, kb, v, vb = _proj_qkv(grp, x2, lw["g_pre1"], lw["w_qkv"], 1, 0)
    zs, xbc = _proj_zx(grp, x2, lw["g_pre1"], lw["w_z"], lw["w_xbc"], 1, 0)
    ga, gs, dt = _proj_gates(grp, x2, lw["g_pre1"], lw["w_gates"], lw["w_dt"], lw["dt_bias"], 1, 0)
    shp = lambda a: a.reshape(n_seq, seq_len, a.shape[-1])
    o = attend(shp(q), shp(kb), shp(vb))
    ysn, conv_new, ssm_new = _ssd(shp(xbc), shp(zs), shp(dt), conv0, ssm0.reshape(n_seq, D_INNER, D_STATE),
                                  lw["conv_w"], lw["conv_b"], lw["a_log"], lw["d_skip"], lw["g_ssm"], lw["e_mat"])
    x1, u_rows, route = _mix(grp, x2, o.reshape(grp.T, ATTN_W), ysn.reshape(grp.T, D_INNER), ga, gs,
                             lw["w_oa"], lw["w_os"], lw["w_o"], lw["g_post1"], lw["g_pre2"], lw["w_r"], lw["b_r"])
    y2 = _hmoe_and_residual(grp, x1, u_rows, route, lw["w_gate"], lw["w_up"], lw["w_down"], lw["g_post2"])
    return (y2, k.reshape(n_seq, seq_len, N_KV, HEAD_W), v.reshape(n_seq, seq_len, N_KV, HEAD_W), conv_new,
            ssm_new.reshape(n_seq, SSM_H, SSM_P, D_STATE))


def kernel(x_prompt, x_sample, cache_k, cache_v, state_conv, state_ssm, page_table, c_prompt, c_sample, w_ada, b_ada, g_pre1, g_post1, g_pre2, g_post2, w_in, lambda_q1, lambda_k1, lambda_q2, lambda_k2, g_subln, w_oa, conv_w, conv_b, dt_bias, a_log, d_skip, g_ssm, w_os, w_o, w_rg, b_rg, w_re, b_re, w_gate, w_up, w_down):
    depth = w_in.shape[0]
    bp, lp_len, _ = x_prompt.shape
    bs, ls_len, _ = x_sample.shape
    past = page_table.shape[1] * cache_k.shape[2]
    ck = cache_k.reshape(cache_k.shape[0], cache_k.shape[1], cache_k.shape[2] * N_KV, HEAD_W)
    cv = cache_v.reshape(cache_v.shape[0], cache_v.shape[1], cache_v.shape[2] * N_KV, HEAD_W)
    rope_p = _rope_tables(jnp.arange(lp_len, dtype=i32))
    rope_s8 = _rope_tables(past + jnp.arange(ls_len, dtype=i32))
    rope_s = tuple(jnp.tile(t, (bs, 1)) for t in rope_s8)
    e_mat = (jnp.arange(LANES, dtype=i32)[:, None] == jnp.arange(D_INNER, dtype=i32)[None, :] // SSM_P).astype(bf16)

    hp = x_prompt.reshape(bp * lp_len, D_MODEL)
    hs = x_sample.reshape(bs * ls_len, D_MODEL)
    c_all = jnp.concatenate([c_prompt, c_sample], axis=0)
    outs = [[] for _ in range(8)]
    for l in range(depth):
        lam_init = 0.8 - 0.6 * math.exp(-0.3 * l)
        w_in_l = w_in[l].astype(bf16)
        lw = dict(
            w_qkv=w_in_l[:, :OFF_Z], w_z=w_in_l[:, OFF_Z:OFF_XBC], w_xbc=w_in_l[:, OFF_XBC:OFF_DT],
            w_dt=jnp.pad(w_in_l[:, OFF_DT:OFF_GA], ((0, 0), (0, LANES - SSM_H))), w_gates=w_in_l[:, OFF_GA:],
            dt_bias=jnp.pad(dt_bias[l].reshape(1, SSM_H), ((0, 0), (0, LANES - SSM_H))),
            g_pre1=g_pre1[l].reshape(1, D_MODEL), g_post1=g_post1[l].reshape(1, D_MODEL),
            g_pre2=g_pre2[l].reshape(1, D_MODEL), g_post2=g_post2[l].reshape(1, D_MODEL),
            w_oa=w_oa[l].astype(bf16), w_os=w_os[l].astype(bf16), w_o=w_o[l].astype(bf16),
            conv_w=conv_w[l], conv_b=conv_b[l], a_log=a_log[l], d_skip=d_skip[l], g_ssm=g_ssm[l], e_mat=e_mat,
            w_r=jnp.pad(jnp.concatenate([w_rg[l], w_re[l]], axis=1),
                        ((0, 0), (0, LANES - N_EGROUPS - N_EXPERTS))).astype(bf16),
            b_r=jnp.pad(jnp.concatenate([b_rg[l], b_re[l]]), (0, LANES - N_EGROUPS - N_EXPERTS)).reshape(1, LANES),
            w_gate=w_gate[l].astype(bf16), w_up=w_up[l].astype(bf16), w_down=w_down[l].astype(bf16),
        )
        lam_p = jnp.stack([lambda_q1[l], lambda_k1[l], lambda_q2[l], lambda_k2[l]])
        gsub = g_subln[l].reshape(1, HEAD_W)
        mod = _ada(c_all, w_ada[l].astype(bf16), b_ada[l])
        grp_p = _Group(bp, lp_len, mod[:bp], rope_p, per_token=False)
        grp_s = _Group(bs, ls_len, jnp.repeat(mod[bp:], ls_len, axis=0), rope_s, per_token=True)

        attend_p = lambda q, kb, vb: _attn_prompt(q, kb, vb, lam_p, gsub, lam_init)
        attend_s = lambda q, kb, vb: _attn_sample(q, kb, vb, ck, cv, page_table, l, lam_p, gsub, lam_init)
        conv0 = jnp.zeros((bp, CONV_W - 1, CONV_DIM), f32)
        ssm0 = jnp.zeros((bp, SSM_H, SSM_P, D_STATE), f32)
        hp, kp, vp, cp, sp = _layer(grp_p, hp, lw, lam_init, attend_p, conv0, ssm0)
        hs, ksm, vsm, csm, ssm = _layer(grp_s, hs, lw, lam_init, attend_s, state_conv[l], state_ssm[l])
        for lst, val in zip(outs, (kp, vp, cp, sp, ksm, vsm, csm, ssm)):
            lst.append(val)
    stacked = [jnp.stack(o) for o in outs]
    return (hp.reshape(bp, lp_len, D_MODEL), hs.reshape(bs, ls_len, D_MODEL), *stacked)
```

```python
import functools
import math

import jax
import jax.numpy as jnp
from jax import lax
from jax.experimental import pallas as pl
from jax.experimental.pallas import tpu as pltpu

f32 = jnp.float32
bf16 = jnp.bfloat16
i32 = jnp.int32

D_MODEL = 1024
N_HEADS = 8
N_KV = 4
HEAD_DIM = 64
HEAD_W = 2 * HEAD_DIM
ATTN_W = N_HEADS * HEAD_W
KV_W = N_KV * HEAD_W
ATTN_SCALE = HEAD_DIM ** -0.5
ROPE_THETA = 10000.0
SUBLN_EPS = 1e-5
D_INNER = 2048
SSM_P = 64
SSM_H = D_INNER // SSM_P
SSM_G = 4
SSM_GW = D_INNER // SSM_G
D_STATE = 128
CONV_W = 4
CONV_DIM = D_INNER + 2 * SSM_G * D_STATE
SSD_CHUNK = 128
SSM_EPS = 1e-5
N_EGROUPS = 4
E_PER_GROUP = 8
N_EXPERTS = 32
D_FF = 512
EPS = 1e-6
OFF_K = ATTN_W
OFF_V = OFF_K + KV_W
OFF_Z = OFF_V + KV_W
OFF_XBC = OFF_Z + D_INNER
OFF_DT = OFF_XBC + CONV_DIM
OFF_GA = OFF_DT + SSM_H
OFF_GS = OFF_GA + D_MODEL

LANES = 128
SUBLANES = 8
VMEM_LIMIT = 56 * 1024 * 1024
NEG = -1e30
LOG2E = math.log2(math.e)

TOKEN_TILE = 512
ATTN_TILE = 512
PAGES_PER_STEP = 32
MOE_BLOCK = 512
MOE_BLOCK_SMALL = 128
SSD_SHORT_SEQS_PER_STEP = 4
MIX_ROW_PARTS = 1


def _params(sem, vmem=VMEM_LIMIT):
    return pltpu.CompilerParams(dimension_semantics=sem, vmem_limit_bytes=vmem)


def _sigmoid(x):
    return 1.0 / (1.0 + jnp.exp(-x))


def _silu(x):
    return x * _sigmoid(x)


def _rms(x, g, eps):
    return x * lax.rsqrt(jnp.mean(x * x, axis=-1, keepdims=True) + eps) * g


def _bdot(a, b):
    return jnp.dot(a.astype(bf16), b.astype(bf16), preferred_element_type=f32)


class _Group:
    def __init__(self, n_seq, seq_len, mod_rows, rope_rows, per_token):
        self.n_seq, self.seq_len = n_seq, seq_len
        self.T = n_seq * seq_len
        self.tm = min(TOKEN_TILE, self.T if per_token else seq_len)
        assert self.T % self.tm == 0
        self.per_token = per_token
        if per_token:
            self.mod = mod_rows
            self.rope = rope_rows
        else:
            assert seq_len % self.tm == 0
            self.tiles_per_seq = seq_len // self.tm
            self.mod = mod_rows.reshape(n_seq, 1, 6 * D_MODEL)
            self.rope = rope_rows
        self.n_tiles = self.T // self.tm

    def mod_spec(self, col):
        if self.per_token:
            return pl.BlockSpec((self.tm, D_MODEL), lambda i: (i, col))
        tps = self.tiles_per_seq
        return pl.BlockSpec((None, 1, D_MODEL), lambda i: (i // tps, 0, col))

    def rope_spec(self):
        if self.per_token:
            return pl.BlockSpec((self.tm, LANES), lambda i: (i, 0))
        tps = self.tiles_per_seq
        return pl.BlockSpec((self.tm, LANES), lambda i: (i % tps, 0))

    def row_spec(self, width):
        return pl.BlockSpec((self.tm, width), lambda i: (i, 0))


def _const_spec(shape):
    nd = len(shape)
    return pl.BlockSpec(shape, lambda i: (0,) * nd)


def _ada_kernel(c_ref, w_ref, b_ref, o_ref):
    o_ref[...] = _bdot(_silu(c_ref[...]), w_ref[...]) + b_ref[...]


def _ada(c, w_ada, b_ada):
    m = c.shape[0]
    n = w_ada.shape[1]
    tn = D_MODEL
    return pl.pallas_call(
        _ada_kernel,
        out_shape=jax.ShapeDtypeStruct((m, n), f32),
        grid=(n // tn,),
        in_specs=[pl.BlockSpec((m, D_MODEL), lambda j: (0, 0)),
                  pl.BlockSpec((D_MODEL, tn), lambda j: (0, j)),
                  pl.BlockSpec((1, tn), lambda j: (0, j))],
        out_specs=pl.BlockSpec((m, tn), lambda j: (0, j)),
        compiler_params=_params(("parallel",)),
        name="ada_mod",
    )(c, w_ada, b_ada.reshape(1, n))


def _modulated(x_ref, g_ref, sc_ref, sh_ref):
    u = _rms(x_ref[...], g_ref[...], EPS) * (1.0 + sc_ref[...]) + sh_ref[...]
    return u.astype(bf16)


def _rope_cols(x, n_blocks, cos, sin_signed, lo):
    out = []
    for j in range(n_blocks):
        xb = x[:, j * LANES:(j + 1) * LANES]
        partner = jnp.where(lo, pltpu.roll(xb, LANES - HEAD_DIM // 2, 1), pltpu.roll(xb, HEAD_DIM // 2, 1))
        out.append(xb * cos + partner * sin_signed)
    return jnp.concatenate(out, axis=1)


def _qkv_kernel(x_ref, g_ref, sc_ref, sh_ref, w_ref, cos_ref, sin_ref,
                q_ref, k_ref, kb_ref, v_ref, vb_ref):
    u = _modulated(x_ref, g_ref, sc_ref, sh_ref)
    proj = jnp.dot(u, w_ref[...], preferred_element_type=f32)
    cos, sin_signed = cos_ref[...], sin_ref[...]
    lane = lax.broadcasted_iota(i32, cos.shape, 1)
    lo = jnp.bitwise_and(lane, HEAD_DIM - 1) < HEAD_DIM // 2
    q = _rope_cols(proj[:, :OFF_K], N_HEADS, cos, sin_signed, lo)
    k = _rope_cols(proj[:, OFF_K:OFF_V], N_KV, cos, sin_signed, lo)
    v = proj[:, OFF_V:OFF_Z]
    q_ref[...] = (q * (ATTN_SCALE * LOG2E)).astype(bf16)
    kb_ref[...] = k.astype(bf16)
    vb_ref[...] = v.astype(bf16)
    tm = k.shape[0]
    for h in range(N_KV):
        k_ref[pl.ds(h, tm, stride=N_KV), :] = k[:, h * HEAD_W:(h + 1) * HEAD_W]
        v_ref[pl.ds(h, tm, stride=N_KV), :] = v[:, h * HEAD_W:(h + 1) * HEAD_W]


def _proj_qkv(grp, x2, g, w_qkv, col_scale, col_shift):
    T, tm = grp.T, grp.tm
    head_rows = pl.BlockSpec((tm * N_KV, HEAD_W), lambda i: (i, 0))
    return pl.pallas_call(
        _qkv_kernel,
        out_shape=(jax.ShapeDtypeStruct((T, ATTN_W), bf16),
                   jax.ShapeDtypeStruct((T * N_KV, HEAD_W), f32), jax.ShapeDtypeStruct((T, KV_W), bf16),
                   jax.ShapeDtypeStruct((T * N_KV, HEAD_W), f32), jax.ShapeDtypeStruct((T, KV_W), bf16)),
        grid=(grp.n_tiles,),
        in_specs=[grp.row_spec(D_MODEL), _const_spec((1, D_MODEL)),
                  grp.mod_spec(col_scale), grp.mod_spec(col_shift),
                  _const_spec(w_qkv.shape), grp.rope_spec(), grp.rope_spec()],
        out_specs=(grp.row_spec(ATTN_W), head_rows, grp.row_spec(KV_W),
                   head_rows, grp.row_spec(KV_W)),
        compiler_params=_params(("parallel",)),
        name="proj_qkv",
    )(x2, g, grp.mod, grp.mod, w_qkv, grp.rope[0], grp.rope[1])


def _zx_kernel(x_ref, g_ref, sc_ref, sh_ref, wz_ref, wx_ref, zs_ref, xbc_ref):
    u = _modulated(x_ref, g_ref, sc_ref, sh_ref)
    zs_ref[...] = _silu(jnp.dot(u, wz_ref[...], preferred_element_type=f32)).astype(bf16)
    xbc_ref[...] = jnp.dot(u, wx_ref[...], preferred_element_type=f32).astype(bf16)


def _proj_zx(grp, x2, g, w_z, w_xbc, col_scale, col_shift):
    T = grp.T
    return pl.pallas_call(
        _zx_kernel,
        out_shape=(jax.ShapeDtypeStruct((T, D_INNER), bf16), jax.ShapeDtypeStruct((T, CONV_DIM), bf16)),
        grid=(grp.n_tiles,),
        in_specs=[grp.row_spec(D_MODEL), _const_spec((1, D_MODEL)),
                  grp.mod_spec(col_scale), grp.mod_spec(col_shift),
                  _const_spec(w_z.shape), _const_spec(w_xbc.shape)],
        out_specs=(grp.row_spec(D_INNER), grp.row_spec(CONV_DIM)),
        compiler_params=_params(("parallel",)),
        name="proj_zx",
    )(x2, g, grp.mod, grp.mod, w_z, w_xbc)


def _gates_kernel(x_ref, g_ref, sc_ref, sh_ref, wg_ref, wdt_ref, dtb_ref, ga_ref, gs_ref, dt_ref):
    u = _modulated(x_ref, g_ref, sc_ref, sh_ref)
    gates = _sigmoid(jnp.dot(u, wg_ref[...], preferred_element_type=f32))
    ga_ref[...] = gates[:, :D_MODEL].astype(bf16)
    gs_ref[...] = gates[:, D_MODEL:].astype(bf16)
    raw = jnp.dot(u, wdt_ref[...], preferred_element_type=f32) + dtb_ref[...]
    dt_ref[...] = jnp.maximum(raw, 0.0) + jnp.log1p(jnp.exp(-jnp.abs(raw)))


def _proj_gates(grp, x2, g, w_gates, w_dt, dt_bias, col_scale, col_shift):
    T = grp.T
    return pl.pallas_call(
        _gates_kernel,
        out_shape=(jax.ShapeDtypeStruct((T, D_MODEL), bf16), jax.ShapeDtypeStruct((T, D_MODEL), bf16),
                   jax.ShapeDtypeStruct((T, LANES), f32)),
        grid=(grp.n_tiles,),
        in_specs=[grp.row_spec(D_MODEL), _const_spec((1, D_MODEL)),
                  grp.mod_spec(col_scale), grp.mod_spec(col_shift),
                  _const_spec(w_gates.shape), _const_spec(w_dt.shape), _const_spec((1, LANES))],
        out_specs=(grp.row_spec(D_MODEL), grp.row_spec(D_MODEL), grp.row_spec(LANES)),
        compiler_params=_params(("parallel",)),
        name="proj_gates",
    )(x2, g, grp.mod, grp.mod, w_gates, w_dt, dt_bias)


def _lambda_full(lam_ref, lam_init):
    lp = lam_ref[...]
    l1 = jnp.sum(lp[0:1] * lp[1:2], axis=1, keepdims=True)
    l2 = jnp.sum(lp[2:3] * lp[3:4], axis=1, keepdims=True)
    return jnp.exp(l1) - jnp.exp(l2) + lam_init


def _with_ones(v):
    return jnp.concatenate([v, jnp.ones(v.shape, bf16)], axis=1)


def _softmax_step(s, v1, m_ref, acc_ref):
    reps = s.shape[1] // LANES if s.shape[1] % LANES == 0 else 0
    m_prev = m_ref[...]
    m_new = jnp.maximum(m_prev, jnp.max(s, axis=1, keepdims=True))
    alpha = jnp.exp2(m_prev - m_new)
    if reps:
        p = jnp.exp2(s - jnp.concatenate([m_new] * reps, axis=1))
    else:
        p = jnp.exp2(s - m_new[:, 0:1])
    acc_ref[...] = (jnp.concatenate([alpha, alpha], axis=1) * acc_ref[...]
                    + jnp.dot(p.astype(bf16), v1, preferred_element_type=f32))
    m_ref[...] = m_new


def _diff_heads(acc, lam, gsub, lam_init, rows):
    outs = []
    for r in range(2):
        a0 = acc[(2 * r) * rows:(2 * r + 1) * rows]
        a1 = acc[(2 * r + 1) * rows:(2 * r + 2) * rows]
        o = a0[:, :HEAD_W] / a0[:, HEAD_W:] - lam * (a1[:, :HEAD_W] / a1[:, HEAD_W:])
        outs.append(_rms(o, gsub, SUBLN_EPS) * (1.0 - lam_init))
    return jnp.concatenate(outs, axis=1)


def _pad_components(q, rows):
    lane = lax.broadcasted_iota(i32, (rows, LANES), 1)
    lo = lane < HEAD_DIM
    blocks = []
    for r in range(2):
        qh = q[:, r * LANES:(r + 1) * LANES].astype(f32)
        blocks.append(jnp.where(lo, qh, 0.0))
        blocks.append(jnp.where(lo, 0.0, qh))
    return jnp.concatenate(blocks, axis=0)


def _stacked_causal(n, copies=4):
    assert n & (n - 1) == 0
    row = jnp.bitwise_and(lax.broadcasted_iota(i32, (copies * n, n), 0), n - 1)
    col = lax.broadcasted_iota(i32, (copies * n, n), 1)
    return col <= row


def _attn_prompt_kernel(lam_ref, gsub_ref, q_ref, k_ref, v_ref, o_ref, qpad, m_s, acc_s, s_a, s_b, *, tq, lam_init):
    i = pl.program_id(2)
    qpad[...] = _pad_components(q_ref[0], tq).astype(bf16)
    m_s[...] = jnp.full(m_s.shape, NEG, f32)
    acc_s[...] = jnp.zeros(acc_s.shape, f32)

    def scores(c):
        kc = k_ref[0, pl.ds(pl.multiple_of(c * tq, tq), tq), :]
        return lax.dot_general(qpad[...], kc, (((1,), (1,)), ((), ())), preferred_element_type=f32)

    def consume(s_ref, c):
        _softmax_step(s_ref[...], _with_ones(v_ref[0, pl.ds(pl.multiple_of(c * tq, tq), tq), :]), m_s, acc_s)

    s_a[...] = jnp.where(_stacked_causal(tq), scores(i), NEG)

    def pair(p, has_next):
        s_b[...] = scores(2 * p)
        consume(s_a, jnp.where(p == 0, i, 2 * p - 1))
        if has_next:
            s_a[...] = scores(2 * p + 1)
        consume(s_b, 2 * p)

    def two_pairs(t, carry):
        pair(2 * t, True)
        pair(2 * t + 1, True)
        return carry

    odd = jnp.bitwise_and(i, 1)
    n_full = jnp.right_shift(i + 1, 1) - odd
    lax.fori_loop(0, jnp.right_shift(n_full, 1), two_pairs, 0)

    @pl.when(jnp.bitwise_and(n_full, 1) == 1)
    def _():
        pair(n_full - 1, True)

    @pl.when(odd == 1)
    def _():
        pair(n_full, False)

    @pl.when(odd == 0)
    def _():
        consume(s_a, jnp.maximum(i - 1, 0))

    lam = _lambda_full(lam_ref, lam_init)
    o_ref[0] = _diff_heads(acc_s[...], lam, gsub_ref[...], lam_init, tq).astype(bf16)


def _attn_prompt(q, kb, vb, lam_p, gsub, lam_init):
    B, L, _ = q.shape
    tq = min(ATTN_TILE, L)
    assert L % tq == 0
    kern = functools.partial(_attn_prompt_kernel, tq=tq, lam_init=lam_init)
    return pl.pallas_call(
        kern,
        out_shape=jax.ShapeDtypeStruct((B, L, ATTN_W), bf16),
        grid=(B, N_KV, L // tq),
        in_specs=[pl.BlockSpec((4, HEAD_DIM), lambda b, g, i: (0, 0)),
                  pl.BlockSpec((1, HEAD_W), lambda b, g, i: (0, 0)),
                  pl.BlockSpec((1, tq, 2 * HEAD_W), lambda b, g, i: (b, i, g)),
                  pl.BlockSpec((1, L, HEAD_W), lambda b, g, i: (b, 0, g)),
                  pl.BlockSpec((1, L, HEAD_W), lambda b, g, i: (b, 0, g))],
        out_specs=pl.BlockSpec((1, tq, 2 * HEAD_W), lambda b, g, i: (b, i, g)),
        scratch_shapes=[pltpu.VMEM((4 * tq, LANES), bf16), pltpu.VMEM((4 * tq, LANES), f32),
                        pltpu.VMEM((4 * tq, 2 * LANES), f32),
                        pltpu.VMEM((4 * tq, tq), f32), pltpu.VMEM((4 * tq, tq), f32)],
        compiler_params=_params(("parallel", "parallel", "arbitrary")),
        name="attn_prompt",
    )(lam_p, gsub, q, kb, vb)


def _attn_sample_kernel(pt_ref, lam_ref, gsub_ref, q_ref, kn_ref, vn_ref, *rest, n_pages_step, seq_new, lam_init):
    del pt_ref
    kp = rest[:n_pages_step]
    vp = rest[n_pages_step:2 * n_pages_step]
    o_ref, qpad, m_s, acc_s = rest[2 * n_pages_step:]
    j = pl.program_id(1)

    @pl.when(j == 0)
    def _():
        q = q_ref[0]
        for g in range(N_KV):
            qpad[g] = _pad_components(q[:, g * 2 * HEAD_W:(g + 1) * 2 * HEAD_W], seq_new)
        m_s[...] = jnp.full(m_s.shape, NEG, f32)
        acc_s[...] = jnp.zeros(acc_s.shape, f32)

    page = kp[0].shape[0] // N_KV
    for g in range(N_KV):
        qg = qpad[g].astype(bf16)
        kg = jnp.concatenate([r[pl.ds(g, page, stride=N_KV), :].astype(bf16) for r in kp], axis=0)
        vg = jnp.concatenate([r[pl.ds(g, page, stride=N_KV), :].astype(bf16) for r in vp], axis=0)
        s = lax.dot_general(qg, kg, (((1,), (1,)), ((), ())), preferred_element_type=f32)
        _softmax_step(s, _with_ones(vg), m_s.at[g], acc_s.at[g])

    @pl.when(j == pl.num_programs(1) - 1)
    def _():
        lam = _lambda_full(lam_ref, lam_init)
        causal = _stacked_causal(seq_new)
        outs = []
        for g in range(N_KV):
            qg = qpad[g].astype(bf16)
            kg = kn_ref[0][:, g * HEAD_W:(g + 1) * HEAD_W]
            vg = vn_ref[0][:, g * HEAD_W:(g + 1) * HEAD_W]
            s = lax.dot_general(qg, kg, (((1,), (1,)), ((), ())), preferred_element_type=f32)
            _softmax_step(jnp.where(causal, s, NEG), _with_ones(vg), m_s.at[g], acc_s.at[g])
            outs.append(_diff_heads(acc_s[g], lam, gsub_ref[...], lam_init, seq_new))
        o_ref[0] = jnp.concatenate(outs, axis=1).astype(bf16)


def _attn_sample(q, kb, vb, cache_k, cache_v, page_table, layer, lam_p, gsub, lam_init):
    S, Ls, _ = q.shape
    n_pages = page_table.shape[1]
    page_rows = cache_k.shape[2]
    pps = PAGES_PER_STEP
    while n_pages % pps:
        pps -= 1
    steps = n_pages // pps

    def page_spec(p):
        return pl.BlockSpec((None, None, page_rows, HEAD_W), lambda s, j, pt: (layer, pt[s, j * pps + p], 0, 0))

    kern = functools.partial(_attn_sample_kernel, n_pages_step=pps, seq_new=Ls, lam_init=lam_init)
    grid_spec = pltpu.PrefetchScalarGridSpec(
        num_scalar_prefetch=1,
        grid=(S, steps),
        in_specs=[pl.BlockSpec((4, HEAD_DIM), lambda s, j, pt: (0, 0)),
                  pl.BlockSpec((1, HEAD_W), lambda s, j, pt: (0, 0)),
                  pl.BlockSpec((1, Ls, ATTN_W), lambda s, j, pt: (s, 0, 0)),
                  pl.BlockSpec((1, Ls, KV_W), lambda s, j, pt: (s, 0, 0)),
                  pl.BlockSpec((1, Ls, KV_W), lambda s, j, pt: (s, 0, 0))]
        + [page_spec(p) for p in range(pps)] + [page_spec(p) for p in range(pps)],
        out_specs=pl.BlockSpec((1, Ls, ATTN_W), lambda s, j, pt: (s, 0, 0)),
        scratch_shapes=[pltpu.VMEM((N_KV, 4 * Ls, LANES), f32), pltpu.VMEM((N_KV, 4 * Ls, LANES), f32),
                        pltpu.VMEM((N_KV, 4 * Ls, 2 * LANES), f32)],
    )
    return pl.pallas_call(
        kern,
        out_shape=jax.ShapeDtypeStruct((S, Ls, ATTN_W), bf16),
        grid_spec=grid_spec,
        compiler_params=_params(("parallel", "arbitrary")),
        name="attn_sample",
    )(page_table, lam_p, gsub, q, kb, vb, *([cache_k] * pps), *([cache_v] * pps))


def _split3(x):
    hi = x.astype(bf16)
    r1 = x - hi.astype(f32)
    mid = r1.astype(bf16)
    lo = (r1 - mid.astype(f32)).astype(bf16)
    return hi, mid, lo


def _ssd_kernel(*refs, Q, nb):
    for bi in range(nb):
        _ssd_row(bi, *refs, Q=Q)


def _ssd_row(bi, xbc_ref, zs_ref, dt_ref, dtT_ref, conv0_ref, ssm0_ref, cw_ref, cb_ref,
             alog_ref, alogT_ref, dskip_ref, gssm_ref, e_ref,
             y_ref, convo_ref, ssmo_ref, hT_all, cbuf_all, *, Q):
    c = pl.program_id(1)
    hT, cbuf = hT_all.at[bi], cbuf_all.at[bi]
    tail = CONV_W - 1
    base = SUBLANES

    @pl.when(c == 0)
    def _():
        cbuf[base - tail:base, :] = conv0_ref[bi]
        for g in range(SSM_G):
            hT[g] = ssm0_ref[bi, g * SSM_GW:(g + 1) * SSM_GW, :].T

    cbuf[base:base + Q, :] = xbc_ref[bi].astype(f32)
    acc = cb_ref[...] + cw_ref[CONV_W - 1:CONV_W, :] * cbuf[base:base + Q, :]
    for j in range(tail):
        acc = acc + cw_ref[j:j + 1, :] * cbuf[pl.ds(base - tail + j, Q), :]
    xc = _silu(acc)
    convo_ref[bi] = cbuf[pl.ds(base + Q - tail, tail), :]
    cbuf[0:SUBLANES, :] = cbuf[Q:Q + SUBLANES, :]

    xs = xc[:, :D_INNER]
    xs_b = xs.astype(bf16)
    e_mat = e_ref[...]

    dt = dt_ref[bi]
    dtT = dtT_ref[bi]
    a = dt * (-jnp.exp(alog_ref[...]))
    aT = dtT * (-jnp.exp(alogT_ref[...]))
    r_i = lax.broadcasted_iota(i32, (Q, Q), 0)
    c_i = lax.broadcasted_iota(i32, (Q, Q), 1)
    tril = (c_i <= r_i)
    tril_b = tril.astype(bf16)
    triu_b = (r_i <= c_i).astype(bf16)
    acs = sum(jnp.dot(tril_b, part, preferred_element_type=f32) for part in _split3(a))
    acsT = sum(jnp.dot(part, triu_b, preferred_element_type=f32) for part in _split3(aT))
    acs_last = acs[Q - 1:Q, :]
    e_acs = jnp.exp(acs)
    w_end = jnp.exp(acs_last - acs) * dt

    lane = lax.broadcasted_iota(i32, (Q, LANES), 1)
    first_head = lane < SSM_P
    y_cols = []
    new_h = []
    for g in range(SSM_G):
        bg = xc[:, D_INNER + g * D_STATE:D_INNER + (g + 1) * D_STATE].astype(bf16)
        cg = xc[:, D_INNER + SSM_G * D_STATE + g * D_STATE:D_INNER + SSM_G * D_STATE + (g + 1) * D_STATE].astype(bf16)
        cbm = lax.dot_general(cg, bg, (((1,), (1,)), ((), ())), preferred_element_type=f32)
        y_inter = jnp.dot(cg, hT[g].astype(bf16), preferred_element_type=f32)
        pair_cols = []
        for pr in range(SSM_GW // LANES):
            xp = xs_b[:, g * SSM_GW + pr * LANES:g * SSM_GW + (pr + 1) * LANES]
            ys = []
            for hh in range(2):
                h = g * (SSM_GW // SSM_P) + 2 * pr + hh
                seg = acs[:, h:h + 1] - acsT[h:h + 1, :]
                lm = jnp.exp(jnp.where(tril, seg, NEG))
                mm = (cbm * lm * dtT[h:h + 1, :]).astype(bf16)
                ys.append(jnp.dot(mm, xp, preferred_element_type=f32))
            pair_cols.append(jnp.where(first_head, ys[0], ys[1]))
        y_cols.append((jnp.concatenate(pair_cols, axis=1), y_inter))
        new_h.append((bg, g))

    e_acs_x = jnp.dot(e_acs.astype(bf16), e_mat, preferred_element_type=f32)
    w_end_x = jnp.dot(w_end.astype(bf16), e_mat, preferred_element_type=f32)
    xw = (xs * w_end_x).astype(bf16)
    dec8 = jnp.broadcast_to(jnp.exp(acs_last), (SUBLANES, LANES))
    dec_hi = dec8.astype(bf16)
    dec_lo = (dec8 - dec_hi.astype(f32)).astype(bf16)
    dec_x = (jnp.dot(dec_hi, e_mat, preferred_element_type=f32)
             + jnp.dot(dec_lo, e_mat, preferred_element_type=f32))[0:1, :]

    y_parts = []
    for g in range(SSM_G):
        y_intra, y_inter = y_cols[g]
        sl = slice(g * SSM_GW, (g + 1) * SSM_GW)
        yg = y_intra + y_inter * e_acs_x[:, sl] + dskip_ref[:, sl] * xs[:, sl]
        yg = yg * zs_ref[bi][:, sl].astype(f32)
        y_parts.append(_rms(yg, gssm_ref[:, sl], SSM_EPS))
        bg = new_h[g][0]
        upd = lax.dot_general(bg, xw[:, sl], (((0,), (0,)), ((), ())), preferred_element_type=f32)
        hT[g] = hT[g] * dec_x[:, sl] + upd
    y_ref[bi] = jnp.concatenate(y_parts, axis=1).astype(bf16)

    @pl.when(c == pl.num_programs(1) - 1)
    def _():
        for g in range(SSM_G):
            ssmo_ref[bi, g * SSM_GW:(g + 1) * SSM_GW, :] = hT[g].T


def _ssd(xbc, zs, dt, conv0, ssm0, conv_w, conv_b, a_log, d_skip, g_ssm, e_mat):
    B, L, _ = xbc.shape
    Q = min(SSD_CHUNK, L)
    assert L % Q == 0 and Q % SUBLANES == 0
    nc = L // Q
    dtT = jnp.swapaxes(dt[:, :, :SSM_H], 1, 2)
    alog_row = jnp.pad(a_log.reshape(1, SSM_H), ((0, 0), (0, LANES - SSM_H)))
    alog_col = a_log.reshape(SSM_H, 1)
    dskip_x = jnp.repeat(d_skip, SSM_P).reshape(1, D_INNER)
    nb = SSD_SHORT_SEQS_PER_STEP if (nc == 1 and B % SSD_SHORT_SEQS_PER_STEP == 0) else 1
    kern = functools.partial(_ssd_kernel, Q=Q, nb=nb)
    cmap = lambda b, c: (0, 0)
    return pl.pallas_call(
        kern,
        out_shape=(jax.ShapeDtypeStruct((B, L, D_INNER), bf16),
                   jax.ShapeDtypeStruct((B, CONV_W - 1, CONV_DIM), f32),
                   jax.ShapeDtypeStruct((B, D_INNER, D_STATE), f32)),
        grid=(B // nb, nc),
        in_specs=[pl.BlockSpec((nb, Q, CONV_DIM), lambda b, c: (b, c, 0)),
                  pl.BlockSpec((nb, Q, D_INNER), lambda b, c: (b, c, 0)),
                  pl.BlockSpec((nb, Q, LANES), lambda b, c: (b, c, 0)),
                  pl.BlockSpec((nb, SSM_H, Q), lambda b, c: (b, 0, c)),
                  pl.BlockSpec((nb, CONV_W - 1, CONV_DIM), lambda b, c: (b, 0, 0)),
                  pl.BlockSpec((nb, D_INNER, D_STATE), lambda b, c: (b, 0, 0)),
                  pl.BlockSpec((CONV_W, CONV_DIM), cmap), pl.BlockSpec((1, CONV_DIM), cmap),
                  pl.BlockSpec((1, LANES), cmap), pl.BlockSpec((SSM_H, 1), cmap),
                  pl.BlockSpec((1, D_INNER), cmap), pl.BlockSpec((1, D_INNER), cmap),
                  pl.BlockSpec((LANES, D_INNER), cmap)],
        out_specs=(pl.BlockSpec((nb, Q, D_INNER), lambda b, c: (b, c, 0)),
                   pl.BlockSpec((nb, CONV_W - 1, CONV_DIM), lambda b, c: (b, 0, 0)),
                   pl.BlockSpec((nb, D_INNER, D_STATE), lambda b, c: (b, 0, 0))),
        scratch_shapes=[pltpu.VMEM((nb, SSM_G, D_STATE, SSM_GW), f32),
                        pltpu.VMEM((nb, SUBLANES + Q, CONV_DIM), f32)],
        compiler_params=_params(("parallel", "arbitrary")),
        name="ssd_scan",
    )(xbc, zs, dt, dtT, conv0, ssm0, conv_w, conv_b.reshape(1, CONV_DIM), alog_row, alog_col,
      dskip_x, g_ssm.reshape(1, D_INNER), e_mat)


def _route(lg):
    lane = lax.broadcasted_iota(i32, lg.shape, 1)
    lanef = lane.astype(f32)
    big = float(LANES)
    gl = jnp.where(lane < N_EGROUPS, lg, NEG)
    gmax = jnp.max(gl, axis=1, keepdims=True)
    gsel = jnp.min(jnp.where(gl == gmax, lanef, big), axis=1, keepdims=True)
    p_g = 1.0 / jnp.sum(jnp.exp(gl - gmax), axis=1, keepdims=True)
    first = N_EGROUPS + E_PER_GROUP * gsel
    emask = (lanef >= first) & (lanef < first + E_PER_GROUP)
    el = jnp.where(emask, lg, NEG)
    v1 = jnp.max(el, axis=1, keepdims=True)
    i1 = jnp.min(jnp.where(emask & (el == v1), lanef, big), axis=1, keepdims=True)
    emask2 = emask & (lanef != i1)
    el2 = jnp.where(emask2, lg, NEG)
    v2 = jnp.max(el2, axis=1, keepdims=True)
    i2 = jnp.min(jnp.where(emask2 & (el2 == v2), lanef, big), axis=1, keepdims=True)
    t = jnp.exp(v2 - v1)
    w1 = p_g / (1.0 + t)
    w2 = p_g * t / (1.0 + t)
    out = jnp.where(lane == 0, i1 - N_EGROUPS, 0.0)
    out = jnp.where(lane == 1, i2 - N_EGROUPS, out)
    out = jnp.where(lane == 2, w1, out)
    out = jnp.where(lane == 3, w2, out)
    return out


def _to_row_tiles(ref, x, rows, first_row=0):
    for j in range(D_MODEL // LANES):
        ref[pl.ds(first_row * SUBLANES + j, rows, stride=SUBLANES), :] = x[:, j * LANES:(j + 1) * LANES]


def _from_row_tiles(ref, rows):
    return jnp.concatenate([ref[pl.ds(j, rows, stride=SUBLANES), :] for j in range(D_MODEL // LANES)], axis=1)


def _mix_kernel(x_ref, gt1_ref, sc2_ref, sh2_ref, o_ref, ys_ref, ga_ref, gs_ref,
                woa_ref, wos_ref, wo_ref, gpost_ref, gpre_ref, wr_ref, br_ref,
                x1_ref, u2_ref, route_ref):
    tm = x_ref.shape[0]
    n_parts = MIX_ROW_PARTS if tm % (MIX_ROW_PARTS * SUBLANES) == 0 else 1
    h = tm // n_parts

    def rows_of(ref, sl):
        return ref[sl, :] if ref.shape[0] == tm else ref[...]

    for part in range(n_parts):
        sl = slice(part * h, (part + 1) * h)
        ya = jnp.dot(o_ref[sl, :], woa_ref[...], preferred_element_type=f32)
        ys = jnp.dot(ys_ref[sl, :], wos_ref[...], preferred_element_type=f32)
        m = ga_ref[sl, :].astype(f32) * ya + gs_ref[sl, :].astype(f32) * ys
        mo = jnp.dot(m.astype(bf16), wo_ref[...], preferred_element_type=f32)
        x1 = x_ref[sl, :] + rows_of(gt1_ref, sl) * _rms(mo, gpost_ref[...], EPS)
        x1_ref[sl, :] = x1
        u2 = _rms(x1, gpre_ref[...], EPS) * (1.0 + rows_of(sc2_ref, sl)) + rows_of(sh2_ref, sl)
        _to_row_tiles(u2_ref, u2, h, first_row=part * h)
        lg = jnp.dot(u2.astype(bf16), wr_ref[...], preferred_element_type=f32) + br_ref[...]
        route_ref[sl, :] = _route(lg)


def _mix(grp, x2, o, ysn, ga, gs, w_oa, w_os, w_o, g_post1, g_pre2, w_r, b_r):
    T, tm = grp.T, grp.tm
    return pl.pallas_call(
        _mix_kernel,
        out_shape=(jax.ShapeDtypeStruct((T, D_MODEL), f32),
                   jax.ShapeDtypeStruct((T * SUBLANES, LANES), f32),
                   jax.ShapeDtypeStruct((T, LANES), f32)),
        grid=(grp.n_tiles,),
        in_specs=[grp.row_spec(D_MODEL), grp.mod_spec(2), grp.mod_spec(4), grp.mod_spec(3),
                  grp.row_spec(ATTN_W), grp.row_spec(D_INNER), grp.row_spec(D_MODEL), grp.row_spec(D_MODEL),
                  _const_spec(w_oa.shape), _const_spec(w_os.shape), _const_spec(w_o.shape),
                  _const_spec((1, D_MODEL)), _const_spec((1, D_MODEL)),
                  _const_spec(w_r.shape), _const_spec((1, LANES))],
        out_specs=(grp.row_spec(D_MODEL),
                   pl.BlockSpec((tm * SUBLANES, LANES), lambda i: (i, 0)),
                   grp.row_spec(LANES)),
        compiler_params=_params(("parallel",)),
        name="mixer_out",
    )(x2, grp.mod, grp.mod, grp.mod, o, ysn, ga, gs, w_oa, w_os, w_o, g_post1, g_pre2, w_r, b_r)


def _count_kernel(route_ref, cnt_ref):
    i = pl.program_id(0)

    @pl.when(i == 0)
    def _():
        cnt_ref[...] = jnp.zeros(cnt_ref.shape, f32)

    r = route_ref[...]
    lanef = lax.broadcasted_iota(i32, r.shape, 1).astype(f32)
    hits = (lanef == r[:, 0:1]).astype(f32) + (lanef == r[:, 1:2]).astype(f32)
    cnt_ref[...] += jnp.sum(hits, axis=0, keepdims=True)


def _expert_counts(route, tm):
    T = route.shape[0]
    return pl.pallas_call(
        _count_kernel,
        out_shape=jax.ShapeDtypeStruct((1, LANES), f32),
        grid=(T // tm,),
        in_specs=[pl.BlockSpec((tm, LANES), lambda i: (i, 0))],
        out_specs=pl.BlockSpec((1, LANES), lambda i: (0, 0)),
        compiler_params=_params(("arbitrary",)),
        name="moe_count",
    )(route)


def _slot_kernel(route_ref, start_ref, dest_ref, carry):
    i = pl.program_id(0)

    @pl.when(i == 0)
    def _():
        carry[...] = jnp.broadcast_to(start_ref[...], carry.shape)

    r = route_ref[...]
    tm = r.shape[0]
    lane = lax.broadcasted_iota(i32, r.shape, 1)
    lanef = lane.astype(f32)
    oh1 = (lanef == r[:, 0:1]).astype(f32)
    oh2 = (lanef == r[:, 1:2]).astype(f32)
    cnt = oh1 + oh2
    ri = lax.broadcasted_iota(i32, (tm, tm), 0)
    ci = lax.broadcasted_iota(i32, (tm, tm), 1)
    before = (ci < ri).astype(bf16)
    pos = jnp.dot(before, cnt.astype(bf16), preferred_element_type=f32) + carry[0:1, :]
    d1 = jnp.sum(oh1 * pos, axis=1, keepdims=True)
    d2 = jnp.sum(oh2 * pos, axis=1, keepdims=True)
    dest_ref[...] = jnp.where(lane == 0, d1, jnp.where(lane == 1, d2, 0.0)).astype(i32)
    carry[...] = carry[...] + jnp.sum(cnt, axis=0, keepdims=True)


def _slots(route, starts, tm):
    T = route.shape[0]
    return pl.pallas_call(
        _slot_kernel,
        out_shape=jax.ShapeDtypeStruct((T, LANES), i32),
        grid=(T // tm,),
        in_specs=[pl.BlockSpec((tm, LANES), lambda i: (i, 0)), pl.BlockSpec((1, LANES), lambda i: (0, 0))],
        out_specs=pl.BlockSpec((tm, LANES), lambda i: (i, 0)),
        scratch_shapes=[pltpu.VMEM((SUBLANES, LANES), f32)],
        compiler_params=_params(("arbitrary",)),
        name="moe_slots",
    )(route, starts)


def _row_copy(src, src_row, dst, dst_row, sem):
    return pltpu.make_async_copy(src.at[pl.ds(pl.multiple_of(src_row * SUBLANES, SUBLANES), SUBLANES), :],
                                 dst.at[pl.ds(pl.multiple_of(dst_row * SUBLANES, SUBLANES), SUBLANES), :], sem)


def _dispatch_kernel(dest_ref, u_ref, xs_in_ref, xs_ref, sem, *, tm):
    del xs_in_ref

    def issue(t, c):
        for k in range(2):
            _row_copy(u_ref, t, xs_ref, dest_ref[0, 0, 2 * t + k], sem).start(priority=k)
        return c

    lax.fori_loop(0, tm, issue, 0)
    rows = 2 * tm * SUBLANES
    pltpu.make_async_copy(xs_ref.at[pl.ds(0, rows), :], xs_ref.at[pl.ds(0, rows), :], sem).wait()


def _dispatch(dest2, u_rows, n_slots, tm):
    T = dest2.shape[0]
    xs0 = jnp.zeros((n_slots * SUBLANES, LANES), f32)
    dest3 = dest2.reshape(T // tm, 1, 2 * tm)
    return pl.pallas_call(
        functools.partial(_dispatch_kernel, tm=tm),
        out_shape=jax.ShapeDtypeStruct((n_slots * SUBLANES, LANES), f32),
        grid=(T // tm,),
        in_specs=[pl.BlockSpec((1, 1, 2 * tm), lambda i: (i, 0, 0), memory_space=pltpu.SMEM),
                  pl.BlockSpec((tm * SUBLANES, LANES), lambda i: (i, 0)), pl.BlockSpec(memory_space=pl.ANY)],
        out_specs=pl.BlockSpec(memory_space=pl.ANY),
        scratch_shapes=[pltpu.SemaphoreType.DMA(())],
        input_output_aliases={2: 0},
        compiler_params=_params(("arbitrary",)),
        name="moe_dispatch",
    )(dest3, u_rows, xs0)


def _expert_kernel(be_ref, xs_ref, wg_ref, wu_ref, wd_ref, o_ref, *, blk):
    del be_ref
    x = _from_row_tiles(xs_ref, blk).astype(bf16)
    h = _silu(jnp.dot(x, wg_ref[...], preferred_element_type=f32)) * jnp.dot(x, wu_ref[...], preferred_element_type=f32)
    _to_row_tiles(o_ref, jnp.dot(h.astype(bf16), wd_ref[...], preferred_element_type=f32), blk)


def _experts(xs, blk_e, w_gate, w_up, w_down, blk):
    n_blocks = blk_e.shape[0]
    rows = blk * SUBLANES
    grid_spec = pltpu.PrefetchScalarGridSpec(
        num_scalar_prefetch=1,
        grid=(n_blocks,),
        in_specs=[pl.BlockSpec((rows, LANES), lambda i, be: (i, 0)),
                  pl.BlockSpec((None, D_MODEL, D_FF), lambda i, be: (be[i], 0, 0)),
                  pl.BlockSpec((None, D_MODEL, D_FF), lambda i, be: (be[i], 0, 0)),
                  pl.BlockSpec((None, D_FF, D_MODEL), lambda i, be: (be[i], 0, 0))],
        out_specs=pl.BlockSpec((rows, LANES), lambda i, be: (i, 0)),
    )
    return pl.pallas_call(
        functools.partial(_expert_kernel, blk=blk),
        out_shape=jax.ShapeDtypeStruct(xs.shape, f32),
        grid_spec=grid_spec,
        compiler_params=_params(("parallel",)),
        name="moe_experts",
    )(blk_e, xs, w_gate, w_up, w_down)


def _combine_kernel(dest_ref, dest_next_ref, outs_ref, x1_ref, gt2_ref, route_ref, gpost_ref, y_ref, buf, sem, *, tm):
    i = pl.program_id(0)
    slot = jnp.bitwise_and(i, 1)

    def gather(d_ref, s):
        def issue(t, c):
            for k in range(2):
                _row_copy(outs_ref, d_ref[0, 0, 2 * t + k], buf.at[s, k], t, sem.at[s]).start(priority=k)
            return c
        lax.fori_loop(0, tm, issue, 0)

    @pl.when(i == 0)
    def _():
        gather(dest_ref, 0)

    @pl.when(i + 1 < pl.num_programs(0))
    def _():
        gather(dest_next_ref, 1 - slot)

    pltpu.make_async_copy(buf.at[slot], buf.at[slot], sem.at[slot]).wait()
    r = route_ref[...]
    f = r[:, 2:3] * _from_row_tiles(buf.at[slot, 0], tm) + r[:, 3:4] * _from_row_tiles(buf.at[slot, 1], tm)
    y_ref[...] = x1_ref[...] + gt2_ref[...] * _rms(f, gpost_ref[...], EPS)


def _combine(grp, dest2, outs, x1, route, g_post2, tm):
    T = grp.T
    n = T // tm
    dest3 = dest2.reshape(n, 1, 2 * tm)
    return pl.pallas_call(
        functools.partial(_combine_kernel, tm=tm),
        out_shape=jax.ShapeDtypeStruct((T, D_MODEL), f32),
        grid=(n,),
        in_specs=[pl.BlockSpec((1, 1, 2 * tm), lambda i: (i, 0, 0), memory_space=pltpu.SMEM),
                  pl.BlockSpec((1, 1, 2 * tm), lambda i: (jnp.minimum(i + 1, n - 1), 0, 0), memory_space=pltpu.SMEM),
                  pl.BlockSpec(memory_space=pl.ANY),
                  grp.row_spec(D_MODEL), grp.mod_spec(5), grp.row_spec(LANES), _const_spec((1, D_MODEL))],
        out_specs=grp.row_spec(D_MODEL),
        scratch_shapes=[pltpu.VMEM((2, 2, tm * SUBLANES, LANES), f32), pltpu.SemaphoreType.DMA((2,))],
        compiler_params=_params(("arbitrary",)),
        name="moe_combine",
    )(dest3, dest3, outs, x1, grp.mod, route, g_post2)


def _hmoe_and_residual(grp, x1, u_rows, route, w_gate, w_up, w_down, g_post2):
    T = grp.T
    tm = grp.tm
    blk = MOE_BLOCK if 2 * T >= N_EXPERTS * MOE_BLOCK else MOE_BLOCK_SMALL
    counts = _expert_counts(route, tm)[0, :N_EXPERTS].astype(i32)
    padded = (counts + blk - 1) // blk * blk
    ends = jnp.cumsum(padded)
    starts = ends - padded
    n_blocks = -(-2 * T // blk) + N_EXPERTS
    n_slots = n_blocks * blk
    blk_start = jnp.arange(n_blocks, dtype=i32) * blk
    blk_e = jnp.minimum(jnp.sum(blk_start[:, None] >= ends[None, :], axis=1), N_EXPERTS - 1).astype(i32)
    starts_row = jnp.pad(starts.astype(f32), (0, LANES - N_EXPERTS)).reshape(1, LANES)
    dest2 = _slots(route, starts_row, tm)[:, :2]
    xs = _dispatch(dest2, u_rows, n_slots, tm)
    outs = _experts(xs, blk_e, w_gate, w_up, w_down, blk)
    return _combine(grp, dest2, outs, x1, route, g_post2, tm)


def _rope_tables(pos):
    half = HEAD_DIM // 2
    inv = ROPE_THETA ** (-2.0 * jnp.arange(half, dtype=f32) / HEAD_DIM)
    ang = pos.astype(f32)[:, None] * inv[None, :]
    cos, sin = jnp.cos(ang), jnp.sin(ang)
    return jnp.tile(cos, (1, LANES // half)), jnp.tile(jnp.concatenate([-sin, sin], axis=1), (1, LANES // HEAD_DIM))


def _layer(grp, x2, lw, lam_init, attend, conv0, ssm0):
    n_seq, seq_len = grp.n_seq, grp.seq_len
    q, k, kb, v, vb = _proj_qkv(grp, x2, lw["g_pre1"], lw["w_qkv"], 1, 0)
    zs, xbc = _proj_zx(grp, x2, lw["g_pre1"], lw["w_z"], lw["w_xbc"], 1, 0)
    ga, gs, dt = _proj_gates(grp, x2, lw["g_pre1"], lw["w_gates"], lw["w_dt"], lw["dt_bias"], 1, 0)
    shp = lambda a: a.reshape(n_seq, seq_len, a.shape[-1])
    o = attend(shp(q), shp(kb), shp(vb))
    ysn, conv_new, ssm_new = _ssd(shp(xbc), shp(zs), shp(dt), conv0, ssm0.reshape(n_seq, D_INNER, D_STATE),
                                  lw["conv_w"], lw["conv_b"], lw["a_log"], lw["d_skip"], lw["g_ssm"], lw["e_mat"])
    x1, u_rows, route = _mix(grp, x2, o.reshape(grp.T, ATTN_W), ysn.reshape(grp.T, D_INNER), ga, gs,
                             lw["w_oa"], lw["w_os"], lw["w_o"], lw["g_post1"], lw["g_pre2"], lw["w_r"], lw["b_r"])
    y2 = _hmoe_and_residual(grp, x1, u_rows, route, lw["w_gate"], lw["w_up"], lw["w_down"], lw["g_post2"])
    return (y2, k.reshape(n_seq, seq_len, N_KV, HEAD_W), v.reshape(n_seq, seq_len, N_KV, HEAD_W), conv_new,
            ssm_new.reshape(n_seq, SSM_H, SSM_P, D_STATE))


def kernel(x_prompt, x_sample, cache_k, cache_v, state_conv, state_ssm, page_table, c_prompt, c_sample, w_ada, b_ada, g_pre1, g_post1, g_pre2, g_post2, w_in, lambda_q1, lambda_k1, lambda_q2, lambda_k2, g_subln, w_oa, conv_w, conv_b, dt_bias, a_log, d_skip, g_ssm, w_os, w_o, w_rg, b_rg, w_re, b_re, w_gate, w_up, w_down):
    depth = w_in.shape[0]
    bp, lp_len, _ = x_prompt.shape
    bs, ls_len, _ = x_sample.shape
    past = page_table.shape[1] * cache_k.shape[2]
    ck = cache_k.reshape(cache_k.shape[0], cache_k.shape[1], cache_k.shape[2] * N_KV, HEAD_W)
    cv = cache_v.reshape(cache_v.shape[0], cache_v.shape[1], cache_v.shape[2] * N_KV, HEAD_W)
    rope_p = _rope_tables(jnp.arange(lp_len, dtype=i32))
    rope_s8 = _rope_tables(past + jnp.arange(ls_len, dtype=i32))
    rope_s = tuple(jnp.tile(t, (bs, 1)) for t in rope_s8)
    e_mat = (jnp.arange(LANES, dtype=i32)[:, None] == jnp.arange(D_INNER, dtype=i32)[None, :] // SSM_P).astype(bf16)

    hp = x_prompt.reshape(bp * lp_len, D_MODEL)
    hs = x_sample.reshape(bs * ls_len, D_MODEL)
    c_all = jnp.concatenate([c_prompt, c_sample], axis=0)
    outs = [[] for _ in range(8)]
    for l in range(depth):
        lam_init = 0.8 - 0.6 * math.exp(-0.3 * l)
        w_in_l = w_in[l].astype(bf16)
        lw = dict(
            w_qkv=w_in_l[:, :OFF_Z], w_z=w_in_l[:, OFF_Z:OFF_XBC], w_xbc=w_in_l[:, OFF_XBC:OFF_DT],
            w_dt=jnp.pad(w_in_l[:, OFF_DT:OFF_GA], ((0, 0), (0, LANES - SSM_H))), w_gates=w_in_l[:, OFF_GA:],
            dt_bias=jnp.pad(dt_bias[l].reshape(1, SSM_H), ((0, 0), (0, LANES - SSM_H))),
            g_pre1=g_pre1[l].reshape(1, D_MODEL), g_post1=g_post1[l].reshape(1, D_MODEL),
            g_pre2=g_pre2[l].reshape(1, D_MODEL), g_post2=g_post2[l].reshape(1, D_MODEL),
            w_oa=w_oa[l].astype(bf16), w_os=w_os[l].astype(bf16), w_o=w_o[l].astype(bf16),
            conv_w=conv_w[l], conv_b=conv_b[l], a_log=a_log[l], d_skip=d_skip[l], g_ssm=g_ssm[l], e_mat=e_mat,
            w_r=jnp.pad(jnp.concatenate([w_rg[l], w_re[l]], axis=1),
                        ((0, 0), (0, LANES - N_EGROUPS - N_EXPERTS))).astype(bf16),
            b_r=jnp.pad(jnp.concatenate([b_rg[l], b_re[l]]), (0, LANES - N_EGROUPS - N_EXPERTS)).reshape(1, LANES),
            w_gate=w_gate[l].astype(bf16), w_up=w_up[l].astype(bf16), w_down=w_down[l].astype(bf16),
        )
        lam_p = jnp.stack([lambda_q1[l], lambda_k1[l], lambda_q2[l], lambda_k2[l]])
        gsub = g_subln[l].reshape(1, HEAD_W)
        mod = _ada(c_all, w_ada[l].astype(bf16), b_ada[l])
        grp_p = _Group(bp, lp_len, mod[:bp], rope_p, per_token=False)
        grp_s = _Group(bs, ls_len, jnp.repeat(mod[bp:], ls_len, axis=0), rope_s, per_token=True)

        attend_p = lambda q, kb, vb: _attn_prompt(q, kb, vb, lam_p, gsub, lam_init)
        attend_s = lambda q, kb, vb: _attn_sample(q, kb, vb, ck, cv, page_table, l, lam_p, gsub, lam_init)
        conv0 = jnp.zeros((bp, CONV_W - 1, CONV_DIM), f32)
        ssm0 = jnp.zeros((bp, SSM_H, SSM_P, D_STATE), f32)
        hp, kp, vp, cp, sp = _layer(grp_p, hp, lw, lam_init, attend_p, conv0, ssm0)
        hs, ksm, vsm, csm, ssm = _layer(grp_s, hs, lw, lam_init, attend_s, state_conv[l], state_ssm[l])
        for lst, val in zip(outs, (kp, vp, cp, sp, ksm, vsm, csm, ssm)):
            lst.append(val)
    stacked = [jnp.stack(o) for o in outs]
    return (hp.reshape(bp, lp_len, D_MODEL), hs.reshape(bs, ls_len, D_MODEL), *stacked)
```

```python
import functools
import math

import jax
import jax.numpy as jnp
from jax import lax
from jax.experimental import pallas as pl
from jax.experimental.pallas import tpu as pltpu

f32 = jnp.float32
bf16 = jnp.bfloat16
i32 = jnp.int32

D_MODEL = 1024
N_HEADS = 8
N_KV = 4
HEAD_DIM = 64
HEAD_W = 2 * HEAD_DIM
ATTN_W = N_HEADS * HEAD_W
KV_W = N_KV * HEAD_W
ATTN_SCALE = HEAD_DIM ** -0.5
ROPE_THETA = 10000.0
SUBLN_EPS = 1e-5
D_INNER = 2048
SSM_P = 64
SSM_H = D_INNER // SSM_P
SSM_G = 4
SSM_GW = D_INNER // SSM_G
D_STATE = 128
CONV_W = 4
CONV_DIM = D_INNER + 2 * SSM_G * D_STATE
SSD_CHUNK = 128
SSM_EPS = 1e-5
N_EGROUPS = 4
E_PER_GROUP = 8
N_EXPERTS = 32
D_FF = 512
EPS = 1e-6
OFF_K = ATTN_W
OFF_V = OFF_K + KV_W
OFF_Z = OFF_V + KV_W
OFF_XBC = OFF_Z + D_INNER
OFF_DT = OFF_XBC + CONV_DIM
OFF_GA = OFF_DT + SSM_H
OFF_GS = OFF_GA + D_MODEL

LANES = 128
SUBLANES = 8
VMEM_LIMIT = 56 * 1024 * 1024
NEG = -1e30
LOG2E = math.log2(math.e)

TOKEN_TILE = 512
ATTN_TILE = 512
PAGES_PER_STEP = 32
MOE_BLOCK = 512
MOE_BLOCK_SMALL = 128
SSD_SHORT_SEQS_PER_STEP = 4


def _params(sem, vmem=VMEM_LIMIT):
    return pltpu.CompilerParams(dimension_semantics=sem, vmem_limit_bytes=vmem)


def _sigmoid(x):
    return 1.0 / (1.0 + jnp.exp(-x))


def _silu(x):
    return x * _sigmoid(x)


def _rms(x, g, eps):
    return x * lax.rsqrt(jnp.mean(x * x, axis=-1, keepdims=True) + eps) * g


def _bdot(a, b):
    return jnp.dot(a.astype(bf16), b.astype(bf16), preferred_element_type=f32)


class _Group:
    def __init__(self, n_seq, seq_len, mod_rows, rope_rows, per_token):
        self.n_seq, self.seq_len = n_seq, seq_len
        self.T = n_seq * seq_len
        self.tm = min(TOKEN_TILE, self.T if per_token else seq_len)
        assert self.T % self.tm == 0
        self.per_token = per_token
        if per_token:
            self.mod = mod_rows
            self.rope = rope_rows
        else:
            assert seq_len % self.tm == 0
            self.tiles_per_seq = seq_len // self.tm
            self.mod = mod_rows.reshape(n_seq, 1, 6 * D_MODEL)
            self.rope = rope_rows
        self.n_tiles = self.T // self.tm

    def mod_spec(self, col):
        if self.per_token:
            return pl.BlockSpec((self.tm, D_MODEL), lambda i: (i, col))
        tps = self.tiles_per_seq
        return pl.BlockSpec((None, 1, D_MODEL), lambda i: (i // tps, 0, col))

    def rope_spec(self):
        if self.per_token:
            return pl.BlockSpec((self.tm, LANES), lambda i: (i, 0))
        tps = self.tiles_per_seq
        return pl.BlockSpec((self.tm, LANES), lambda i: (i % tps, 0))

    def row_spec(self, width):
        return pl.BlockSpec((self.tm, width), lambda i: (i, 0))


def _const_spec(shape):
    nd = len(shape)
    return pl.BlockSpec(shape, lambda i: (0,) * nd)


def _ada_kernel(c_ref, w_ref, b_ref, o_ref):
    o_ref[...] = _bdot(_silu(c_ref[...]), w_ref[...]) + b_ref[...]


def _ada(c, w_ada, b_ada):
    m = c.shape[0]
    n = w_ada.shape[1]
    tn = D_MODEL
    return pl.pallas_call(
        _ada_kernel,
        out_shape=jax.ShapeDtypeStruct((m, n), f32),
        grid=(n // tn,),
        in_specs=[pl.BlockSpec((m, D_MODEL), lambda j: (0, 0)),
                  pl.BlockSpec((D_MODEL, tn), lambda j: (0, j)),
                  pl.BlockSpec((1, tn), lambda j: (0, j))],
        out_specs=pl.BlockSpec((m, tn), lambda j: (0, j)),
        compiler_params=_params(("parallel",)),
        name="ada_mod",
    )(c, w_ada, b_ada.reshape(1, n))


def _modulated(x_ref, g_ref, sc_ref, sh_ref):
    u = _rms(x_ref[...], g_ref[...], EPS) * (1.0 + sc_ref[...]) + sh_ref[...]
    return u.astype(bf16)


def _rope_cols(x, n_blocks, cos, sin_signed, lo):
    out = []
    for j in range(n_blocks):
        xb = x[:, j * LANES:(j + 1) * LANES]
        partner = jnp.where(lo, pltpu.roll(xb, LANES - HEAD_DIM // 2, 1), pltpu.roll(xb, HEAD_DIM // 2, 1))
        out.append(xb * cos + partner * sin_signed)
    return jnp.concatenate(out, axis=1)


def _qkv_kernel(x_ref, g_ref, sc_ref, sh_ref, w_ref, cos_ref, sin_ref,
                q_ref, k_ref, kb_ref, v_ref, vb_ref):
    u = _modulated(x_ref, g_ref, sc_ref, sh_ref)
    proj = jnp.dot(u, w_ref[...], preferred_element_type=f32)
    cos, sin_signed = cos_ref[...], sin_ref[...]
    lane = lax.broadcasted_iota(i32, cos.shape, 1)
    lo = jnp.bitwise_and(lane, HEAD_DIM - 1) < HEAD_DIM // 2
    q = _rope_cols(proj[:, :OFF_K], N_HEADS, cos, sin_signed, lo)
    k = _rope_cols(proj[:, OFF_K:OFF_V], N_KV, cos, sin_signed, lo)
    v = proj[:, OFF_V:OFF_Z]
    q_ref[...] = (q * (ATTN_SCALE * LOG2E)).astype(bf16)
    kb_ref[...] = k.astype(bf16)
    vb_ref[...] = v.astype(bf16)
    tm = k.shape[0]
    for h in range(N_KV):
        k_ref[pl.ds(h, tm, stride=N_KV), :] = k[:, h * HEAD_W:(h + 1) * HEAD_W]
        v_ref[pl.ds(h, tm, stride=N_KV), :] = v[:, h * HEAD_W:(h + 1) * HEAD_W]


def _proj_qkv(grp, x2, g, w_qkv, col_scale, col_shift):
    T, tm = grp.T, grp.tm
    head_rows = pl.BlockSpec((tm * N_KV, HEAD_W), lambda i: (i, 0))
    return pl.pallas_call(
        _qkv_kernel,
        out_shape=(jax.ShapeDtypeStruct((T, ATTN_W), bf16),
                   jax.ShapeDtypeStruct((T * N_KV, HEAD_W), f32), jax.ShapeDtypeStruct((T, KV_W), bf16),
                   jax.ShapeDtypeStruct((T * N_KV, HEAD_W), f32), jax.ShapeDtypeStruct((T, KV_W), bf16)),
        grid=(grp.n_tiles,),
        in_specs=[grp.row_spec(D_MODEL), _const_spec((1, D_MODEL)),
                  grp.mod_spec(col_scale), grp.mod_spec(col_shift),
                  _const_spec(w_qkv.shape), grp.rope_spec(), grp.rope_spec()],
        out_specs=(grp.row_spec(ATTN_W), head_rows, grp.row_spec(KV_W),
                   head_rows, grp.row_spec(KV_W)),
        compiler_params=_params(("parallel",)),
        name="proj_qkv",
    )(x2, g, grp.mod, grp.mod, w_qkv, grp.rope[0], grp.rope[1])


def _zx_kernel(x_ref, g_ref, sc_ref, sh_ref, wz_ref, wx_ref, zs_ref, xbc_ref):
    u = _modulated(x_ref, g_ref, sc_ref, sh_ref)
    zs_ref[...] = _silu(jnp.dot(u, wz_ref[...], preferred_element_type=f32)).astype(bf16)
    xbc_ref[...] = jnp.dot(u, wx_ref[...], preferred_element_type=f32).astype(bf16)


def _proj_zx(grp, x2, g, w_z, w_xbc, col_scale, col_shift):
    T = grp.T
    return pl.pallas_call(
        _zx_kernel,
        out_shape=(jax.ShapeDtypeStruct((T, D_INNER), bf16), jax.ShapeDtypeStruct((T, CONV_DIM), bf16)),
        grid=(grp.n_tiles,),
        in_specs=[grp.row_spec(D_MODEL), _const_spec((1, D_MODEL)),
                  grp.mod_spec(col_scale), grp.mod_spec(col_shift),
                  _const_spec(w_z.shape), _const_spec(w_xbc.shape)],
        out_specs=(grp.row_spec(D_INNER), grp.row_spec(CONV_DIM)),
        compiler_params=_params(("parallel",)),
        name="proj_zx",
    )(x2, g, grp.mod, grp.mod, w_z, w_xbc)


def _gates_kernel(x_ref, g_ref, sc_ref, sh_ref, wg_ref, wdt_ref, dtb_ref, ga_ref, gs_ref, dt_ref):
    u = _modulated(x_ref, g_ref, sc_ref, sh_ref)
    gates = _sigmoid(jnp.dot(u, wg_ref[...], preferred_element_type=f32))
    ga_ref[...] = gates[:, :D_MODEL].astype(bf16)
    gs_ref[...] = gates[:, D_MODEL:].astype(bf16)
    raw = jnp.dot(u, wdt_ref[...], preferred_element_type=f32) + dtb_ref[...]
    dt_ref[...] = jnp.maximum(raw, 0.0) + jnp.log1p(jnp.exp(-jnp.abs(raw)))


def _proj_gates(grp, x2, g, w_gates, w_dt, dt_bias, col_scale, col_shift):
    T = grp.T
    return pl.pallas_call(
        _gates_kernel,
        out_shape=(jax.ShapeDtypeStruct((T, D_MODEL), bf16), jax.ShapeDtypeStruct((T, D_MODEL), bf16),
                   jax.ShapeDtypeStruct((T, LANES), f32)),
        grid=(grp.n_tiles,),
        in_specs=[grp.row_spec(D_MODEL), _const_spec((1, D_MODEL)),
                  grp.mod_spec(col_scale), grp.mod_spec(col_shift),
                  _const_spec(w_gates.shape), _const_spec(w_dt.shape), _const_spec((1, LANES))],
        out_specs=(grp.row_spec(D_MODEL), grp.row_spec(D_MODEL), grp.row_spec(LANES)),
        compiler_params=_params(("parallel",)),
        name="proj_gates",
    )(x2, g, grp.mod, grp.mod, w_gates, w_dt, dt_bias)


def _lambda_full(lam_ref, lam_init):
    lp = lam_ref[...]
    l1 = jnp.sum(lp[0:1] * lp[1:2], axis=1, keepdims=True)
    l2 = jnp.sum(lp[2:3] * lp[3:4], axis=1, keepdims=True)
    return jnp.exp(l1) - jnp.exp(l2) + lam_init


def _with_ones(v):
    return jnp.concatenate([v, jnp.ones(v.shape, bf16)], axis=1)


def _softmax_step(s, v1, m_ref, acc_ref):
    reps = s.shape[1] // LANES if s.shape[1] % LANES == 0 else 0
    m_prev = m_ref[...]
    m_new = jnp.maximum(m_prev, jnp.max(s, axis=1, keepdims=True))
    alpha = jnp.exp2(m_prev - m_new)
    if reps:
        p = jnp.exp2(s - jnp.concatenate([m_new] * reps, axis=1))
    else:
        p = jnp.exp2(s - m_new[:, 0:1])
    acc_ref[...] = (jnp.concatenate([alpha, alpha], axis=1) * acc_ref[...]
                    + jnp.dot(p.astype(bf16), v1, preferred_element_type=f32))
    m_ref[...] = m_new


def _diff_heads(acc, lam, gsub, lam_init, rows):
    outs = []
    for r in range(2):
        a0 = acc[(2 * r) * rows:(2 * r + 1) * rows]
        a1 = acc[(2 * r + 1) * rows:(2 * r + 2) * rows]
        o = a0[:, :HEAD_W] / a0[:, HEAD_W:] - lam * (a1[:, :HEAD_W] / a1[:, HEAD_W:])
        outs.append(_rms(o, gsub, SUBLN_EPS) * (1.0 - lam_init))
    return jnp.concatenate(outs, axis=1)


def _pad_components(q, rows):
    lane = lax.broadcasted_iota(i32, (rows, LANES), 1)
    lo = lane < HEAD_DIM
    blocks = []
    for r in range(2):
        qh = q[:, r * LANES:(r + 1) * LANES].astype(f32)
        blocks.append(jnp.where(lo, qh, 0.0))
        blocks.append(jnp.where(lo, 0.0, qh))
    return jnp.concatenate(blocks, axis=0)


def _stacked_causal(n, copies=4):
    assert n & (n - 1) == 0
    row = jnp.bitwise_and(lax.broadcasted_iota(i32, (copies * n, n), 0), n - 1)
    col = lax.broadcasted_iota(i32, (copies * n, n), 1)
    return col <= row


def _attn_prompt_kernel(lam_ref, gsub_ref, q_ref, k_ref, v_ref, o_ref, qpad, m_s, acc_s, s_a, s_b, *, tq, lam_init):
    i = pl.program_id(2)
    qpad[...] = _pad_components(q_ref[0], tq).astype(bf16)
    m_s[...] = jnp.full(m_s.shape, NEG, f32)
    acc_s[...] = jnp.zeros(acc_s.shape, f32)

    def scores(c):
        kc = k_ref[0, pl.ds(pl.multiple_of(c * tq, tq), tq), :]
        return lax.dot_general(qpad[...], kc, (((1,), (1,)), ((), ())), preferred_element_type=f32)

    def consume(s_ref, c):
        _softmax_step(s_ref[...], _with_ones(v_ref[0, pl.ds(pl.multiple_of(c * tq, tq), tq), :]), m_s, acc_s)

    s_a[...] = jnp.where(_stacked_causal(tq), scores(i), NEG)

    def pair(p, has_next):
        s_b[...] = scores(2 * p)
        consume(s_a, jnp.where(p == 0, i, 2 * p - 1))
        if has_next:
            s_a[...] = scores(2 * p + 1)
        consume(s_b, 2 * p)

    def two_pairs(t, carry):
        pair(2 * t, True)
        pair(2 * t + 1, True)
        return carry

    odd = jnp.bitwise_and(i, 1)
    n_full = jnp.right_shift(i + 1, 1) - odd
    lax.fori_loop(0, jnp.right_shift(n_full, 1), two_pairs, 0)

    @pl.when(jnp.bitwise_and(n_full, 1) == 1)
    def _():
        pair(n_full - 1, True)

    @pl.when(odd == 1)
    def _():
        pair(n_full, False)

    @pl.when(odd == 0)
    def _():
        consume(s_a, jnp.maximum(i - 1, 0))

    lam = _lambda_full(lam_ref, lam_init)
    o_ref[0] = _diff_heads(acc_s[...], lam, gsub_ref[...], lam_init, tq).astype(bf16)


def _attn_prompt(q, kb, vb, lam_p, gsub, lam_init):
    B, L, _ = q.shape
    tq = min(ATTN_TILE, L)
    assert L % tq == 0
    kern = functools.partial(_attn_prompt_kernel, tq=tq, lam_init=lam_init)
    return pl.pallas_call(
        kern,
        out_shape=jax.ShapeDtypeStruct((B, L, ATTN_W), bf16),
        grid=(B, N_KV, L // tq),
        in_specs=[pl.BlockSpec((4, HEAD_DIM), lambda b, g, i: (0, 0)),
                  pl.BlockSpec((1, HEAD_W), lambda b, g, i: (0, 0)),
                  pl.BlockSpec((1, tq, 2 * HEAD_W), lambda b, g, i: (b, i, g)),
                  pl.BlockSpec((1, L, HEAD_W), lambda b, g, i: (b, 0, g)),
                  pl.BlockSpec((1, L, HEAD_W), lambda b, g, i: (b, 0, g))],
        out_specs=pl.BlockSpec((1, tq, 2 * HEAD_W), lambda b, g, i: (b, i, g)),
        scratch_shapes=[pltpu.VMEM((4 * tq, LANES), bf16), pltpu.VMEM((4 * tq, LANES), f32),
                        pltpu.VMEM((4 * tq, 2 * LANES), f32),
                        pltpu.VMEM((4 * tq, tq), f32), pltpu.VMEM((4 * tq, tq), f32)],
        compiler_params=_params(("parallel", "parallel", "arbitrary")),
        name="attn_prompt",
    )(lam_p, gsub, q, kb, vb)


def _attn_sample_kernel(pt_ref, lam_ref, gsub_ref, q_ref, kn_ref, vn_ref, *rest, n_pages_step, seq_new, lam_init):
    del pt_ref
    kp = rest[:n_pages_step]
    vp = rest[n_pages_step:2 * n_pages_step]
    o_ref, qpad, m_s, acc_s = rest[2 * n_pages_step:]
    j = pl.program_id(1)

    @pl.when(j == 0)
    def _():
        q = q_ref[0]
        for g in range(N_KV):
            qpad[g] = _pad_components(q[:, g * 2 * HEAD_W:(g + 1) * 2 * HEAD_W], seq_new)
        m_s[...] = jnp.full(m_s.shape, NEG, f32)
        acc_s[...] = jnp.zeros(acc_s.shape, f32)

    page = kp[0].shape[0] // N_KV
    for g in range(N_KV):
        qg = qpad[g].astype(bf16)
        kg = jnp.concatenate([r[pl.ds(g, page, stride=N_KV), :].astype(bf16) for r in kp], axis=0)
        vg = jnp.concatenate([r[pl.ds(g, page, stride=N_KV), :].astype(bf16) for r in vp], axis=0)
        s = lax.dot_general(qg, kg, (((1,), (1,)), ((), ())), preferred_element_type=f32)
        _softmax_step(s, _with_ones(vg), m_s.at[g], acc_s.at[g])

    @pl.when(j == pl.num_programs(1) - 1)
    def _():
        lam = _lambda_full(lam_ref, lam_init)
        causal = _stacked_causal(seq_new)
        outs = []
        for g in range(N_KV):
            qg = qpad[g].astype(bf16)
            kg = kn_ref[0][:, g * HEAD_W:(g + 1) * HEAD_W]
            vg = vn_ref[0][:, g * HEAD_W:(g + 1) * HEAD_W]
            s = lax.dot_general(qg, kg, (((1,), (1,)), ((), ())), preferred_element_type=f32)
            _softmax_step(jnp.where(causal, s, NEG), _with_ones(vg), m_s.at[g], acc_s.at[g])
            outs.append(_diff_heads(acc_s[g], lam, gsub_ref[...], lam_init, seq_new))
        o_ref[0] = jnp.concatenate(outs, axis=1).astype(bf16)


def _attn_sample(q, kb, vb, cache_k, cache_v, page_table, layer, lam_p, gsub, lam_init):
    S, Ls, _ = q.shape
    n_pages = page_table.shape[1]
    page_rows = cache_k.shape[2]
    pps = PAGES_PER_STEP
    while n_pages % pps:
        pps -= 1
    steps = n_pages // pps

    def page_spec(p):
        return pl.BlockSpec((None, None, page_rows, HEAD_W), lambda s, j, pt: (layer, pt[s, j * pps + p], 0, 0))

    kern = functools.partial(_attn_sample_kernel, n_pages_step=pps, seq_new=Ls, lam_init=lam_init)
    grid_spec = pltpu.PrefetchScalarGridSpec(
        num_scalar_prefetch=1,
        grid=(S, steps),
        in_specs=[pl.BlockSpec((4, HEAD_DIM), lambda s, j, pt: (0, 0)),
                  pl.BlockSpec((1, HEAD_W), lambda s, j, pt: (0, 0)),
                  pl.BlockSpec((1, Ls, ATTN_W), lambda s, j, pt: (s, 0, 0)),
                  pl.BlockSpec((1, Ls, KV_W), lambda s, j, pt: (s, 0, 0)),
                  pl.BlockSpec((1, Ls, KV_W), lambda s, j, pt: (s, 0, 0))]
        + [page_spec(p) for p in range(pps)] + [page_spec(p) for p in range(pps)],
        out_specs=pl.BlockSpec((1, Ls, ATTN_W), lambda s, j, pt: (s, 0, 0)),
        scratch_shapes=[pltpu.VMEM((N_KV, 4 * Ls, LANES), f32), pltpu.VMEM((N_KV, 4 * Ls, LANES), f32),
                        pltpu.VMEM((N_KV, 4 * Ls, 2 * LANES), f32)],
    )
    return pl.pallas_call(
        kern,
        out_shape=jax.ShapeDtypeStruct((S, Ls, ATTN_W), bf16),
        grid_spec=grid_spec,
        compiler_params=_params(("parallel", "arbitrary")),
        name="attn_sample",
    )(page_table, lam_p, gsub, q, kb, vb, *([cache_k] * pps), *([cache_v] * pps))


def _split3(x):
    hi = x.astype(bf16)
    r1 = x - hi.astype(f32)
    mid = r1.astype(bf16)
    lo = (r1 - mid.astype(f32)).astype(bf16)
    return hi, mid, lo


def _ssd_kernel(*refs, Q, nb):
    for bi in range(nb):
        _ssd_row(bi, *refs, Q=Q)


def _ssd_row(bi, xbc_ref, zs_ref, dt_ref, dtT_ref, conv0_ref, ssm0_ref, cw_ref, cb_ref,
             alog_ref, alogT_ref, dskip_ref, gssm_ref, e_ref,
             y_ref, convo_ref, ssmo_ref, hT_all, cbuf_all, *, Q):
    c = pl.program_id(1)
    hT, cbuf = hT_all.at[bi], cbuf_all.at[bi]
    tail = CONV_W - 1
    base = SUBLANES

    @pl.when(c == 0)
    def _():
        cbuf[base - tail:base, :] = conv0_ref[bi]
        for g in range(SSM_G):
            hT[g] = ssm0_ref[bi, g * SSM_GW:(g + 1) * SSM_GW, :].T

    cbuf[base:base + Q, :] = xbc_ref[bi].astype(f32)
    acc = cb_ref[...] + cw_ref[CONV_W - 1:CONV_W, :] * cbuf[base:base + Q, :]
    for j in range(tail):
        acc = acc + cw_ref[j:j + 1, :] * cbuf[pl.ds(base - tail + j, Q), :]
    xc = _silu(acc)
    convo_ref[bi] = cbuf[pl.ds(base + Q - tail, tail), :]
    cbuf[0:SUBLANES, :] = cbuf[Q:Q + SUBLANES, :]

    xs = xc[:, :D_INNER]
    xs_b = xs.astype(bf16)
    e_mat = e_ref[...]

    dt = dt_ref[bi]
    dtT = dtT_ref[bi]
    a = dt * (-jnp.exp(alog_ref[...]))
    aT = dtT * (-jnp.exp(alogT_ref[...]))
    r_i = lax.broadcasted_iota(i32, (Q, Q), 0)
    c_i = lax.broadcasted_iota(i32, (Q, Q), 1)
    tril = (c_i <= r_i)
    tril_b = tril.astype(bf16)
    triu_b = (r_i <= c_i).astype(bf16)
    acs = sum(jnp.dot(tril_b, part, preferred_element_type=f32) for part in _split3(a))
    acsT = sum(jnp.dot(part, triu_b, preferred_element_type=f32) for part in _split3(aT))
    acs_last = acs[Q - 1:Q, :]
    e_acs = jnp.exp(acs)
    w_end = jnp.exp(acs_last - acs) * dt

    lane = lax.broadcasted_iota(i32, (Q, LANES), 1)
    first_head = lane < SSM_P
    y_cols = []
    new_h = []
    for g in range(SSM_G):
        bg = xc[:, D_INNER + g * D_STATE:D_INNER + (g + 1) * D_STATE].astype(bf16)
        cg = xc[:, D_INNER + SSM_G * D_STATE + g * D_STATE:D_INNER + SSM_G * D_STATE + (g + 1) * D_STATE].astype(bf16)
        cbm = lax.dot_general(cg, bg, (((1,), (1,)), ((), ())), preferred_element_type=f32)
        y_inter = jnp.dot(cg, hT[g].astype(bf16), preferred_element_type=f32)
        pair_cols = []
        for pr in range(SSM_GW // LANES):
            xp = xs_b[:, g * SSM_GW + pr * LANES:g * SSM_GW + (pr + 1) * LANES]
            ys = []
            for hh in range(2):
                h = g * (SSM_GW // SSM_P) + 2 * pr + hh
                seg = acs[:, h:h + 1] - acsT[h:h + 1, :]
                lm = jnp.exp(jnp.where(tril, seg, NEG))
                mm = (cbm * lm * dtT[h:h + 1, :]).astype(bf16)
                ys.append(jnp.dot(mm, xp, preferred_element_type=f32))
            pair_cols.append(jnp.where(first_head, ys[0], ys[1]))
        y_cols.append((jnp.concatenate(pair_cols, axis=1), y_inter))
        new_h.append((bg, g))

    e_acs_x = jnp.dot(e_acs.astype(bf16), e_mat, preferred_element_type=f32)
    w_end_x = jnp.dot(w_end.astype(bf16), e_mat, preferred_element_type=f32)
    xw = (xs * w_end_x).astype(bf16)
    dec8 = jnp.broadcast_to(jnp.exp(acs_last), (SUBLANES, LANES))
    dec_hi = dec8.astype(bf16)
    dec_lo = (dec8 - dec_hi.astype(f32)).astype(bf16)
    dec_x = (jnp.dot(dec_hi, e_mat, preferred_element_type=f32)
             + jnp.dot(dec_lo, e_mat, preferred_element_type=f32))[0:1, :]

    y_parts = []
    for g in range(SSM_G):
        y_intra, y_inter = y_cols[g]
        sl = slice(g * SSM_GW, (g + 1) * SSM_GW)
        yg = y_intra + y_inter * e_acs_x[:, sl] + dskip_ref[:, sl] * xs[:, sl]
        yg = yg * zs_ref[bi][:, sl].astype(f32)
        y_parts.append(_rms(yg, gssm_ref[:, sl], SSM_EPS))
        bg = new_h[g][0]
        upd = lax.dot_general(bg, xw[:, sl], (((0,), (0,)), ((), ())), preferred_element_type=f32)
        hT[g] = hT[g] * dec_x[:, sl] + upd
    y_ref[bi] = jnp.concatenate(y_parts, axis=1).astype(bf16)

    @pl.when(c == pl.num_programs(1) - 1)
    def _():
        for g in range(SSM_G):
            ssmo_ref[bi, g * SSM_GW:(g + 1) * SSM_GW, :] = hT[g].T


def _ssd(xbc, zs, dt, conv0, ssm0, conv_w, conv_b, a_log, d_skip, g_ssm, e_mat):
    B, L, _ = xbc.shape
    Q = min(SSD_CHUNK, L)
    assert L % Q == 0 and Q % SUBLANES == 0
    nc = L // Q
    dtT = jnp.swapaxes(dt[:, :, :SSM_H], 1, 2)
    alog_row = jnp.pad(a_log.reshape(1, SSM_H), ((0, 0), (0, LANES - SSM_H)))
    alog_col = a_log.reshape(SSM_H, 1)
    dskip_x = jnp.repeat(d_skip, SSM_P).reshape(1, D_INNER)
    nb = SSD_SHORT_SEQS_PER_STEP if (nc == 1 and B % SSD_SHORT_SEQS_PER_STEP == 0) else 1
    kern = functools.partial(_ssd_kernel, Q=Q, nb=nb)
    cmap = lambda b, c: (0, 0)
    return pl.pallas_call(
        kern,
        out_shape=(jax.ShapeDtypeStruct((B, L, D_INNER), bf16),
                   jax.ShapeDtypeStruct((B, CONV_W - 1, CONV_DIM), f32),
                   jax.ShapeDtypeStruct((B, D_INNER, D_STATE), f32)),
        grid=(B // nb, nc),
        in_specs=[pl.BlockSpec((nb, Q, CONV_DIM), lambda b, c: (b, c, 0)),
                  pl.BlockSpec((nb, Q, D_INNER), lambda b, c: (b, c, 0)),
                  pl.BlockSpec((nb, Q, LANES), lambda b, c: (b, c, 0)),
                  pl.BlockSpec((nb, SSM_H, Q), lambda b, c: (b, 0, c)),
                  pl.BlockSpec((nb, CONV_W - 1, CONV_DIM), lambda b, c: (b, 0, 0)),
                  pl.BlockSpec((nb, D_INNER, D_STATE), lambda b, c: (b, 0, 0)),
                  pl.BlockSpec((CONV_W, CONV_DIM), cmap), pl.BlockSpec((1, CONV_DIM), cmap),
                  pl.BlockSpec((1, LANES), cmap), pl.BlockSpec((SSM_H, 1), cmap),
                  pl.BlockSpec((1, D_INNER), cmap), pl.BlockSpec((1, D_INNER), cmap),
                  pl.BlockSpec((LANES, D_INNER), cmap)],
        out_specs=(pl.BlockSpec((nb, Q, D_INNER), lambda b, c: (b, c, 0)),
                   pl.BlockSpec((nb, CONV_W - 1, CONV_DIM), lambda b, c: (b, 0, 0)),
                   pl.BlockSpec((nb, D_INNER, D_STATE), lambda b, c: (b, 0, 0))),
        scratch_shapes=[pltpu.VMEM((nb, SSM_G, D_STATE, SSM_GW), f32),
                        pltpu.VMEM((nb, SUBLANES + Q, CONV_DIM), f32)],
        compiler_params=_params(("parallel", "arbitrary")),
        name="ssd_scan",
    )(xbc, zs, dt, dtT, conv0, ssm0, conv_w, conv_b.reshape(1, CONV_DIM), alog_row, alog_col,
      dskip_x, g_ssm.reshape(1, D_INNER), e_mat)


def _route(lg):
    lane = lax.broadcasted_iota(i32, lg.shape, 1)
    lanef = lane.astype(f32)
    big = float(LANES)
    gl = jnp.where(lane < N_EGROUPS, lg, NEG)
    gmax = jnp.max(gl, axis=1, keepdims=True)
    gsel = jnp.min(jnp.where(gl == gmax, lanef, big), axis=1, keepdims=True)
    p_g = 1.0 / jnp.sum(jnp.exp(gl - gmax), axis=1, keepdims=True)
    first = N_EGROUPS + E_PER_GROUP * gsel
    emask = (lanef >= first) & (lanef < first + E_PER_GROUP)
    el = jnp.where(emask, lg, NEG)
    v1 = jnp.max(el, axis=1, keepdims=True)
    i1 = jnp.min(jnp.where(emask & (el == v1), lanef, big), axis=1, keepdims=True)
    emask2 = emask & (lanef != i1)
    el2 = jnp.where(emask2, lg, NEG)
    v2 = jnp.max(el2, axis=1, keepdims=True)
    i2 = jnp.min(jnp.where(emask2 & (el2 == v2), lanef, big), axis=1, keepdims=True)
    t = jnp.exp(v2 - v1)
    w1 = p_g / (1.0 + t)
    w2 = p_g * t / (1.0 + t)
    out = jnp.where(lane == 0, i1 - N_EGROUPS, 0.0)
    out = jnp.where(lane == 1, i2 - N_EGROUPS, out)
    out = jnp.where(lane == 2, w1, out)
    out = jnp.where(lane == 3, w2, out)
    return out


def _to_row_tiles(ref, x, rows):
    for j in range(D_MODEL // LANES):
        ref[pl.ds(j, rows, stride=SUBLANES), :] = x[:, j * LANES:(j + 1) * LANES]


def _from_row_tiles(ref, rows):
    return jnp.concatenate([ref[pl.ds(j, rows, stride=SUBLANES), :] for j in range(D_MODEL // LANES)], axis=1)


def _mix_kernel(x_ref, gt1_ref, sc2_ref, sh2_ref, o_ref, ys_ref, ga_ref, gs_ref,
                woa_ref, wos_ref, wo_ref, gpost_ref, gpre_ref, wr_ref, br_ref,
                x1_ref, u2_ref, route_ref):
    tm = x_ref.shape[0]
    ya = jnp.dot(o_ref[...], woa_ref[...], preferred_element_type=f32)
    ys = jnp.dot(ys_ref[...], wos_ref[...], preferred_element_type=f32)
    m = ga_ref[...].astype(f32) * ya + gs_ref[...].astype(f32) * ys
    mo = jnp.dot(m.astype(bf16), wo_ref[...], preferred_element_type=f32)
    x1 = x_ref[...] + gt1_ref[...] * _rms(mo, gpost_ref[...], EPS)
    x1_ref[...] = x1
    u2 = _rms(x1, gpre_ref[...], EPS) * (1.0 + sc2_ref[...]) + sh2_ref[...]
    _to_row_tiles(u2_ref, u2, tm)
    lg = jnp.dot(u2.astype(bf16), wr_ref[...], preferred_element_type=f32) + br_ref[...]
    route_ref[...] = _route(lg)


def _mix(grp, x2, o, ysn, ga, gs, w_oa, w_os, w_o, g_post1, g_pre2, w_r, b_r):
    T, tm = grp.T, grp.tm
    return pl.pallas_call(
        _mix_kernel,
        out_shape=(jax.ShapeDtypeStruct((T, D_MODEL), f32),
                   jax.ShapeDtypeStruct((T * SUBLANES, LANES), f32),
                   jax.ShapeDtypeStruct((T, LANES), f32)),
        grid=(grp.n_tiles,),
        in_specs=[grp.row_spec(D_MODEL), grp.mod_spec(2), grp.mod_spec(4), grp.mod_spec(3),
                  grp.row_spec(ATTN_W), grp.row_spec(D_INNER), grp.row_spec(D_MODEL), grp.row_spec(D_MODEL),
                  _const_spec(w_oa.shape), _const_spec(w_os.shape), _const_spec(w_o.shape),
                  _const_spec((1, D_MODEL)), _const_spec((1, D_MODEL)),
                  _const_spec(w_r.shape), _const_spec((1, LANES))],
        out_specs=(grp.row_spec(D_MODEL),
                   pl.BlockSpec((tm * SUBLANES, LANES), lambda i: (i, 0)),
                   grp.row_spec(LANES)),
        compiler_params=_params(("parallel",)),
        name="mixer_out",
    )(x2, grp.mod, grp.mod, grp.mod, o, ysn, ga, gs, w_oa, w_os, w_o, g_post1, g_pre2, w_r, b_r)


def _count_kernel(route_ref, cnt_ref):
    i = pl.program_id(0)

    @pl.when(i == 0)
    def _():
        cnt_ref[...] = jnp.zeros(cnt_ref.shape, f32)

    r = route_ref[...]
    lanef = lax.broadcasted_iota(i32, r.shape, 1).astype(f32)
    hits = (lanef == r[:, 0:1]).astype(f32) + (lanef == r[:, 1:2]).astype(f32)
    cnt_ref[...] += jnp.sum(hits, axis=0, keepdims=True)


def _expert_counts(route, tm):
    T = route.shape[0]
    return pl.pallas_call(
        _count_kernel,
        out_shape=jax.ShapeDtypeStruct((1, LANES), f32),
        grid=(T // tm,),
        in_specs=[pl.BlockSpec((tm, LANES), lambda i: (i, 0))],
        out_specs=pl.BlockSpec((1, LANES), lambda i: (0, 0)),
        compiler_params=_params(("arbitrary",)),
        name="moe_count",
    )(route)


def _slot_kernel(route_ref, start_ref, dest_ref, carry):
    i = pl.program_id(0)

    @pl.when(i == 0)
    def _():
        carry[...] = jnp.broadcast_to(start_ref[...], carry.shape)

    r = route_ref[...]
    tm = r.shape[0]
    lane = lax.broadcasted_iota(i32, r.shape, 1)
    lanef = lane.astype(f32)
    oh1 = (lanef == r[:, 0:1]).astype(f32)
    oh2 = (lanef == r[:, 1:2]).astype(f32)
    cnt = oh1 + oh2
    ri = lax.broadcasted_iota(i32, (tm, tm), 0)
    ci = lax.broadcasted_iota(i32, (tm, tm), 1)
    before = (ci < ri).astype(bf16)
    pos = jnp.dot(before, cnt.astype(bf16), preferred_element_type=f32) + carry[0:1, :]
    d1 = jnp.sum(oh1 * pos, axis=1, keepdims=True)
    d2 = jnp.sum(oh2 * pos, axis=1, keepdims=True)
    dest_ref[...] = jnp.where(lane == 0, d1, jnp.where(lane == 1, d2, 0.0)).astype(i32)
    carry[...] = carry[...] + jnp.sum(cnt, axis=0, keepdims=True)


def _slots(route, starts, tm):
    T = route.shape[0]
    return pl.pallas_call(
        _slot_kernel,
        out_shape=jax.ShapeDtypeStruct((T, LANES), i32),
        grid=(T // tm,),
        in_specs=[pl.BlockSpec((tm, LANES), lambda i: (i, 0)), pl.BlockSpec((1, LANES), lambda i: (0, 0))],
        out_specs=pl.BlockSpec((tm, LANES), lambda i: (i, 0)),
        scratch_shapes=[pltpu.VMEM((SUBLANES, LANES), f32)],
        compiler_params=_params(("arbitrary",)),
        name="moe_slots",
    )(route, starts)


def _row_copy(src, src_row, dst, dst_row, sem):
    return pltpu.make_async_copy(src.at[pl.ds(pl.multiple_of(src_row * SUBLANES, SUBLANES), SUBLANES), :],
                                 dst.at[pl.ds(pl.multiple_of(dst_row * SUBLANES, SUBLANES), SUBLANES), :], sem)


def _dispatch_kernel(dest_ref, u_ref, xs_in_ref, xs_ref, sem, *, tm):
    del xs_in_ref

    def issue(t, c):
        for k in range(2):
            _row_copy(u_ref, t, xs_ref, dest_ref[0, 0, 2 * t + k], sem).start(priority=k)
        return c

    lax.fori_loop(0, tm, issue, 0)
    rows = 2 * tm * SUBLANES
    pltpu.make_async_copy(xs_ref.at[pl.ds(0, rows), :], xs_ref.at[pl.ds(0, rows), :], sem).wait()


def _dispatch(dest2, u_rows, n_slots, tm):
    T = dest2.shape[0]
    xs0 = jnp.zeros((n_slots * SUBLANES, LANES), f32)
    dest3 = dest2.reshape(T // tm, 1, 2 * tm)
    return pl.pallas_call(
        functools.partial(_dispatch_kernel, tm=tm),
        out_shape=jax.ShapeDtypeStruct((n_slots * SUBLANES, LANES), f32),
        grid=(T // tm,),
        in_specs=[pl.BlockSpec((1, 1, 2 * tm), lambda i: (i, 0, 0), memory_space=pltpu.SMEM),
                  pl.BlockSpec((tm * SUBLANES, LANES), lambda i: (i, 0)), pl.BlockSpec(memory_space=pl.ANY)],
        out_specs=pl.BlockSpec(memory_space=pl.ANY),
        scratch_shapes=[pltpu.SemaphoreType.DMA(())],
        input_output_aliases={2: 0},
        compiler_params=_params(("arbitrary",)),
        name="moe_dispatch",
    )(dest3, u_rows, xs0)


def _expert_kernel(be_ref, xs_ref, wg_ref, wu_ref, wd_ref, o_ref, *, blk):
    del be_ref
    x = _from_row_tiles(xs_ref, blk).astype(bf16)
    h = _silu(jnp.dot(x, wg_ref[...], preferred_element_type=f32)) * jnp.dot(x, wu_ref[...], preferred_element_type=f32)
    _to_row_tiles(o_ref, jnp.dot(h.astype(bf16), wd_ref[...], preferred_element_type=f32), blk)


def _experts(xs, blk_e, w_gate, w_up, w_down, blk):
    n_blocks = blk_e.shape[0]
    rows = blk * SUBLANES
    grid_spec = pltpu.PrefetchScalarGridSpec(
        num_scalar_prefetch=1,
        grid=(n_blocks,),
        in_specs=[pl.BlockSpec((rows, LANES), lambda i, be: (i, 0)),
                  pl.BlockSpec((None, D_MODEL, D_FF), lambda i, be: (be[i], 0, 0)),
                  pl.BlockSpec((None, D_MODEL, D_FF), lambda i, be: (be[i], 0, 0)),
                  pl.BlockSpec((None, D_FF, D_MODEL), lambda i, be: (be[i], 0, 0))],
        out_specs=pl.BlockSpec((rows, LANES), lambda i, be: (i, 0)),
    )
    return pl.pallas_call(
        functools.partial(_expert_kernel, blk=blk),
        out_shape=jax.ShapeDtypeStruct(xs.shape, f32),
        grid_spec=grid_spec,
        compiler_params=_params(("parallel",)),
        name="moe_experts",
    )(blk_e, xs, w_gate, w_up, w_down)


def _combine_kernel(dest_ref, dest_next_ref, outs_ref, x1_ref, gt2_ref, route_ref, gpost_ref, y_ref, buf, sem, *, tm):
    i = pl.program_id(0)
    slot = jnp.bitwise_and(i, 1)

    def gather(d_ref, s):
        def issue(t, c):
            for k in range(2):
                _row_copy(outs_ref, d_ref[0, 0, 2 * t + k], buf.at[s, k], t, sem.at[s]).start()
            return c
        lax.fori_loop(0, tm, issue, 0)

    @pl.when(i == 0)
    def _():
        gather(dest_ref, 0)

    @pl.when(i + 1 < pl.num_programs(0))
    def _():
        gather(dest_next_ref, 1 - slot)

    pltpu.make_async_copy(buf.at[slot], buf.at[slot], sem.at[slot]).wait()
    r = route_ref[...]
    f = r[:, 2:3] * _from_row_tiles(buf.at[slot, 0], tm) + r[:, 3:4] * _from_row_tiles(buf.at[slot, 1], tm)
    y_ref[...] = x1_ref[...] + gt2_ref[...] * _rms(f, gpost_ref[...], EPS)


def _combine(grp, dest2, outs, x1, route, g_post2, tm):
    T = grp.T
    n = T // tm
    dest3 = dest2.reshape(n, 1, 2 * tm)
    return pl.pallas_call(
        functools.partial(_combine_kernel, tm=tm),
        out_shape=jax.ShapeDtypeStruct((T, D_MODEL), f32),
        grid=(n,),
        in_specs=[pl.BlockSpec((1, 1, 2 * tm), lambda i: (i, 0, 0), memory_space=pltpu.SMEM),
                  pl.BlockSpec((1, 1, 2 * tm), lambda i: (jnp.minimum(i + 1, n - 1), 0, 0), memory_space=pltpu.SMEM),
                  pl.BlockSpec(memory_space=pl.ANY),
                  grp.row_spec(D_MODEL), grp.mod_spec(5), grp.row_spec(LANES), _const_spec((1, D_MODEL))],
        out_specs=grp.row_spec(D_MODEL),
        scratch_shapes=[pltpu.VMEM((2, 2, tm * SUBLANES, LANES), f32), pltpu.SemaphoreType.DMA((2,))],
        compiler_params=_params(("arbitrary",)),
        name="moe_combine",
    )(dest3, dest3, outs, x1, grp.mod, route, g_post2)


def _hmoe_and_residual(grp, x1, u_rows, route, w_gate, w_up, w_down, g_post2):
    T = grp.T
    tm = grp.tm
    blk = MOE_BLOCK if 2 * T >= N_EXPERTS * MOE_BLOCK else MOE_BLOCK_SMALL
    counts = _expert_counts(route, tm)[0, :N_EXPERTS].astype(i32)
    padded = (counts + blk - 1) // blk * blk
    ends = jnp.cumsum(padded)
    starts = ends - padded
    n_blocks = -(-2 * T // blk) + N_EXPERTS
    n_slots = n_blocks * blk
    blk_start = jnp.arange(n_blocks, dtype=i32) * blk
    blk_e = jnp.minimum(jnp.sum(blk_start[:, None] >= ends[None, :], axis=1), N_EXPERTS - 1).astype(i32)
    starts_row = jnp.pad(starts.astype(f32), (0, LANES - N_EXPERTS)).reshape(1, LANES)
    dest2 = _slots(route, starts_row, tm)[:, :2]
    xs = _dispatch(dest2, u_rows, n_slots, tm)
    outs = _experts(xs, blk_e, w_gate, w_up, w_down, blk)
    return _combine(grp, dest2, outs, x1, route, g_post2, tm)


def _rope_tables(pos):
    half = HEAD_DIM // 2
    inv = ROPE_THETA ** (-2.0 * jnp.arange(half, dtype=f32) / HEAD_DIM)
    ang = pos.astype(f32)[:, None] * inv[None, :]
    cos, sin = jnp.cos(ang), jnp.sin(ang)
    return jnp.tile(cos, (1, LANES // half)), jnp.tile(jnp.concatenate([-sin, sin], axis=1), (1, LANES // HEAD_DIM))


def _layer(grp, x2, lw, lam_init, attend, conv0, ssm0):
    n_seq, seq_len = grp.n_seq, grp.seq_len
    q, k, kb, v, vb = _proj_qkv(grp, x2, lw["g_pre1"], lw["w_qkv"], 1, 0)
    zs, xbc = _proj_zx(grp, x2, lw["g_pre1"], lw["w_z"], lw["w_xbc"], 1, 0)
    ga, gs, dt = _proj_gates(grp, x2, lw["g_pre1"], lw["w_gates"], lw["w_dt"], lw["dt_bias"], 1, 0)
    shp = lambda a: a.reshape(n_seq, seq_len, a.shape[-1])
    o = attend(shp(q), shp(kb), shp(vb))
    ysn, conv_new, ssm_new = _ssd(shp(xbc), shp(zs), shp(dt), conv0, ssm0.reshape(n_seq, D_INNER, D_STATE),
                                  lw["conv_w"], lw["conv_b"], lw["a_log"], lw["d_skip"], lw["g_ssm"], lw["e_mat"])
    x1, u_rows, route = _mix(grp, x2, o.reshape(grp.T, ATTN_W), ysn.reshape(grp.T, D_INNER), ga, gs,
                             lw["w_oa"], lw["w_os"], lw["w_o"], lw["g_post1"], lw["g_pre2"], lw["w_r"], lw["b_r"])
    y2 = _hmoe_and_residual(grp, x1, u_rows, route, lw["w_gate"], lw["w_up"], lw["w_down"], lw["g_post2"])
    return (y2, k.reshape(n_seq, seq_len, N_KV, HEAD_W), v.reshape(n_seq, seq_len, N_KV, HEAD_W), conv_new,
            ssm_new.reshape(n_seq, SSM_H, SSM_P, D_STATE))


def kernel(x_prompt, x_sample, cache_k, cache_v, state_conv, state_ssm, page_table, c_prompt, c_sample, w_ada, b_ada, g_pre1, g_post1, g_pre2, g_post2, w_in, lambda_q1, lambda_k1, lambda_q2, lambda_k2, g_subln, w_oa, conv_w, conv_b, dt_bias, a_log, d_skip, g_ssm, w_os, w_o, w_rg, b_rg, w_re, b_re, w_gate, w_up, w_down):
    depth = w_in.shape[0]
    bp, lp_len, _ = x_prompt.shape
    bs, ls_len, _ = x_sample.shape
    past = page_table.shape[1] * cache_k.shape[2]
    ck = cache_k.reshape(cache_k.shape[0], cache_k.shape[1], cache_k.shape[2] * N_KV, HEAD_W)
    cv = cache_v.reshape(cache_v.shape[0], cache_v.shape[1], cache_v.shape[2] * N_KV, HEAD_W)
    rope_p = _rope_tables(jnp.arange(lp_len, dtype=i32))
    rope_s8 = _rope_tables(past + jnp.arange(ls_len, dtype=i32))
    rope_s = tuple(jnp.tile(t, (bs, 1)) for t in rope_s8)
    e_mat = (jnp.arange(LANES, dtype=i32)[:, None] == jnp.arange(D_INNER, dtype=i32)[None, :] // SSM_P).astype(bf16)

    hp = x_prompt.reshape(bp * lp_len, D_MODEL)
    hs = x_sample.reshape(bs * ls_len, D_MODEL)
    c_all = jnp.concatenate([c_prompt, c_sample], axis=0)
    outs = [[] for _ in range(8)]
    for l in range(depth):
        lam_init = 0.8 - 0.6 * math.exp(-0.3 * l)
        w_in_l = w_in[l].astype(bf16)
        lw = dict(
            w_qkv=w_in_l[:, :OFF_Z], w_z=w_in_l[:, OFF_Z:OFF_XBC], w_xbc=w_in_l[:, OFF_XBC:OFF_DT],
            w_dt=jnp.pad(w_in_l[:, OFF_DT:OFF_GA], ((0, 0), (0, LANES - SSM_H))), w_gates=w_in_l[:, OFF_GA:],
            dt_bias=jnp.pad(dt_bias[l].reshape(1, SSM_H), ((0, 0), (0, LANES - SSM_H))),
            g_pre1=g_pre1[l].reshape(1, D_MODEL), g_post1=g_post1[l].reshape(1, D_MODEL),
            g_pre2=g_pre2[l].reshape(1, D_MODEL), g_post2=g_post2[l].reshape(1, D_MODEL),
            w_oa=w_oa[l].astype(bf16), w_os=w_os[l].astype(bf16), w_o=w_o[l].astype(bf16),
            conv_w=conv_w[l], conv_b=conv_b[l], a_log=a_log[l], d_skip=d_skip[l], g_ssm=g_ssm[l], e_mat=e_mat,
            w_r=jnp.pad(jnp.concatenate([w_rg[l], w_re[l]], axis=1),
                        ((0, 0), (0, LANES - N_EGROUPS - N_EXPERTS))).astype(bf16),
            b_r=jnp.pad(jnp.concatenate([b_rg[l], b_re[l]]), (0, LANES - N_EGROUPS - N_EXPERTS)).reshape(1, LANES),
            w_gate=w_gate[l].astype(bf16), w_up=w_up[l].astype(bf16), w_down=w_down[l].astype(bf16),
        )
        lam_p = jnp.stack([lambda_q1[l], lambda_k1[l], lambda_q2[l], lambda_k2[l]])
        gsub = g_subln[l].reshape(1, HEAD_W)
        mod = _ada(c_all, w_ada[l].astype(bf16), b_ada[l])
        grp_p = _Group(bp, lp_len, mod[:bp], rope_p, per_token=False)
        grp_s = _Group(bs, ls_len, jnp.repeat(mod[bp:], ls_len, axis=0), rope_s, per_token=True)

        attend_p = lambda q, kb, vb: _attn_prompt(q, kb, vb, lam_p, gsub, lam_init)
        attend_s = lambda q, kb, vb: _attn_sample(q, kb, vb, ck, cv, page_table, l, lam_p, gsub, lam_init)
        conv0 = jnp.zeros((bp, CONV_W - 1, CONV_DIM), f32)
        ssm0 = jnp.zeros((bp, SSM_H, SSM_P, D_STATE), f32)
        hp, kp, vp, cp, sp = _layer(grp_p, hp, lw, lam_init, attend_p, conv0, ssm0)
        hs, ksm, vsm, csm, ssm = _layer(grp_s, hs, lw, lam_init, attend_s, state_conv[l], state_ssm[l])
        for lst, val in zip(outs, (kp, vp, cp, sp, ksm, vsm, csm, ssm)):
            lst.append(val)
    stacked = [jnp.stack(o) for o in outs]
    return (hp.reshape(bp, lp_len, D_MODEL), hs.reshape(bs, ls_len, D_MODEL), *stacked)
```

```python
import functools
import math

import jax
import jax.numpy as jnp
from jax import lax
from jax.experimental import pallas as pl
from jax.experimental.pallas import tpu as pltpu

f32 = jnp.float32
bf16 = jnp.bfloat16
i32 = jnp.int32

D_MODEL = 1024
N_HEADS = 8
N_KV = 4
HEAD_DIM = 64
HEAD_W = 2 * HEAD_DIM
ATTN_W = N_HEADS * HEAD_W
KV_W = N_KV * HEAD_W
ATTN_SCALE = HEAD_DIM ** -0.5
ROPE_THETA = 10000.0
SUBLN_EPS = 1e-5
D_INNER = 2048
SSM_P = 64
SSM_H = D_INNER // SSM_P
SSM_G = 4
SSM_GW = D_INNER // SSM_G
D_STATE = 128
CONV_W = 4
CONV_DIM = D_INNER + 2 * SSM_G * D_STATE
SSD_CHUNK = 128
SSM_EPS = 1e-5
N_EGROUPS = 4
E_PER_GROUP = 8
N_EXPERTS = 32
D_FF = 512
EPS = 1e-6
OFF_K = ATTN_W
OFF_V = OFF_K + KV_W
OFF_Z = OFF_V + KV_W
OFF_XBC = OFF_Z + D_INNER
OFF_DT = OFF_XBC + CONV_DIM
OFF_GA = OFF_DT + SSM_H
OFF_GS = OFF_GA + D_MODEL

LANES = 128
SUBLANES = 8
VMEM_LIMIT = 56 * 1024 * 1024
NEG = -1e30
LOG2E = math.log2(math.e)

TOKEN_TILE = 512
ATTN_TILE = 512
PAGES_PER_STEP = 32
MOE_BLOCK = 512
MOE_BLOCK_SMALL = 128
SSD_SHORT_SEQS_PER_STEP = 4


def _params(sem, vmem=VMEM_LIMIT):
    return pltpu.CompilerParams(dimension_semantics=sem, vmem_limit_bytes=vmem)


def _sigmoid(x):
    return 1.0 / (1.0 + jnp.exp(-x))


def _silu(x):
    return x * _sigmoid(x)


def _rms(x, g, eps):
    return x * lax.rsqrt(jnp.mean(x * x, axis=-1, keepdims=True) + eps) * g


def _bdot(a, b):
    return jnp.dot(a.astype(bf16), b.astype(bf16), preferred_element_type=f32)


class _Group:
    def __init__(self, n_seq, seq_len, mod_rows, rope_rows, per_token):
        self.n_seq, self.seq_len = n_seq, seq_len
        self.T = n_seq * seq_len
        self.tm = min(TOKEN_TILE, self.T if per_token else seq_len)
        assert self.T % self.tm == 0
        self.per_token = per_token
        if per_token:
            self.mod = mod_rows
            self.rope = rope_rows
        else:
            assert seq_len % self.tm == 0
            self.tiles_per_seq = seq_len // self.tm
            self.mod = mod_rows.reshape(n_seq, 1, 6 * D_MODEL)
            self.rope = rope_rows
        self.n_tiles = self.T // self.tm

    def mod_spec(self, col):
        if self.per_token:
            return pl.BlockSpec((self.tm, D_MODEL), lambda i: (i, col))
        tps = self.tiles_per_seq
        return pl.BlockSpec((None, 1, D_MODEL), lambda i: (i // tps, 0, col))

    def rope_spec(self):
        if self.per_token:
            return pl.BlockSpec((self.tm, LANES), lambda i: (i, 0))
        tps = self.tiles_per_seq
        return pl.BlockSpec((self.tm, LANES), lambda i: (i % tps, 0))

    def row_spec(self, width):
        return pl.BlockSpec((self.tm, width), lambda i: (i, 0))


def _const_spec(shape):
    nd = len(shape)
    return pl.BlockSpec(shape, lambda i: (0,) * nd)


def _ada_kernel(c_ref, w_ref, b_ref, o_ref):
    o_ref[...] = _bdot(_silu(c_ref[...]), w_ref[...]) + b_ref[...]


def _ada(c, w_ada, b_ada):
    m = c.shape[0]
    n = w_ada.shape[1]
    tn = D_MODEL
    return pl.pallas_call(
        _ada_kernel,
        out_shape=jax.ShapeDtypeStruct((m, n), f32),
        grid=(n // tn,),
        in_specs=[pl.BlockSpec((m, D_MODEL), lambda j: (0, 0)),
                  pl.BlockSpec((D_MODEL, tn), lambda j: (0, j)),
                  pl.BlockSpec((1, tn), lambda j: (0, j))],
        out_specs=pl.BlockSpec((m, tn), lambda j: (0, j)),
        compiler_params=_params(("parallel",)),
        name="ada_mod",
    )(c, w_ada, b_ada.reshape(1, n))


def _modulated(x_ref, g_ref, sc_ref, sh_ref):
    u = _rms(x_ref[...], g_ref[...], EPS) * (1.0 + sc_ref[...]) + sh_ref[...]
    return u.astype(bf16)


def _rope_cols(x, n_blocks, cos, sin_signed, lo):
    out = []
    for j in range(n_blocks):
        xb = x[:, j * LANES:(j + 1) * LANES]
        partner = jnp.where(lo, pltpu.roll(xb, LANES - HEAD_DIM // 2, 1), pltpu.roll(xb, HEAD_DIM // 2, 1))
        out.append(xb * cos + partner * sin_signed)
    return jnp.concatenate(out, axis=1)


def _qkv_kernel(x_ref, g_ref, sc_ref, sh_ref, w_ref, cos_ref, sin_ref,
                q_ref, k_ref, kb_ref, v_ref, vb_ref):
    u = _modulated(x_ref, g_ref, sc_ref, sh_ref)
    proj = jnp.dot(u, w_ref[...], preferred_element_type=f32)
    cos, sin_signed = cos_ref[...], sin_ref[...]
    lane = lax.broadcasted_iota(i32, cos.shape, 1)
    lo = jnp.bitwise_and(lane, HEAD_DIM - 1) < HEAD_DIM // 2
    q = _rope_cols(proj[:, :OFF_K], N_HEADS, cos, sin_signed, lo)
    k = _rope_cols(proj[:, OFF_K:OFF_V], N_KV, cos, sin_signed, lo)
    v = proj[:, OFF_V:OFF_Z]
    q_ref[...] = (q * (ATTN_SCALE * LOG2E)).astype(bf16)
    kb_ref[...] = k.astype(bf16)
    vb_ref[...] = v.astype(bf16)
    tm = k.shape[0]
    for h in range(N_KV):
        k_ref[pl.ds(h, tm, stride=N_KV), :] = k[:, h * HEAD_W:(h + 1) * HEAD_W]
        v_ref[pl.ds(h, tm, stride=N_KV), :] = v[:, h * HEAD_W:(h + 1) * HEAD_W]


def _proj_qkv(grp, x2, g, w_qkv, col_scale, col_shift):
    T, tm = grp.T, grp.tm
    head_rows = pl.BlockSpec((tm * N_KV, HEAD_W), lambda i: (i, 0))
    return pl.pallas_call(
        _qkv_kernel,
        out_shape=(jax.ShapeDtypeStruct((T, ATTN_W), bf16),
                   jax.ShapeDtypeStruct((T * N_KV, HEAD_W), f32), jax.ShapeDtypeStruct((T, KV_W), bf16),
                   jax.ShapeDtypeStruct((T * N_KV, HEAD_W), f32), jax.ShapeDtypeStruct((T, KV_W), bf16)),
        grid=(grp.n_tiles,),
        in_specs=[grp.row_spec(D_MODEL), _const_spec((1, D_MODEL)),
                  grp.mod_spec(col_scale), grp.mod_spec(col_shift),
                  _const_spec(w_qkv.shape), grp.rope_spec(), grp.rope_spec()],
        out_specs=(grp.row_spec(ATTN_W), head_rows, grp.row_spec(KV_W),
                   head_rows, grp.row_spec(KV_W)),
        compiler_params=_params(("parallel",)),
        name="proj_qkv",
    )(x2, g, grp.mod, grp.mod, w_qkv, grp.rope[0], grp.rope[1])


def _zx_kernel(x_ref, g_ref, sc_ref, sh_ref, wz_ref, wx_ref, zs_ref, xbc_ref):
    u = _modulated(x_ref, g_ref, sc_ref, sh_ref)
    zs_ref[...] = _silu(jnp.dot(u, wz_ref[...], preferred_element_type=f32)).astype(bf16)
    xbc_ref[...] = jnp.dot(u, wx_ref[...], preferred_element_type=f32).astype(bf16)


def _proj_zx(grp, x2, g, w_z, w_xbc, col_scale, col_shift):
    T = grp.T
    return pl.pallas_call(
        _zx_kernel,
        out_shape=(jax.ShapeDtypeStruct((T, D_INNER), bf16), jax.ShapeDtypeStruct((T, CONV_DIM), bf16)),
        grid=(grp.n_tiles,),
        in_specs=[grp.row_spec(D_MODEL), _const_spec((1, D_MODEL)),
                  grp.mod_spec(col_scale), grp.mod_spec(col_shift),
                  _const_spec(w_z.shape), _const_spec(w_xbc.shape)],
        out_specs=(grp.row_spec(D_INNER), grp.row_spec(CONV_DIM)),
        compiler_params=_params(("parallel",)),
        name="proj_zx",
    )(x2, g, grp.mod, grp.mod, w_z, w_xbc)


def _gates_kernel(x_ref, g_ref, sc_ref, sh_ref, wg_ref, wdt_ref, dtb_ref, ga_ref, gs_ref, dt_ref):
    u = _modulated(x_ref, g_ref, sc_ref, sh_ref)
    gates = _sigmoid(jnp.dot(u, wg_ref[...], preferred_element_type=f32))
    ga_ref[...] = gates[:, :D_MODEL].astype(bf16)
    gs_ref[...] = gates[:, D_MODEL:].astype(bf16)
    raw = jnp.dot(u, wdt_ref[...], preferred_element_type=f32) + dtb_ref[...]
    dt_ref[...] = jnp.maximum(raw, 0.0) + jnp.log1p(jnp.exp(-jnp.abs(raw)))


def _proj_gates(grp, x2, g, w_gates, w_dt, dt_bias, col_scale, col_shift):
    T = grp.T
    return pl.pallas_call(
        _gates_kernel,
        out_shape=(jax.ShapeDtypeStruct((T, D_MODEL), bf16), jax.ShapeDtypeStruct((T, D_MODEL), bf16),
                   jax.ShapeDtypeStruct((T, LANES), f32)),
        grid=(grp.n_tiles,),
        in_specs=[grp.row_spec(D_MODEL), _const_spec((1, D_MODEL)),
                  grp.mod_spec(col_scale), grp.mod_spec(col_shift),
                  _const_spec(w_gates.shape), _const_spec(w_dt.shape), _const_spec((1, LANES))],
        out_specs=(grp.row_spec(D_MODEL), grp.row_spec(D_MODEL), grp.row_spec(LANES)),
        compiler_params=_params(("parallel",)),
        name="proj_gates",
    )(x2, g, grp.mod, grp.mod, w_gates, w_dt, dt_bias)


def _lambda_full(lam_ref, lam_init):
    lp = lam_ref[...]
    l1 = jnp.sum(lp[0:1] * lp[1:2], axis=1, keepdims=True)
    l2 = jnp.sum(lp[2:3] * lp[3:4], axis=1, keepdims=True)
    return jnp.exp(l1) - jnp.exp(l2) + lam_init


def _with_ones(v):
    return jnp.concatenate([v, jnp.ones(v.shape, bf16)], axis=1)


def _softmax_step(s, v1, m_ref, acc_ref):
    reps = s.shape[1] // LANES if s.shape[1] % LANES == 0 else 0
    m_prev = m_ref[...]
    m_new = jnp.maximum(m_prev, jnp.max(s, axis=1, keepdims=True))
    alpha = jnp.exp2(m_prev - m_new)
    if reps:
        p = jnp.exp2(s - jnp.concatenate([m_new] * reps, axis=1))
    else:
        p = jnp.exp2(s - m_new[:, 0:1])
    acc_ref[...] = (jnp.concatenate([alpha, alpha], axis=1) * acc_ref[...]
                    + jnp.dot(p.astype(bf16), v1, preferred_element_type=f32))
    m_ref[...] = m_new


def _diff_heads(acc, lam, gsub, lam_init, rows):
    outs = []
    for r in range(2):
        a0 = acc[(2 * r) * rows:(2 * r + 1) * rows]
        a1 = acc[(2 * r + 1) * rows:(2 * r + 2) * rows]
        o = a0[:, :HEAD_W] / a0[:, HEAD_W:] - lam * (a1[:, :HEAD_W] / a1[:, HEAD_W:])
        outs.append(_rms(o, gsub, SUBLN_EPS) * (1.0 - lam_init))
    return jnp.concatenate(outs, axis=1)


def _pad_components(q, rows):
    lane = lax.broadcasted_iota(i32, (rows, LANES), 1)
    lo = lane < HEAD_DIM
    blocks = []
    for r in range(2):
        qh = q[:, r * LANES:(r + 1) * LANES].astype(f32)
        blocks.append(jnp.where(lo, qh, 0.0))
        blocks.append(jnp.where(lo, 0.0, qh))
    return jnp.concatenate(blocks, axis=0)


def _stacked_causal(n, copies=4):
    assert n & (n - 1) == 0
    row = jnp.bitwise_and(lax.broadcasted_iota(i32, (copies * n, n), 0), n - 1)
    col = lax.broadcasted_iota(i32, (copies * n, n), 1)
    return col <= row


def _attn_prompt_kernel(lam_ref, gsub_ref, q_ref, k_ref, v_ref, o_ref, qpad, m_s, acc_s, s_a, s_b, *, tq, lam_init):
    i = pl.program_id(2)
    qpad[...] = _pad_components(q_ref[0], tq).astype(bf16)
    m_s[...] = jnp.full(m_s.shape, NEG, f32)
    acc_s[...] = jnp.zeros(acc_s.shape, f32)

    def scores(c):
        kc = k_ref[0, pl.ds(pl.multiple_of(c * tq, tq), tq), :]
        return lax.dot_general(qpad[...], kc, (((1,), (1,)), ((), ())), preferred_element_type=f32)

    def consume(s_ref, c):
        _softmax_step(s_ref[...], _with_ones(v_ref[0, pl.ds(pl.multiple_of(c * tq, tq), tq), :]), m_s, acc_s)

    s_a[...] = jnp.where(_stacked_causal(tq), scores(i), NEG)

    def pair(p, has_next):
        s_b[...] = scores(2 * p)
        consume(s_a, jnp.where(p == 0, i, 2 * p - 1))
        if has_next:
            s_a[...] = scores(2 * p + 1)
        consume(s_b, 2 * p)

    def two_pairs(t, carry):
        pair(2 * t, True)
        pair(2 * t + 1, True)
        return carry

    odd = jnp.bitwise_and(i, 1)
    n_full = jnp.right_shift(i + 1, 1) - odd
    lax.fori_loop(0, jnp.right_shift(n_full, 1), two_pairs, 0)

    @pl.when(jnp.bitwise_and(n_full, 1) == 1)
    def _():
        pair(n_full - 1, True)

    @pl.when(odd == 1)
    def _():
        pair(n_full, False)

    @pl.when(odd == 0)
    def _():
        consume(s_a, jnp.maximum(i - 1, 0))

    lam = _lambda_full(lam_ref, lam_init)
    o_ref[0] = _diff_heads(acc_s[...], lam, gsub_ref[...], lam_init, tq).astype(bf16)


def _attn_prompt(q, kb, vb, lam_p, gsub, lam_init):
    B, L, _ = q.shape
    tq = min(ATTN_TILE, L)
    assert L % tq == 0
    kern = functools.partial(_attn_prompt_kernel, tq=tq, lam_init=lam_init)
    return pl.pallas_call(
        kern,
        out_shape=jax.ShapeDtypeStruct((B, L, ATTN_W), bf16),
        grid=(B, N_KV, L // tq),
        in_specs=[pl.BlockSpec((4, HEAD_DIM), lambda b, g, i: (0, 0)),
                  pl.BlockSpec((1, HEAD_W), lambda b, g, i: (0, 0)),
                  pl.BlockSpec((1, tq, 2 * HEAD_W), lambda b, g, i: (b, i, g)),
                  pl.BlockSpec((1, L, HEAD_W), lambda b, g, i: (b, 0, g)),
                  pl.BlockSpec((1, L, HEAD_W), lambda b, g, i: (b, 0, g))],
        out_specs=pl.BlockSpec((1, tq, 2 * HEAD_W), lambda b, g, i: (b, i, g)),
        scratch_shapes=[pltpu.VMEM((4 * tq, LANES), bf16), pltpu.VMEM((4 * tq, LANES), f32),
                        pltpu.VMEM((4 * tq, 2 * LANES), f32),
                        pltpu.VMEM((4 * tq, tq), f32), pltpu.VMEM((4 * tq, tq), f32)],
        compiler_params=_params(("parallel", "parallel", "arbitrary")),
        name="attn_prompt",
    )(lam_p, gsub, q, kb, vb)


def _attn_sample_kernel(pt_ref, lam_ref, gsub_ref, q_ref, kn_ref, vn_ref, *rest, n_pages_step, seq_new, lam_init):
    del pt_ref
    kp = rest[:n_pages_step]
    vp = rest[n_pages_step:2 * n_pages_step]
    o_ref, qpad, m_s, acc_s = rest[2 * n_pages_step:]
    j = pl.program_id(1)

    @pl.when(j == 0)
    def _():
        q = q_ref[0]
        for g in range(N_KV):
            qpad[g] = _pad_components(q[:, g * 2 * HEAD_W:(g + 1) * 2 * HEAD_W], seq_new)
        m_s[...] = jnp.full(m_s.shape, NEG, f32)
        acc_s[...] = jnp.zeros(acc_s.shape, f32)

    page = kp[0].shape[0] // N_KV
    for g in range(N_KV):
        qg = qpad[g].astype(bf16)
        kg = jnp.concatenate([r[pl.ds(g, page, stride=N_KV), :].astype(bf16) for r in kp], axis=0)
        vg = jnp.concatenate([r[pl.ds(g, page, stride=N_KV), :].astype(bf16) for r in vp], axis=0)
        s = lax.dot_general(qg, kg, (((1,), (1,)), ((), ())), preferred_element_type=f32)
        _softmax_step(s, _with_ones(vg), m_s.at[g], acc_s.at[g])

    @pl.when(j == pl.num_programs(1) - 1)
    def _():
        lam = _lambda_full(lam_ref, lam_init)
        causal = _stacked_causal(seq_new)
        outs = []
        for g in range(N_KV):
            qg = qpad[g].astype(bf16)
            kg = kn_ref[0][:, g * HEAD_W:(g + 1) * HEAD_W]
            vg = vn_ref[0][:, g * HEAD_W:(g + 1) * HEAD_W]
            s = lax.dot_general(qg, kg, (((1,), (1,)), ((), ())), preferred_element_type=f32)
            _softmax_step(jnp.where(causal, s, NEG), _with_ones(vg), m_s.at[g], acc_s.at[g])
            outs.append(_diff_heads(acc_s[g], lam, gsub_ref[...], lam_init, seq_new))
        o_ref[0] = jnp.concatenate(outs, axis=1).astype(bf16)


def _attn_sample(q, kb, vb, cache_k, cache_v, page_table, layer, lam_p, gsub, lam_init):
    S, Ls, _ = q.shape
    n_pages = page_table.shape[1]
    page_rows = cache_k.shape[2]
    pps = PAGES_PER_STEP
    while n_pages % pps:
        pps -= 1
    steps = n_pages // pps

    def page_spec(p):
        return pl.BlockSpec((None, None, page_rows, HEAD_W), lambda s, j, pt: (layer, pt[s, j * pps + p], 0, 0))

    kern = functools.partial(_attn_sample_kernel, n_pages_step=pps, seq_new=Ls, lam_init=lam_init)
    grid_spec = pltpu.PrefetchScalarGridSpec(
        num_scalar_prefetch=1,
        grid=(S, steps),
        in_specs=[pl.BlockSpec((4, HEAD_DIM), lambda s, j, pt: (0, 0)),
                  pl.BlockSpec((1, HEAD_W), lambda s, j, pt: (0, 0)),
                  pl.BlockSpec((1, Ls, ATTN_W), lambda s, j, pt: (s, 0, 0)),
                  pl.BlockSpec((1, Ls, KV_W), lambda s, j, pt: (s, 0, 0)),
                  pl.BlockSpec((1, Ls, KV_W), lambda s, j, pt: (s, 0, 0))]
        + [page_spec(p) for p in range(pps)] + [page_spec(p) for p in range(pps)],
        out_specs=pl.BlockSpec((1, Ls, ATTN_W), lambda s, j, pt: (s, 0, 0)),
        scratch_shapes=[pltpu.VMEM((N_KV, 4 * Ls, LANES), f32), pltpu.VMEM((N_KV, 4 * Ls, LANES), f32),
                        pltpu.VMEM((N_KV, 4 * Ls, 2 * LANES), f32)],
    )
    return pl.pallas_call(
        kern,
        out_shape=jax.ShapeDtypeStruct((S, Ls, ATTN_W), bf16),
        grid_spec=grid_spec,
        compiler_params=_params(("parallel", "arbitrary")),
        name="attn_sample",
    )(page_table, lam_p, gsub, q, kb, vb, *([cache_k] * pps), *([cache_v] * pps))


def _split3(x):
    hi = x.astype(bf16)
    r1 = x - hi.astype(f32)
    mid = r1.astype(bf16)
    lo = (r1 - mid.astype(f32)).astype(bf16)
    return hi, mid, lo


def _ssd_kernel(*refs, Q, nb):
    for bi in range(nb):
        _ssd_row(bi, *refs, Q=Q)


def _ssd_row(bi, xbc_ref, zs_ref, dt_ref, dtT_ref, conv0_ref, ssm0_ref, cw_ref, cb_ref,
             alog_ref, alogT_ref, dskip_ref, gssm_ref, e_ref,
             y_ref, convo_ref, ssmo_ref, hT_all, cbuf_all, *, Q):
    c = pl.program_id(1)
    hT, cbuf = hT_all.at[bi], cbuf_all.at[bi]
    tail = CONV_W - 1
    base = SUBLANES

    @pl.when(c == 0)
    def _():
        cbuf[base - tail:base, :] = conv0_ref[bi]
        for g in range(SSM_G):
            hT[g] = ssm0_ref[bi, g * SSM_GW:(g + 1) * SSM_GW, :].T

    cbuf[base:base + Q, :] = xbc_ref[bi].astype(f32)
    acc = cb_ref[...] + cw_ref[CONV_W - 1:CONV_W, :] * cbuf[base:base + Q, :]
    for j in range(tail):
        acc = acc + cw_ref[j:j + 1, :] * cbuf[pl.ds(base - tail + j, Q), :]
    xc = _silu(acc)
    convo_ref[bi] = cbuf[pl.ds(base + Q - tail, tail), :]
    cbuf[0:SUBLANES, :] = cbuf[Q:Q + SUBLANES, :]

    xs = xc[:, :D_INNER]
    xs_b = xs.astype(bf16)
    e_mat = e_ref[...]

    dt = dt_ref[bi]
    dtT = dtT_ref[bi]
    a = dt * (-jnp.exp(alog_ref[...]))
    aT = dtT * (-jnp.exp(alogT_ref[...]))
    r_i = lax.broadcasted_iota(i32, (Q, Q), 0)
    c_i = lax.broadcasted_iota(i32, (Q, Q), 1)
    tril = (c_i <= r_i)
    tril_b = tril.astype(bf16)
    triu_b = (r_i <= c_i).astype(bf16)
    acs = sum(jnp.dot(tril_b, part, preferred_element_type=f32) for part in _split3(a))
    acsT = sum(jnp.dot(part, triu_b, preferred_element_type=f32) for part in _split3(aT))
    acs_last = acs[Q - 1:Q, :]
    e_acs = jnp.exp(acs)
    w_end = jnp.exp(acs_last - acs) * dt

    lane = lax.broadcasted_iota(i32, (Q, LANES), 1)
    first_head = lane < SSM_P
    y_cols = []
    new_h = []
    for g in range(SSM_G):
        bg = xc[:, D_INNER + g * D_STATE:D_INNER + (g + 1) * D_STATE].astype(bf16)
        cg = xc[:, D_INNER + SSM_G * D_STATE + g * D_STATE:D_INNER + SSM_G * D_STATE + (g + 1) * D_STATE].astype(bf16)
        cbm = lax.dot_general(cg, bg, (((1,), (1,)), ((), ())), preferred_element_type=f32)
        y_inter = jnp.dot(cg, hT[g].astype(bf16), preferred_element_type=f32)
        pair_cols = []
        for pr in range(SSM_GW // LANES):
            xp = xs_b[:, g * SSM_GW + pr * LANES:g * SSM_GW + (pr + 1) * LANES]
            ys = []
            for hh in range(2):
                h = g * (SSM_GW // SSM_P) + 2 * pr + hh
                seg = acs[:, h:h + 1] - acsT[h:h + 1, :]
                lm = jnp.exp(jnp.where(tril, seg, NEG))
                mm = (cbm * lm * dtT[h:h + 1, :]).astype(bf16)
                ys.append(jnp.dot(mm, xp, preferred_element_type=f32))
            pair_cols.append(jnp.where(first_head, ys[0], ys[1]))
        y_cols.append((jnp.concatenate(pair_cols, axis=1), y_inter))
        new_h.append((bg, g))

    e_acs_x = jnp.dot(e_acs.astype(bf16), e_mat, preferred_element_type=f32)
    w_end_x = jnp.dot(w_end.astype(bf16), e_mat, preferred_element_type=f32)
    xw = (xs * w_end_x).astype(bf16)
    dec8 = jnp.broadcast_to(jnp.exp(acs_last), (SUBLANES, LANES))
    dec_hi = dec8.astype(bf16)
    dec_lo = (dec8 - dec_hi.astype(f32)).astype(bf16)
    dec_x = (jnp.dot(dec_hi, e_mat, preferred_element_type=f32)
             + jnp.dot(dec_lo, e_mat, preferred_element_type=f32))[0:1, :]

    y_parts = []
    for g in range(SSM_G):
        y_intra, y_inter = y_cols[g]
        sl = slice(g * SSM_GW, (g + 1) * SSM_GW)
        yg = y_intra + y_inter * e_acs_x[:, sl] + dskip_ref[:, sl] * xs[:, sl]
        yg = yg * zs_ref[bi][:, sl].astype(f32)
        y_parts.append(_rms(yg, gssm_ref[:, sl], SSM_EPS))
        bg = new_h[g][0]
        upd = lax.dot_general(bg, xw[:, sl], (((0,), (0,)), ((), ())), preferred_element_type=f32)
        hT[g] = hT[g] * dec_x[:, sl] + upd
    y_ref[bi] = jnp.concatenate(y_parts, axis=1).astype(bf16)

    @pl.when(c == pl.num_programs(1) - 1)
    def _():
        for g in range(SSM_G):
            ssmo_ref[bi, g * SSM_GW:(g + 1) * SSM_GW, :] = hT[g].T


def _ssd(xbc, zs, dt, conv0, ssm0, conv_w, conv_b, a_log, d_skip, g_ssm, e_mat):
    B, L, _ = xbc.shape
    Q = min(SSD_CHUNK, L)
    assert L % Q == 0 and Q % SUBLANES == 0
    nc = L // Q
    dtT = jnp.swapaxes(dt[:, :, :SSM_H], 1, 2)
    alog_row = jnp.pad(a_log.reshape(1, SSM_H), ((0, 0), (0, LANES - SSM_H)))
    alog_col = a_log.reshape(SSM_H, 1)
    dskip_x = jnp.repeat(d_skip, SSM_P).reshape(1, D_INNER)
    nb = SSD_SHORT_SEQS_PER_STEP if (nc == 1 and B % SSD_SHORT_SEQS_PER_STEP == 0) else 1
    kern = functools.partial(_ssd_kernel, Q=Q, nb=nb)
    cmap = lambda b, c: (0, 0)
    return pl.pallas_call(
        kern,
        out_shape=(jax.ShapeDtypeStruct((B, L, D_INNER), bf16),
                   jax.ShapeDtypeStruct((B, CONV_W - 1, CONV_DIM), f32),
                   jax.ShapeDtypeStruct((B, D_INNER, D_STATE), f32)),
        grid=(B // nb, nc),
        in_specs=[pl.BlockSpec((nb, Q, CONV_DIM), lambda b, c: (b, c, 0)),
                  pl.BlockSpec((nb, Q, D_INNER), lambda b, c: (b, c, 0)),
                  pl.BlockSpec((nb, Q, LANES), lambda b, c: (b, c, 0)),
                  pl.BlockSpec((nb, SSM_H, Q), lambda b, c: (b, 0, c)),
                  pl.BlockSpec((nb, CONV_W - 1, CONV_DIM), lambda b, c: (b, 0, 0)),
                  pl.BlockSpec((nb, D_INNER, D_STATE), lambda b, c: (b, 0, 0)),
                  pl.BlockSpec((CONV_W, CONV_DIM), cmap), pl.BlockSpec((1, CONV_DIM), cmap),
                  pl.BlockSpec((1, LANES), cmap), pl.BlockSpec((SSM_H, 1), cmap),
                  pl.BlockSpec((1, D_INNER), cmap), pl.BlockSpec((1, D_INNER), cmap),
                  pl.BlockSpec((LANES, D_INNER), cmap)],
        out_specs=(pl.BlockSpec((nb, Q, D_INNER), lambda b, c: (b, c, 0)),
                   pl.BlockSpec((nb, CONV_W - 1, CONV_DIM), lambda b, c: (b, 0, 0)),
                   pl.BlockSpec((nb, D_INNER, D_STATE), lambda b, c: (b, 0, 0))),
        scratch_shapes=[pltpu.VMEM((nb, SSM_G, D_STATE, SSM_GW), f32),
                        pltpu.VMEM((nb, SUBLANES + Q, CONV_DIM), f32)],
        compiler_params=_params(("parallel", "arbitrary")),
        name="ssd_scan",
    )(xbc, zs, dt, dtT, conv0, ssm0, conv_w, conv_b.reshape(1, CONV_DIM), alog_row, alog_col,
      dskip_x, g_ssm.reshape(1, D_INNER), e_mat)


def _route(lg):
    lane = lax.broadcasted_iota(i32, lg.shape, 1)
    lanef = lane.astype(f32)
    big = float(LANES)
    gl = jnp.where(lane < N_EGROUPS, lg, NEG)
    gmax = jnp.max(gl, axis=1, keepdims=True)
    gsel = jnp.min(jnp.where(gl == gmax, lanef, big), axis=1, keepdims=True)
    p_g = 1.0 / jnp.sum(jnp.exp(gl - gmax), axis=1, keepdims=True)
    first = N_EGROUPS + E_PER_GROUP * gsel
    emask = (lanef >= first) & (lanef < first + E_PER_GROUP)
    el = jnp.where(emask, lg, NEG)
    v1 = jnp.max(el, axis=1, keepdims=True)
    i1 = jnp.min(jnp.where(emask & (el == v1), lanef, big), axis=1, keepdims=True)
    emask2 = emask & (lanef != i1)
    el2 = jnp.where(emask2, lg, NEG)
    v2 = jnp.max(el2, axis=1, keepdims=True)
    i2 = jnp.min(jnp.where(emask2 & (el2 == v2), lanef, big), axis=1, keepdims=True)
    t = jnp.exp(v2 - v1)
    w1 = p_g / (1.0 + t)
    w2 = p_g * t / (1.0 + t)
    out = jnp.where(lane == 0, i1 - N_EGROUPS, 0.0)
    out = jnp.where(lane == 1, i2 - N_EGROUPS, out)
    out = jnp.where(lane == 2, w1, out)
    out = jnp.where(lane == 3, w2, out)
    return out


def _to_row_tiles(ref, x, rows):
    for j in range(D_MODEL // LANES):
        ref[pl.ds(j, rows, stride=SUBLANES), :] = x[:, j * LANES:(j + 1) * LANES]


def _from_row_tiles(ref, rows):
    return jnp.concatenate([ref[pl.ds(j, rows, stride=SUBLANES), :] for j in range(D_MODEL // LANES)], axis=1)


def _mix_kernel(x_ref, gt1_ref, sc2_ref, sh2_ref, o_ref, ys_ref, ga_ref, gs_ref,
                woa_ref, wos_ref, wo_ref, gpost_ref, gpre_ref, wr_ref, br_ref,
                x1_ref, u2_ref, route_ref, cnt_ref):
    tm = x_ref.shape[0]
    ya = jnp.dot(o_ref[...], woa_ref[...], preferred_element_type=f32)
    ys = jnp.dot(ys_ref[...], wos_ref[...], preferred_element_type=f32)
    m = ga_ref[...].astype(f32) * ya + gs_ref[...].astype(f32) * ys
    mo = jnp.dot(m.astype(bf16), wo_ref[...], preferred_element_type=f32)
    x1 = x_ref[...] + gt1_ref[...] * _rms(mo, gpost_ref[...], EPS)
    x1_ref[...] = x1
    u2 = _rms(x1, gpre_ref[...], EPS) * (1.0 + sc2_ref[...]) + sh2_ref[...]
    _to_row_tiles(u2_ref, u2, tm)
    lg = jnp.dot(u2.astype(bf16), wr_ref[...], preferred_element_type=f32) + br_ref[...]
    r = _route(lg)
    route_ref[...] = r

    @pl.when(pl.program_id(0) == 0)
    def _():
        cnt_ref[...] = jnp.zeros(cnt_ref.shape, f32)

    lanef = lax.broadcasted_iota(i32, r.shape, 1).astype(f32)
    hits = (lanef == r[:, 0:1]).astype(f32) + (lanef == r[:, 1:2]).astype(f32)
    cnt_ref[...] += jnp.sum(hits, axis=0, keepdims=True)


def _mix(grp, x2, o, ysn, ga, gs, w_oa, w_os, w_o, g_post1, g_pre2, w_r, b_r):
    T, tm = grp.T, grp.tm
    return pl.pallas_call(
        _mix_kernel,
        out_shape=(jax.ShapeDtypeStruct((T, D_MODEL), f32),
                   jax.ShapeDtypeStruct((T * SUBLANES, LANES), f32),
                   jax.ShapeDtypeStruct((T, LANES), f32),
                   jax.ShapeDtypeStruct((1, LANES), f32)),
        grid=(grp.n_tiles,),
        in_specs=[grp.row_spec(D_MODEL), grp.mod_spec(2), grp.mod_spec(4), grp.mod_spec(3),
                  grp.row_spec(ATTN_W), grp.row_spec(D_INNER), grp.row_spec(D_MODEL), grp.row_spec(D_MODEL),
                  _const_spec(w_oa.shape), _const_spec(w_os.shape), _const_spec(w_o.shape),
                  _const_spec((1, D_MODEL)), _const_spec((1, D_MODEL)),
                  _const_spec(w_r.shape), _const_spec((1, LANES))],
        out_specs=(grp.row_spec(D_MODEL),
                   pl.BlockSpec((tm * SUBLANES, LANES), lambda i: (i, 0)),
                   grp.row_spec(LANES),
                   pl.BlockSpec((1, LANES), lambda i: (0, 0))),
        compiler_params=_params(("arbitrary",)),
        name="mixer_out",
    )(x2, grp.mod, grp.mod, grp.mod, o, ysn, ga, gs, w_oa, w_os, w_o, g_post1, g_pre2, w_r, b_r)


def _slot_kernel(route_ref, start_ref, dest_ref, carry):
    i = pl.program_id(0)

    @pl.when(i == 0)
    def _():
        carry[...] = jnp.broadcast_to(start_ref[...], carry.shape)

    r = route_ref[...]
    tm = r.shape[0]
    lane = lax.broadcasted_iota(i32, r.shape, 1)
    lanef = lane.astype(f32)
    oh1 = (lanef == r[:, 0:1]).astype(f32)
    oh2 = (lanef == r[:, 1:2]).astype(f32)
    cnt = oh1 + oh2
    ri = lax.broadcasted_iota(i32, (tm, tm), 0)
    ci = lax.broadcasted_iota(i32, (tm, tm), 1)
    before = (ci < ri).astype(bf16)
    pos = jnp.dot(before, cnt.astype(bf16), preferred_element_type=f32) + carry[0:1, :]
    d1 = jnp.sum(oh1 * pos, axis=1, keepdims=True)
    d2 = jnp.sum(oh2 * pos, axis=1, keepdims=True)
    dest_ref[...] = jnp.where(lane == 0, d1, jnp.where(lane == 1, d2, 0.0)).astype(i32)
    carry[...] = carry[...] + jnp.sum(cnt, axis=0, keepdims=True)


def _slots(route, starts, tm):
    T = route.shape[0]
    return pl.pallas_call(
        _slot_kernel,
        out_shape=jax.ShapeDtypeStruct((T, LANES), i32),
        grid=(T // tm,),
        in_specs=[pl.BlockSpec((tm, LANES), lambda i: (i, 0)), pl.BlockSpec((1, LANES), lambda i: (0, 0))],
        out_specs=pl.BlockSpec((tm, LANES), lambda i: (i, 0)),
        scratch_shapes=[pltpu.VMEM((SUBLANES, LANES), f32)],
        compiler_params=_params(("arbitrary",)),
        name="moe_slots",
    )(route, starts)


def _row_copy(src, src_row, dst, dst_row, sem):
    return pltpu.make_async_copy(src.at[pl.ds(pl.multiple_of(src_row * SUBLANES, SUBLANES), SUBLANES), :],
                                 dst.at[pl.ds(pl.multiple_of(dst_row * SUBLANES, SUBLANES), SUBLANES), :], sem)


def _dispatch_kernel(dest_ref, u_ref, xs_in_ref, xs_ref, sem, *, tm):
    del xs_in_ref

    def issue(t, c):
        for k in range(2):
            _row_copy(u_ref, t, xs_ref, dest_ref[0, 0, 2 * t + k], sem).start(priority=k)
        return c

    lax.fori_loop(0, tm, issue, 0)
    rows = 2 * tm * SUBLANES
    pltpu.make_async_copy(xs_ref.at[pl.ds(0, rows), :], xs_ref.at[pl.ds(0, rows), :], sem).wait()


def _dispatch(dest2, u_rows, n_slots, tm):
    T = dest2.shape[0]
    xs0 = jnp.zeros((n_slots * SUBLANES, LANES), f32)
    dest3 = dest2.reshape(T // tm, 1, 2 * tm)
    return pl.pallas_call(
        functools.partial(_dispatch_kernel, tm=tm),
        out_shape=jax.ShapeDtypeStruct((n_slots * SUBLANES, LANES), f32),
        grid=(T // tm,),
        in_specs=[pl.BlockSpec((1, 1, 2 * tm), lambda i: (i, 0, 0), memory_space=pltpu.SMEM),
                  pl.BlockSpec((tm * SUBLANES, LANES), lambda i: (i, 0)), pl.BlockSpec(memory_space=pl.ANY)],
        out_specs=pl.BlockSpec(memory_space=pl.ANY),
        scratch_shapes=[pltpu.SemaphoreType.DMA(())],
        input_output_aliases={2: 0},
        compiler_params=_params(("arbitrary",)),
        name="moe_dispatch",
    )(dest3, u_rows, xs0)


def _expert_kernel(be_ref, xs_ref, wg_ref, wu_ref, wd_ref, o_ref, *, blk):
    del be_ref
    x = _from_row_tiles(xs_ref, blk).astype(bf16)
    h = _silu(jnp.dot(x, wg_ref[...], preferred_element_type=f32)) * jnp.dot(x, wu_ref[...], preferred_element_type=f32)
    _to_row_tiles(o_ref, jnp.dot(h.astype(bf16), wd_ref[...], preferred_element_type=f32), blk)


def _experts(xs, blk_e, w_gate, w_up, w_down, blk):
    n_blocks = blk_e.shape[0]
    rows = blk * SUBLANES
    grid_spec = pltpu.PrefetchScalarGridSpec(
        num_scalar_prefetch=1,
        grid=(n_blocks,),
        in_specs=[pl.BlockSpec((rows, LANES), lambda i, be: (i, 0)),
                  pl.BlockSpec((None, D_MODEL, D_FF), lambda i, be: (be[i], 0, 0)),
                  pl.BlockSpec((None, D_MODEL, D_FF), lambda i, be: (be[i], 0, 0)),
                  pl.BlockSpec((None, D_FF, D_MODEL), lambda i, be: (be[i], 0, 0))],
        out_specs=pl.BlockSpec((rows, LANES), lambda i, be: (i, 0)),
    )
    return pl.pallas_call(
        functools.partial(_expert_kernel, blk=blk),
        out_shape=jax.ShapeDtypeStruct(xs.shape, f32),
        grid_spec=grid_spec,
        compiler_params=_params(("parallel",)),
        name="moe_experts",
    )(blk_e, xs, w_gate, w_up, w_down)


def _combine_kernel(dest_ref, dest_next_ref, outs_ref, x1_ref, gt2_ref, route_ref, gpost_ref, y_ref, buf, sem, *, tm):
    i = pl.program_id(0)
    slot = jnp.bitwise_and(i, 1)

    def gather(d_ref, s):
        def issue(t, c):
            for k in range(2):
                _row_copy(outs_ref, d_ref[0, 0, 2 * t + k], buf.at[s, k], t, sem.at[s]).start()
            return c
        lax.fori_loop(0, tm, issue, 0)

    @pl.when(i == 0)
    def _():
        gather(dest_ref, 0)

    @pl.when(i + 1 < pl.num_programs(0))
    def _():
        gather(dest_next_ref, 1 - slot)

    pltpu.make_async_copy(buf.at[slot], buf.at[slot], sem.at[slot]).wait()
    r = route_ref[...]
    f = r[:, 2:3] * _from_row_tiles(buf.at[slot, 0], tm) + r[:, 3:4] * _from_row_tiles(buf.at[slot, 1], tm)
    y_ref[...] = x1_ref[...] + gt2_ref[...] * _rms(f, gpost_ref[...], EPS)


def _combine(grp, dest2, outs, x1, route, g_post2, tm):
    T = grp.T
    n = T // tm
    dest3 = dest2.reshape(n, 1, 2 * tm)
    return pl.pallas_call(
        functools.partial(_combine_kernel, tm=tm),
        out_shape=jax.ShapeDtypeStruct((T, D_MODEL), f32),
        grid=(n,),
        in_specs=[pl.BlockSpec((1, 1, 2 * tm), lambda i: (i, 0, 0), memory_space=pltpu.SMEM),
                  pl.BlockSpec((1, 1, 2 * tm), lambda i: (jnp.minimum(i + 1, n - 1), 0, 0), memory_space=pltpu.SMEM),
                  pl.BlockSpec(memory_space=pl.ANY),
                  grp.row_spec(D_MODEL), grp.mod_spec(5), grp.row_spec(LANES), _const_spec((1, D_MODEL))],
        out_specs=grp.row_spec(D_MODEL),
        scratch_shapes=[pltpu.VMEM((2, 2, tm * SUBLANES, LANES), f32), pltpu.SemaphoreType.DMA((2,))],
        compiler_params=_params(("arbitrary",)),
        name="moe_combine",
    )(dest3, dest3, outs, x1, grp.mod, route, g_post2)


def _hmoe_and_residual(grp, x1, u_rows, route, cnt, w_gate, w_up, w_down, g_post2):
    T = grp.T
    tm = grp.tm
    blk = MOE_BLOCK if 2 * T >= N_EXPERTS * MOE_BLOCK else MOE_BLOCK_SMALL
    counts = cnt[0, :N_EXPERTS].astype(i32)
    padded = (counts + blk - 1) // blk * blk
    ends = jnp.cumsum(padded)
    starts = ends - padded
    n_blocks = -(-2 * T // blk) + N_EXPERTS
    n_slots = n_blocks * blk
    blk_start = jnp.arange(n_blocks, dtype=i32) * blk
    blk_e = jnp.minimum(jnp.sum(blk_start[:, None] >= ends[None, :], axis=1), N_EXPERTS - 1).astype(i32)
    starts_row = jnp.pad(starts.astype(f32), (0, LANES - N_EXPERTS)).reshape(1, LANES)
    dest2 = _slots(route, starts_row, tm)[:, :2]
    xs = _dispatch(dest2, u_rows, n_slots, tm)
    outs = _experts(xs, blk_e, w_gate, w_up, w_down, blk)
    return _combine(grp, dest2, outs, x1, route, g_post2, tm)


def _rope_tables(pos):
    half = HEAD_DIM // 2
    inv = ROPE_THETA ** (-2.0 * jnp.arange(half, dtype=f32) / HEAD_DIM)
    ang = pos.astype(f32)[:, None] * inv[None, :]
    cos, sin = jnp.cos(ang), jnp.sin(ang)
    return jnp.tile(cos, (1, LANES // half)), jnp.tile(jnp.concatenate([-sin, sin], axis=1), (1, LANES // HEAD_DIM))


def _layer(grp, x2, lw, lam_init, attend, conv0, ssm0):
    n_seq, seq_len = grp.n_seq, grp.seq_len
    q, k, kb, v, vb = _proj_qkv(grp, x2, lw["g_pre1"], lw["w_qkv"], 1, 0)
    zs, xbc = _proj_zx(grp, x2, lw["g_pre1"], lw["w_z"], lw["w_xbc"], 1, 0)
    ga, gs, dt = _proj_gates(grp, x2, lw["g_pre1"], lw["w_gates"], lw["w_dt"], lw["dt_bias"], 1, 0)
    shp = lambda a: a.reshape(n_seq, seq_len, a.shape[-1])
    o = attend(shp(q), shp(kb), shp(vb))
    ysn, conv_new, ssm_new = _ssd(shp(xbc), shp(zs), shp(dt), conv0, ssm0.reshape(n_seq, D_INNER, D_STATE),
                                  lw["conv_w"], lw["conv_b"], lw["a_log"], lw["d_skip"], lw["g_ssm"], lw["e_mat"])
    x1, u_rows, route, cnt = _mix(grp, x2, o.reshape(grp.T, ATTN_W), ysn.reshape(grp.T, D_INNER), ga, gs,
                                  lw["w_oa"], lw["w_os"], lw["w_o"], lw["g_post1"], lw["g_pre2"], lw["w_r"], lw["b_r"])
    y2 = _hmoe_and_residual(grp, x1, u_rows, route, cnt, lw["w_gate"], lw["w_up"], lw["w_down"], lw["g_post2"])
    return (y2, k.reshape(n_seq, seq_len, N_KV, HEAD_W), v.reshape(n_seq, seq_len, N_KV, HEAD_W), conv_new,
            ssm_new.reshape(n_seq, SSM_H, SSM_P, D_STATE))


def kernel(x_prompt, x_sample, cache_k, cache_v, state_conv, state_ssm, page_table, c_prompt, c_sample, w_ada, b_ada, g_pre1, g_post1, g_pre2, g_post2, w_in, lambda_q1, lambda_k1, lambda_q2, lambda_k2, g_subln, w_oa, conv_w, conv_b, dt_bias, a_log, d_skip, g_ssm, w_os, w_o, w_rg, b_rg, w_re, b_re, w_gate, w_up, w_down):
    depth = w_in.shape[0]
    bp, lp_len, _ = x_prompt.shape
    bs, ls_len, _ = x_sample.shape
    past = page_table.shape[1] * cache_k.shape[2]
    ck = cache_k.reshape(cache_k.shape[0], cache_k.shape[1], cache_k.shape[2] * N_KV, HEAD_W)
    cv = cache_v.reshape(cache_v.shape[0], cache_v.shape[1], cache_v.shape[2] * N_KV, HEAD_W)
    rope_p = _rope_tables(jnp.arange(lp_len, dtype=i32))
    rope_s8 = _rope_tables(past + jnp.arange(ls_len, dtype=i32))
    rope_s = tuple(jnp.tile(t, (bs, 1)) for t in rope_s8)
    e_mat = (jnp.arange(LANES, dtype=i32)[:, None] == jnp.arange(D_INNER, dtype=i32)[None, :] // SSM_P).astype(bf16)

    hp = x_prompt.reshape(bp * lp_len, D_MODEL)
    hs = x_sample.reshape(bs * ls_len, D_MODEL)
    c_all = jnp.concatenate([c_prompt, c_sample], axis=0)
    outs = [[] for _ in range(8)]
    for l in range(depth):
        lam_init = 0.8 - 0.6 * math.exp(-0.3 * l)
        w_in_l = w_in[l].astype(bf16)
        lw = dict(
            w_qkv=w_in_l[:, :OFF_Z], w_z=w_in_l[:, OFF_Z:OFF_XBC], w_xbc=w_in_l[:, OFF_XBC:OFF_DT],
            w_dt=jnp.pad(w_in_l[:, OFF_DT:OFF_GA], ((0, 0), (0, LANES - SSM_H))), w_gates=w_in_l[:, OFF_GA:],
            dt_bias=jnp.pad(dt_bias[l].reshape(1, SSM_H), ((0, 0), (0, LANES - SSM_H))),
            g_pre1=g_pre1[l].reshape(1, D_MODEL), g_post1=g_post1[l].reshape(1, D_MODEL),
            g_pre2=g_pre2[l].reshape(1, D_MODEL), g_post2=g_post2[l].reshape(1, D_MODEL),
            w_oa=w_oa[l].astype(bf16), w_os=w_os[l].astype(bf16), w_o=w_o[l].astype(bf16),
            conv_w=conv_w[l], conv_b=conv_b[l], a_log=a_log[l], d_skip=d_skip[l], g_ssm=g_ssm[l], e_mat=e_mat,
            w_r=jnp.pad(jnp.concatenate([w_rg[l], w_re[l]], axis=1),
                        ((0, 0), (0, LANES - N_EGROUPS - N_EXPERTS))).astype(bf16),
            b_r=jnp.pad(jnp.concatenate([b_rg[l], b_re[l]]), (0, LANES - N_EGROUPS - N_EXPERTS)).reshape(1, LANES),
            w_gate=w_gate[l].astype(bf16), w_up=w_up[l].astype(bf16), w_down=w_down[l].astype(bf16),
        )
        lam_p = jnp.stack([lambda_q1[l], lambda_k1[l], lambda_q2[l], lambda_k2[l]])
        gsub = g_subln[l].reshape(1, HEAD_W)
        mod = _ada(c_all, w_ada[l].astype(bf16), b_ada[l])
        grp_p = _Group(bp, lp_len, mod[:bp], rope_p, per_token=False)
        grp_s = _Group(bs, ls_len, jnp.repeat(mod[bp:], ls_len, axis=0), rope_s, per_token=True)

        attend_p = lambda q, kb, vb: _attn_prompt(q, kb, vb, lam_p, gsub, lam_init)
        attend_s = lambda q, kb, vb: _attn_sample(q, kb, vb, ck, cv, page_table, l, lam_p, gsub, lam_init)
        conv0 = jnp.zeros((bp, CONV_W - 1, CONV_DIM), f32)
        ssm0 = jnp.zeros((bp, SSM_H, SSM_P, D_STATE), f32)
        hp, kp, vp, cp, sp = _layer(grp_p, hp, lw, lam_init, attend_p, conv0, ssm0)
        hs, ksm, vsm, csm, ssm = _layer(grp_s, hs, lw, lam_init, attend_s, state_conv[l], state_ssm[l])
        for lst, val in zip(outs, (kp, vp, cp, sp, ksm, vsm, csm, ssm)):
            lst.append(val)
    stacked = [jnp.stack(o) for o in outs]
    return (hp.reshape(bp, lp_len, D_MODEL), hs.reshape(bs, ls_len, D_MODEL), *stacked)
```
